```python
import math
import jax, jax.numpy as jnp
from jax import lax
import numpy as np

D_MODEL = 2048
BATCH = 4
SEQ = 2048
DEPTH = 2
DEC_BATCH = 128
DEC_SEQ = 1
PAST_LEN = 16384
PAGE_SIZE = 128

N_EVEN = (DEPTH + 1) // 2
N_ODD = DEPTH // 2
MIX_HALF = D_MODEL // 2
RET_HEADS = 4
RET_DK = MIX_HALF // RET_HEADS
RET_CHUNK = 128
HG_HEADS = 8
HG_DK = MIX_HALF // HG_HEADS
HG_DV = MIX_HALF // HG_HEADS
HG_CHUNK = 64
S5_GROUP = 16
S5_GROUPS = D_MODEL // S5_GROUP
S5_STATE = 64
D_FF = -(-8 * D_MODEL // (3 * 256)) * 256
ROPE_BASE = 10000.0
EPS = 1e-6

kernel_name = "retention_hgrn2_s5_hybrid_step"


def _rmsnorm(x, g):
    xf = x.astype(jnp.float32)
    y = xf * lax.rsqrt(jnp.mean(xf * xf, axis=-1, keepdims=True) + EPS)
    return (y * g.astype(jnp.float32)).astype(x.dtype)


def _rotary(x, pos):
    half = x.shape[-1] // 2
    inv = ROPE_BASE ** (-jnp.arange(half, dtype=jnp.float32) / half)
    ang = pos[:, None] * inv[None, :]
    cos = jnp.cos(ang)[None, :, None, :]
    sin = jnp.sin(ang)[None, :, None, :]
    x1, x2 = x[..., :half], x[..., half:]
    return jnp.concatenate([x1 * cos - x2 * sin, x1 * sin + x2 * cos], axis=-1)


def _to_chunks(t, c):
    b, l = t.shape[:2]
    return jnp.moveaxis(t.reshape((b, l // c, c) + t.shape[2:]), 1, 0)


def _from_chunks(t):
    n, b, c = t.shape[:3]
    return jnp.moveaxis(t, 0, 1).reshape((b, n * c) + t.shape[3:])


def _retention(q, k, v, s0):
    L = q.shape[1]
    c = math.gcd(L, RET_CHUNK)
    lg = jnp.log(1.0 - 2.0 ** (-5.0 - jnp.arange(RET_HEADS, dtype=jnp.float32)))
    idx = jnp.arange(c, dtype=jnp.float32)
    causal = idx[:, None] >= idx[None, :]
    intra = jnp.exp(jnp.where(causal[None], (idx[:, None] - idx[None, :])[None] * lg[:, None, None], -jnp.inf))
    q_dec = jnp.exp((idx[:, None] + 1.0) * lg[None, :])
    k_dec = jnp.exp((c - 1.0 - idx[:, None]) * lg[None, :])
    c_dec = jnp.exp(c * lg)

    def step(s, inp):
        qc, kc, vc = inp
        att = jnp.einsum('bihd,bjhd->bhij', qc, kc) * intra[None]
        o = (jnp.einsum('bhij,bjhv->bihv', att, vc)
             + jnp.einsum('bihd,bhdv->bihv', qc * q_dec[None, :, :, None], s))
        s = s * c_dec[None, :, None, None] + jnp.einsum('bjhd,bjhv->bhdv', kc * k_dec[None, :, :, None], vc)
        return s, o

    s, o = lax.scan(step, s0, (_to_chunks(q, c), _to_chunks(k, c), _to_chunks(v, c)))
    return _from_chunks(o), s


def _hgrn2(q, k, v, log_f, s0):
    L = q.shape[1]
    c = math.gcd(L, HG_CHUNK)
    idx = jnp.arange(c)
    causal = (idx[:, None] >= idx[None, :])[None, :, :, None, None]

    def step(s, inp):
        qc, kc, vc, gc = inp
        b = jnp.cumsum(gc, axis=1)
        o_cross = jnp.einsum('bihd,bhdv->bihv', qc * jnp.exp(b), s)
        rel = jnp.exp(jnp.where(causal, b[:, :, None] - b[:, None, :], -jnp.inf))
        att = jnp.einsum('bihd,bjhd,bijhd->bhij', qc, kc, rel)
        o = o_cross + jnp.einsum('bhij,bjhv->bihv', att, vc)
        b_last = b[:, -1]
        s = s * jnp.exp(b_last)[..., None] + jnp.einsum('bjhd,bjhv->bhdv', kc * jnp.exp(b_last[:, None] - b), vc)
        return s, o

    s, o = lax.scan(step, s0, (_to_chunks(q, c), _to_chunks(k, c), _to_chunks(v, c), _to_chunks(log_f, c)))
    return _from_chunks(o), s


def _complex_affine_combine(e1, e2):
    a1r, a1i, b1r, b1i = e1
    a2r, a2i, b2r, b2i = e2
    return (a1r * a2r - a1i * a2i,
            a1r * a2i + a1i * a2r,
            a2r * b1r - a2i * b1i + b2r,
            a2r * b1i + a2i * b1r + b2i)


def _s5(u, h0_re, h0_im, lam_re, lam_im, log_dt, b_re, b_im, c_re, c_im, d_skip):
    B, L, _ = u.shape
    ug = u.astype(jnp.float32).reshape(B, L, S5_GROUPS, S5_GROUP)
    lr = lam_re.astype(jnp.float32)
    li = lam_im.astype(jnp.float32)
    dt = jnp.exp(log_dt.astype(jnp.float32))[:, None]
    mag = jnp.exp(lr * dt)
    ar = mag * jnp.cos(li * dt)
    ai = mag * jnp.sin(li * dt)
    den = lr * lr + li * li
    cr = ((ar - 1.0) * lr + ai * li) / den
    ci = (ai * lr - (ar - 1.0) * li) / den
    br = b_re.astype(jnp.float32)
    bi = b_im.astype(jnp.float32)
    bbr = cr[..., None] * br - ci[..., None] * bi
    bbi = cr[..., None] * bi + ci[..., None] * br
    bu_r = jnp.einsum('blgc,gpc->blgp', ug, bbr)
    bu_i = jnp.einsum('blgc,gpc->blgp', ug, bbi)
    h0r = h0_re.astype(jnp.float32)
    h0i = h0_im.astype(jnp.float32)
    bu_r = bu_r.at[:, 0].add(ar * h0r - ai * h0i)
    bu_i = bu_i.at[:, 0].add(ar * h0i + ai * h0r)
    a_r = jnp.broadcast_to(ar, bu_r.shape)
    a_i = jnp.broadcast_to(ai, bu_i.shape)
    _, _, hr, hi = lax.associative_scan(_complex_affine_combine, (a_r, a_i, bu_r, bu_i), axis=1)
    y = (jnp.einsum('gcp,blgp->blgc', c_re.astype(jnp.float32), hr)
         - jnp.einsum('gcp,blgp->blgc', c_im.astype(jnp.float32), hi)
         + d_skip.astype(jnp.float32).reshape(S5_GROUPS, S5_GROUP) * ug)
    return y.reshape(B, L, D_MODEL), hr[:, -1], hi[:, -1]


def _even_mixer(h, pos, s_ret, s_hg, w_in, ret_gn_g, lb, hg_gn_g, w_out):
    B, L, _ = h.shape
    proj = (h @ w_in).astype(jnp.float32)
    r_q, r_k, r_v, r_g, g_q, g_f, g_i, g_g = jnp.split(proj, 8, axis=-1)
    q = _rotary(r_q.reshape(B, L, RET_HEADS, RET_DK), pos)
    k = _rotary(r_k.reshape(B, L, RET_HEADS, RET_DK), pos) * (RET_DK ** -0.5)
    o_r, s_ret_new = _retention(q, k, r_v.reshape(B, L, RET_HEADS, RET_DK), s_ret.astype(jnp.float32))
    mu = jnp.mean(o_r, axis=-1, keepdims=True)
    var = jnp.mean(jnp.square(o_r - mu), axis=-1, keepdims=True)
    o_r = ((o_r - mu) * lax.rsqrt(var + EPS)).reshape(B, L, MIX_HALF) * ret_gn_g.astype(jnp.float32) * jax.nn.silu(r_g)
    lbf = lb.astype(jnp.float32)
    f = lbf + (1.0 - lbf) * jax.nn.sigmoid(g_f)
    hd = lambda t: t.reshape(B, L, HG_HEADS, HG_DK)
    o_h, s_hg_new = _hgrn2(hd(jax.nn.silu(g_q)), hd(1.0 - f), g_i.reshape(B, L, HG_HEADS, HG_DV),
                           hd(jnp.log(f)), s_hg.astype(jnp.float32))
    o_h = o_h * lax.rsqrt(jnp.mean(o_h * o_h, axis=-1, keepdims=True) + EPS)
    o_h = o_h.reshape(B, L, MIX_HALF) * hg_gn_g.astype(jnp.float32) * jax.nn.silu(g_g)
    mix = jnp.concatenate([o_r, o_h], axis=-1).astype(h.dtype) @ w_out
    return mix, s_ret_new, s_hg_new


def _odd_mixer(h, s_re, s_im, lam_re, lam_im, log_dt, b_re, b_im, c_re, c_im, d_skip, w_a, w_b):
    y, hr, hi = _s5(h, s_re, s_im, lam_re, lam_im, log_dt, b_re, b_im, c_re, c_im, d_skip)
    z = jax.nn.gelu(y).astype(h.dtype)
    mix = (z @ w_a) * jax.nn.sigmoid(z @ w_b)
    return mix, hr, hi


def _swiglu(h, wg, wu, wd):
    return (jax.nn.silu(h @ wg) * (h @ wu)) @ wd


def _trunk(x, pos0, s_ret, s_hg, s5_re, s5_im, weights):
    (attn_norm_g, w_in, ret_gn_g, hg_lb, hg_gn_g, w_out, ssm_norm_g, s5_lam_re, s5_lam_im,
     s5_log_dt, s5_b_re, s5_b_im, s5_c_re, s5_c_im, s5_d, w_glu_a, w_glu_b, ffn_norm_g,
     w_ffn_gate, w_ffn_up, w_ffn_down, final_norm_g) = weights
    L = x.shape[1]
    pos = pos0 + jnp.arange(L, dtype=jnp.float32)
    lb_all = jnp.cumsum(jax.nn.softmax(hg_lb.astype(jnp.float32), axis=0), axis=0)
    new_ret, new_hg, new_re, new_im = [], [], [], []
    for layer in range(DEPTH):
        j = layer // 2
        if layer % 2 == 0:
            h = _rmsnorm(x, attn_norm_g[j])
            mix, sr, sh = _even_mixer(h, pos, s_ret[j], s_hg[j], w_in[j], ret_gn_g[j], lb_all[layer],
                                      hg_gn_g[j], w_out[j])
            new_ret.append(sr.astype(s_ret.dtype))
            new_hg.append(sh.astype(s_hg.dtype))
        else:
            h = _rmsnorm(x, ssm_norm_g[j])
            mix, hr, hi = _odd_mixer(h, s5_re[j], s5_im[j], s5_lam_re[j], s5_lam_im[j], s5_log_dt[j],
                                     s5_b_re[j], s5_b_im[j], s5_c_re[j], s5_c_im[j], s5_d[j],
                                     w_glu_a[j], w_glu_b[j])
            new_re.append(hr.astype(s5_re.dtype))
            new_im.append(hi.astype(s5_im.dtype))
        x = x + mix.astype(x.dtype)
        x = x + _swiglu(_rmsnorm(x, ffn_norm_g[layer]), w_ffn_gate[layer], w_ffn_up[layer],
                        w_ffn_down[layer]).astype(x.dtype)
    y = _rmsnorm(x, final_norm_g)
    return y, jnp.stack(new_ret), jnp.stack(new_hg), jnp.stack(new_re), jnp.stack(new_im)


def setup_inputs(seed: int = 0) -> dict:
    key = jax.random.key(seed)
    ks = jax.random.split(key, 32)
    f32 = jnp.float32
    nrm = lambda k, shape, scale: scale * jax.random.normal(k, shape, f32)
    gain = lambda k, shape: 1.0 + 0.05 * jax.random.normal(k, shape, f32)
    n_idx = jnp.arange(S5_STATE, dtype=f32)
    return {
        "x_prompt": nrm(ks[0], (BATCH, SEQ, D_MODEL), 1.0),
        "x_sample": nrm(ks[1], (DEC_BATCH, DEC_SEQ, D_MODEL), 1.0),
        "state_ret": nrm(ks[2], (N_EVEN, DEC_BATCH, RET_HEADS, RET_DK, RET_DK), 0.1),
        "state_hgrn": nrm(ks[3], (N_EVEN, DEC_BATCH, HG_HEADS, HG_DK, HG_DV), 0.3),
        "state_s5_re": nrm(ks[4], (N_ODD, DEC_BATCH, S5_GROUPS, S5_STATE), 0.1),
        "state_s5_im": nrm(ks[5], (N_ODD, DEC_BATCH, S5_GROUPS, S5_STATE), 0.1),
        "attn_norm_g": gain(ks[6], (N_EVEN, D_MODEL)),
        "w_in": nrm(ks[7], (N_EVEN, D_MODEL, 8 * MIX_HALF), D_MODEL ** -0.5),
        "ret_gn_g": gain(ks[8], (N_EVEN, MIX_HALF)),
        "hg_lb": nrm(ks[9], (DEPTH + 1, MIX_HALF), 0.5),
        "hg_gn_g": gain(ks[10], (N_EVEN, MIX_HALF)),
        "w_out": nrm(ks[11], (N_EVEN, D_MODEL, D_MODEL), D_MODEL ** -0.5),
        "ssm_norm_g": gain(ks[12], (N_ODD, D_MODEL)),
        "s5_lam_re": -0.5 + nrm(ks[13], (N_ODD, S5_GROUPS, S5_STATE), 0.01),
        "s5_lam_im": math.pi * n_idx + nrm(ks[14], (N_ODD, S5_GROUPS, S5_STATE), 0.01),
        "s5_log_dt": jax.random.uniform(ks[15], (N_ODD, S5_GROUPS), f32, math.log(1e-3), math.log(1e-1)),
        "s5_b_re": nrm(ks[16], (N_ODD, S5_GROUPS, S5_STATE, S5_GROUP), (2 * S5_GROUP) ** -0.5),
        "s5_b_im": nrm(ks[17], (N_ODD, S5_GROUPS, S5_STATE, S5_GROUP), (2 * S5_GROUP) ** -0.5),
        "s5_c_re": nrm(ks[18], (N_ODD, S5_GROUPS, S5_GROUP, S5_STATE), S5_STATE ** -0.5),
        "s5_c_im": nrm(ks[19], (N_ODD, S5_GROUPS, S5_GROUP, S5_STATE), S5_STATE ** -0.5),
        "s5_d": nrm(ks[20], (N_ODD, D_MODEL), 1.0),
        "w_glu_a": nrm(ks[21], (N_ODD, D_MODEL, D_MODEL), D_MODEL ** -0.5),
        "w_glu_b": nrm(ks[22], (N_ODD, D_MODEL, D_MODEL), D_MODEL ** -0.5),
        "ffn_norm_g": gain(ks[23], (DEPTH, D_MODEL)),
        "w_ffn_gate": nrm(ks[24], (DEPTH, D_MODEL, D_FF), D_MODEL ** -0.5),
        "w_ffn_up": nrm(ks[25], (DEPTH, D_MODEL, D_FF), D_MODEL ** -0.5),
        "w_ffn_down": nrm(ks[26], (DEPTH, D_FF, D_MODEL), D_FF ** -0.5),
        "final_norm_g": gain(ks[27], (D_MODEL,)),
    }


def reference(x_prompt, x_sample, state_ret, state_hgrn, state_s5_re, state_s5_im, attn_norm_g, w_in,
              ret_gn_g, hg_lb, hg_gn_g, w_out, ssm_norm_g, s5_lam_re, s5_lam_im, s5_log_dt, s5_b_re,
              s5_b_im, s5_c_re, s5_c_im, s5_d, w_glu_a, w_glu_b, ffn_norm_g, w_ffn_gate, w_ffn_up,
              w_ffn_down, final_norm_g):
    weights = (attn_norm_g, w_in, ret_gn_g, hg_lb, hg_gn_g, w_out, ssm_norm_g, s5_lam_re, s5_lam_im,
               s5_log_dt, s5_b_re, s5_b_im, s5_c_re, s5_c_im, s5_d, w_glu_a, w_glu_b, ffn_norm_g,
               w_ffn_gate, w_ffn_up, w_ffn_down, final_norm_g)
    fresh = lambda s: jnp.zeros((s.shape[0], BATCH) + s.shape[2:], s.dtype)
    y_prompt, ret_p, hg_p, s5r_p, s5i_p = _trunk(x_prompt, 0.0, fresh(state_ret), fresh(state_hgrn),
                                                 fresh(state_s5_re), fresh(state_s5_im), weights)
    y_sample, ret_s, hg_s, s5r_s, s5i_s = _trunk(x_sample, float(PAST_LEN), state_ret, state_hgrn,
                                                 state_s5_re, state_s5_im, weights)
    return (y_prompt, y_sample, ret_p, ret_s, hg_p, hg_s, s5r_p, s5i_p, s5r_s, s5i_s)
```

```python
import functools
import math

import numpy as np
import jax
import jax.numpy as jnp
from jax import lax
from jax.experimental import pallas as pl
from jax.experimental.pallas import tpu as pltpu

F32 = jnp.float32
BF16 = jnp.bfloat16

D_MODEL = 2048
BATCH = 4
SEQ = 2048
DEC_BATCH = 128
PAST_LEN = 16384
N_PROMPT = BATCH * SEQ
N_ROWS = N_PROMPT + DEC_BATCH
MIX_HALF = D_MODEL // 2
RET_HEADS = 4
RET_DK = MIX_HALF // RET_HEADS
RET_HALF = RET_DK // 2
RET_CHUNK = 128
HG_HEADS = 8
HG_DK = MIX_HALF // HG_HEADS
HG_CHUNK = 64
HG_SUB = 16
HG_ROWS = 256
S5_GROUP = 16
S5_GROUPS = D_MODEL // S5_GROUP
S5_STATE = 64
S5_BLK_GROUPS = 16
S5_BLK_CH = S5_BLK_GROUPS * S5_GROUP
S5_BLK_ST = S5_BLK_GROUPS * S5_STATE
S5_NBLK = S5_GROUPS // S5_BLK_GROUPS
S5_ROWS = 256
D_FF = 5632
ROPE_BASE = 10000.0
EPS = 1e-6
SAMPLE_TOK = 16

VMEM_LIMIT_BYTES = 56 * 1024 * 1024


def _cparams(n_axes):
    return pltpu.CompilerParams(dimension_semantics=("arbitrary",) * n_axes,
                                vmem_limit_bytes=VMEM_LIMIT_BYTES)


def _silu(x):
    return x * jax.nn.sigmoid(x)


def _rmsnorm_kernel(x_ref, g_ref, *o_refs):
    x = x_ref[...]
    y = x * lax.rsqrt(jnp.mean(x * x, axis=-1, keepdims=True) + EPS) * g_ref[...]
    for o_ref in o_refs:
        o_ref[...] = y.astype(o_ref.dtype)


def _rmsnorm(x, g, dtypes, tm=320):
    m, d = x.shape
    return pl.pallas_call(
        _rmsnorm_kernel,
        grid=(m // tm,),
        in_specs=[pl.BlockSpec((tm, d), lambda i: (i, 0)),
                  pl.BlockSpec((1, d), lambda i: (0, 0))],
        out_specs=[pl.BlockSpec((tm, d), lambda i: (i, 0)) for _ in dtypes],
        out_shape=[jax.ShapeDtypeStruct((m, d), dt) for dt in dtypes],
        compiler_params=_cparams(1),
        name="rmsnorm",
    )(x, g.reshape(1, d))


def _mm_kernel(n_w, epilogue, has_res, a_ref, *refs):
    w_refs = refs[:n_w]
    res_ref = refs[n_w] if has_res else None
    o_ref = refs[n_w + has_res]
    wb_refs = refs[n_w + has_res + 1:]

    @pl.when(pl.program_id(1) == 0)
    def _():
        for w_ref, wb_ref in zip(w_refs, wb_refs):
            wb_ref[...] = w_ref[...].astype(BF16)

    a = a_ref[...]
    y = epilogue(*[jnp.dot(a, wb_ref[...], preferred_element_type=F32) for wb_ref in wb_refs])
    if has_res:
        y = res_ref[...] + y
    o_ref[...] = y.astype(o_ref.dtype)


def _matmul(a, ws, epilogue, out_dtype, res=None, tm=640, tn=512, name="matmul"):
    m, k = a.shape
    n = ws[0].shape[1]
    in_specs = [pl.BlockSpec((tm, k), lambda j, i: (i, 0))]
    in_specs += [pl.BlockSpec((k, tn), lambda j, i: (0, j)) for _ in ws]
    args = [a, *ws]
    if res is not None:
        in_specs.append(pl.BlockSpec((tm, tn), lambda j, i: (i, j)))
        args.append(res)
    return pl.pallas_call(
        functools.partial(_mm_kernel, len(ws), epilogue, res is not None),
        grid=(n // tn, m // tm),
        in_specs=in_specs,
        out_specs=pl.BlockSpec((tm, tn), lambda j, i: (i, j)),
        out_shape=jax.ShapeDtypeStruct((m, n), out_dtype),
        scratch_shapes=[pltpu.VMEM((k, tn), BF16) for _ in ws],
        compiler_params=_cparams(2),
        name=name,
    )(*args)


def _epi_id(y):
    return y


def _epi_swiglu(g, u):
    return _silu(g) * u


def _epi_glu(a, b):
    return a * jax.nn.sigmoid(b)


def _rotary(x, cos, sin):
    x1 = x[:, :RET_HALF]
    x2 = x[:, RET_HALF:]
    return jnp.concatenate([x1 * cos - x2 * sin, x1 * sin + x2 * cos], axis=1)


def _group_norm_gate(o, gn, g):
    mu = jnp.mean(o, axis=-1, keepdims=True)
    d = o - mu
    var = jnp.mean(d * d, axis=-1, keepdims=True)
    return d * lax.rsqrt(var + EPS) * gn * _silu(g)


def _ret_prompt_kernel(q_ref, k_ref, v_ref, g_ref, cos_ref, sin_ref, intra_ref, qdec_ref, kdec_ref,
                       cdec_ref, gn_ref, o_ref, s_ref):
    @pl.when(pl.program_id(2) == 0)
    def _():
        s_ref[...] = jnp.zeros_like(s_ref)

    cos = cos_ref[...]
    sin = sin_ref[...]
    q = _rotary(q_ref[...], cos, sin)
    k = _rotary(k_ref[...], cos, sin) * (RET_DK ** -0.5)
    v = v_ref[...].astype(BF16)
    s = s_ref[0, 0]
    att = lax.dot_general(q.astype(BF16), k.astype(BF16), (((1,), (1,)), ((), ())),
                          preferred_element_type=F32) * intra_ref[0]
    o = (jnp.dot(att.astype(BF16), v, preferred_element_type=F32)
         + jnp.dot((q * qdec_ref[0]).astype(BF16), s.astype(BF16), preferred_element_type=F32))
    s_ref[0, 0] = s * cdec_ref[0, 0:1, :] + lax.dot_general(
        (k * kdec_ref[0]).astype(BF16), v, (((0,), (0,)), ((), ())), preferred_element_type=F32)
    o_ref[...] = _group_norm_gate(o, gn_ref[...], g_ref[...]).astype(o_ref.dtype)


def _ret_decay_tables(c):
    lg = np.log(1.0 - 2.0 ** (-5.0 - np.arange(RET_HEADS, dtype=np.float64)))
    idx = np.arange(c, dtype=np.float64)
    diff = idx[:, None] - idx[None, :]
    intra = np.where(diff >= 0, np.exp(np.maximum(diff, 0.0)[None] * lg[:, None, None]), 0.0)
    ones = np.ones((1, 1, RET_DK))
    qdec = np.exp((idx[None, :, None] + 1.0) * lg[:, None, None]) * ones
    kdec = np.exp((c - 1.0 - idx[None, :, None]) * lg[:, None, None]) * ones
    cdec = np.exp(c * lg)[:, None, None] * np.ones((1, 8, RET_DK))
    return [jnp.asarray(t, F32) for t in (intra, qdec, kdec, cdec)]


def _ret_prompt(proj, cos, sin, gn, batch, seq):
    c = RET_CHUNK
    nc = seq // c
    intra, qdec, kdec, cdec = _ret_decay_tables(c)
    col = lambda off: pl.BlockSpec((c, RET_DK), lambda b, h, i: (b * nc + i, off + h))
    head3 = lambda r: pl.BlockSpec((1, r, RET_DK), lambda b, h, i: (h, 0, 0))
    return pl.pallas_call(
        _ret_prompt_kernel,
        grid=(batch, RET_HEADS, nc),
        in_specs=[col(0), col(RET_HEADS), col(2 * RET_HEADS), col(3 * RET_HEADS),
                  pl.BlockSpec((c, RET_HALF), lambda b, h, i: (i, 0)),
                  pl.BlockSpec((c, RET_HALF), lambda b, h, i: (i, 0)),
                  pl.BlockSpec((1, c, c), lambda b, h, i: (h, 0, 0)),
                  head3(c), head3(c), head3(8),
                  pl.BlockSpec((1, RET_DK), lambda b, h, i: (0, h))],
        out_specs=[pl.BlockSpec((c, RET_DK), lambda b, h, i: (b * nc + i, h)),
                   pl.BlockSpec((1, 1, RET_DK, RET_DK), lambda b, h, i: (b, h, 0, 0))],
        out_shape=[jax.ShapeDtypeStruct((proj.shape[0], D_MODEL), BF16),
                   jax.ShapeDtypeStruct((batch, RET_HEADS, RET_DK, RET_DK), F32)],
        compiler_params=_cparams(3),
        name="ret_prompt",
    )(proj, proj, proj, proj, cos, sin, intra, qdec, kdec, cdec, gn.reshape(1, MIX_HALF))


def _columns(x, n):
    t = x.shape[0]
    parts = [jnp.concatenate([x[:, i:i + 128]] * (128 // t), axis=0).T for i in range(0, n, 128)]
    return parts[0] if len(parts) == 1 else jnp.concatenate(parts, axis=0)


def _ret_sample_kernel(q_ref, k_ref, v_ref, g_ref, cos_ref, sin_ref, cdec_ref, gn_ref, s_ref, mix_ref,
                       o_ref, so_ref):
    del mix_ref
    cos = cos_ref[...]
    sin = sin_ref[...]
    q = _rotary(q_ref[...], cos, sin)
    k = _rotary(k_ref[...], cos, sin) * (RET_DK ** -0.5)
    v = v_ref[...]
    qt = _columns(q, RET_DK)
    kt = _columns(k, RET_DK)
    gamma = cdec_ref[0, 0:1, :]
    rows = []
    for t in range(SAMPLE_TOK):
        s_new = s_ref[t, 0] * gamma + kt[:, t:t + 1] * v[t:t + 1, :]
        so_ref[t, 0] = s_new
        rows.append(jnp.sum(qt[:, t:t + 1] * s_new, axis=0, keepdims=True))
    o = jnp.concatenate(rows, axis=0)
    o_ref[...] = _group_norm_gate(o, gn_ref[...], g_ref[...]).astype(o_ref.dtype)


def _ret_sample(proj, mix, state, cos, sin, gn, row0):
    nb = state.shape[0]
    t = SAMPLE_TOK
    r0 = row0 // t
    _, _, _, cdec = _ret_decay_tables(1)
    col = lambda off: pl.BlockSpec((t, RET_DK), lambda b, h: (r0 + b, off + h))
    st = pl.BlockSpec((t, 1, RET_DK, RET_DK), lambda b, h: (b, h, 0, 0))
    return pl.pallas_call(
        _ret_sample_kernel,
        grid=(nb // t, RET_HEADS),
        in_specs=[col(0), col(RET_HEADS), col(2 * RET_HEADS), col(3 * RET_HEADS),
                  pl.BlockSpec((1, RET_HALF), lambda b, h: (0, 0)),
                  pl.BlockSpec((1, RET_HALF), lambda b, h: (0, 0)),
                  pl.BlockSpec((1, 8, RET_DK), lambda b, h: (h, 0, 0)),
                  pl.BlockSpec((1, RET_DK), lambda b, h: (0, h)),
                  st,
                  pl.BlockSpec(memory_space=pl.ANY)],
        out_specs=[pl.BlockSpec((t, RET_DK), lambda b, h: (r0 + b, h)), st],
        out_shape=[jax.ShapeDtypeStruct(mix.shape, mix.dtype),
                   jax.ShapeDtypeStruct(state.shape, F32)],
        input_output_aliases={9: 0},
        compiler_params=_cparams(2),
        name="ret_sample",
    )(proj, proj, proj, proj, cos, sin, cdec, gn.reshape(1, MIX_HALF), state, mix)


def _hg_lower_bound(lb_ref, layer):
    x = lb_ref[...]
    e = jnp.exp(x - jnp.max(x, axis=0, keepdims=True))
    return jnp.sum(e[:layer + 1], axis=0, keepdims=True) / jnp.sum(e, axis=0, keepdims=True)


def _hg_gates(gq, gf, lb):
    f = lb + (1.0 - lb) * jax.nn.sigmoid(gf)
    return _silu(gq), 1.0 - f, f


def _hg_out(o, gn, gg):
    return o * lax.rsqrt(jnp.mean(o * o, axis=-1, keepdims=True) + EPS) * gn * _silu(gg)


def _split3(x):
    hi = x.astype(BF16)
    r = x - hi.astype(F32)
    mid = r.astype(BF16)
    lo = (r - mid.astype(F32)).astype(BF16)
    return hi, mid, lo


def _hg_chunk(qq, kk, lf, v, st, tri):
    c = HG_CHUNK
    hi, mid, lo = _split3(lf)
    b3 = jnp.dot(tri, jnp.concatenate([hi, mid, lo], axis=1), preferred_element_type=F32)
    b = b3[:, :HG_DK] + b3[:, HG_DK:2 * HG_DK] + b3[:, 2 * HG_DK:]
    vb = v.astype(BF16)
    nt = (((1,), (1,)), ((), ()))
    o = lax.dot_general((qq * jnp.exp(b)).astype(BF16), st.astype(BF16), nt, preferred_element_type=F32)
    b_last = b[c - 1:c, :]
    khat = (kk * jnp.exp(b_last - b)).astype(BF16)
    st_new = st * jnp.exp(b_last) + lax.dot_general(vb, khat, (((0,), (0,)), ((), ())),
                                                     preferred_element_type=F32)
    row = lax.broadcasted_iota(jnp.int32, (c, HG_DK), 0)
    sub_row = lax.broadcasted_iota(jnp.int32, (HG_SUB, c), 0)
    sub_col = lax.broadcasted_iota(jnp.int32, (HG_SUB, c), 1)
    att_rows = []
    for i0 in range(0, c, HG_SUB):
        q_i = qq[i0:i0 + HG_SUB]
        b_i = b[i0:i0 + HG_SUB]
        att_i = jnp.zeros((HG_SUB, c), F32)
        if i0 > 0:
            r = b[i0 - 1:i0, :]
            q_t = (q_i * jnp.exp(b_i - r)).astype(BF16)
            k_t = jnp.where(row < i0, kk * jnp.exp(jnp.minimum(r - b, 0.0)), 0.0).astype(BF16)
            att_i = lax.dot_general(q_t, k_t, nt, preferred_element_type=F32)
        for j in range(HG_SUB):
            jj = i0 + j
            p = q_i * (kk[jj:jj + 1] * jnp.exp(jnp.minimum(b_i - b[jj:jj + 1], 0.0)))
            s_j = jnp.sum(p, axis=1, keepdims=True)
            att_i = jnp.where((sub_col == jj) & (sub_row >= j), s_j, att_i)
        att_rows.append(att_i)
    att = jnp.concatenate(att_rows, axis=0).astype(BF16)
    o = o + jnp.dot(att, vb, preferred_element_type=F32)
    return o, st_new


def _hg_prompt_kernel(layer, gq_ref, gf_ref, gi_ref, gg_ref, lb_ref, gn_ref, tri_ref, mix_ref,
                      o_ref, s_ref, st_ref):
    del mix_ref

    @pl.when(pl.program_id(2) == 0)
    def _():
        st_ref[...] = jnp.zeros_like(st_ref)

    lb = _hg_lower_bound(lb_ref, layer)
    tri = tri_ref[...]
    st = st_ref[...]
    for c0 in range(0, HG_ROWS, HG_CHUNK):
        rows = pl.ds(c0, HG_CHUNK)
        qq, kk, f = _hg_gates(gq_ref[rows, :], gf_ref[rows, :], lb)
        o, st = _hg_chunk(qq, kk, jnp.log(f), gi_ref[rows, :], st, tri)
        o_ref[rows, :] = _hg_out(o, gn_ref[...], gg_ref[rows, :]).astype(o_ref.dtype)
    st_ref[...] = st

    @pl.when(pl.program_id(2) == pl.num_programs(2) - 1)
    def _():
        s_ref[0, 0] = st.T


def _hg_prompt(proj, mix, lb_raw, gn, layer, batch, seq):
    nt = seq // HG_ROWS
    c0 = MIX_HALF * 4 // HG_DK
    col = lambda off: pl.BlockSpec((HG_ROWS, HG_DK), lambda b, h, i: (b * nt + i, c0 + off + h))
    tri = jnp.asarray(np.tril(np.ones((HG_CHUNK, HG_CHUNK))), BF16)
    return pl.pallas_call(
        functools.partial(_hg_prompt_kernel, layer),
        grid=(batch, HG_HEADS, nt),
        in_specs=[col(0), col(HG_HEADS), col(2 * HG_HEADS), col(3 * HG_HEADS),
                  pl.BlockSpec((lb_raw.shape[0], HG_DK), lambda b, h, i: (0, h)),
                  pl.BlockSpec((1, HG_DK), lambda b, h, i: (0, h)),
                  pl.BlockSpec((HG_CHUNK, HG_CHUNK), lambda b, h, i: (0, 0)),
                  pl.BlockSpec(memory_space=pl.ANY)],
        out_specs=[pl.BlockSpec((HG_ROWS, HG_DK), lambda b, h, i: (b * nt + i, HG_HEADS + h)),
                   pl.BlockSpec((1, 1, HG_DK, HG_DK), lambda b, h, i: (b, h, 0, 0))],
        out_shape=[jax.ShapeDtypeStruct(mix.shape, mix.dtype),
                   jax.ShapeDtypeStruct((batch, HG_HEADS, HG_DK, HG_DK), F32)],
        scratch_shapes=[pltpu.VMEM((HG_DK, HG_DK), F32)],
        input_output_aliases={7: 0},
        compiler_params=_cparams(3),
        name="hgrn_prompt",
    )(proj, proj, proj, proj, lb_raw, gn.reshape(1, MIX_HALF), tri, mix)


def _hg_sample_kernel(layer, gq_ref, gf_ref, gi_ref, gg_ref, lb_ref, gn_ref, s_ref, mix_ref,
                      o_ref, so_ref):
    del mix_ref
    lb = _hg_lower_bound(lb_ref, layer)
    qq, kk, f = _hg_gates(gq_ref[...], gf_ref[...], lb)
    v = gi_ref[...]
    qt = _columns(qq, HG_DK)
    kt = _columns(kk, HG_DK)
    ft = _columns(f, HG_DK)
    rows = []
    for t in range(SAMPLE_TOK):
        s_new = s_ref[t, 0] * ft[:, t:t + 1] + kt[:, t:t + 1] * v[t:t + 1, :]
        so_ref[t, 0] = s_new
        rows.append(jnp.sum(qt[:, t:t + 1] * s_new, axis=0, keepdims=True))
    o = jnp.concatenate(rows, axis=0)
    o_ref[...] = _hg_out(o, gn_ref[...], gg_ref[...]).astype(o_ref.dtype)


def _hg_sample(proj, mix, state, lb_raw, gn, layer, row0):
    nb = state.shape[0]
    t = SAMPLE_TOK
    r0 = row0 // t
    c0 = MIX_HALF * 4 // HG_DK
    col = lambda off: pl.BlockSpec((t, HG_DK), lambda b, h: (r0 + b, c0 + off + h))
    st = pl.BlockSpec((t, 1, HG_DK, HG_DK), lambda b, h: (b, h, 0, 0))
    return pl.pallas_call(
        functools.partial(_hg_sample_kernel, layer),
        grid=(nb // t, HG_HEADS),
        in_specs=[col(0), col(HG_HEADS), col(2 * HG_HEADS), col(3 * HG_HEADS),
                  pl.BlockSpec((lb_raw.shape[0], HG_DK), lambda b, h: (0, h)),
                  pl.BlockSpec((1, HG_DK), lambda b, h: (0, h)),
                  st,
                  pl.BlockSpec(memory_space=pl.ANY)],
        out_specs=[pl.BlockSpec((t, HG_DK), lambda b, h: (r0 + b, HG_HEADS + h)), st],
        out_shape=[jax.ShapeDtypeStruct(mix.shape, mix.dtype),
                   jax.ShapeDtypeStruct(state.shape, F32)],
        input_output_aliases={7: 0},
        compiler_params=_cparams(2),
        name="hgrn_sample",
    )(proj, proj, proj, proj, lb_raw, gn.reshape(1, MIX_HALF), state, mix)


S5_POW_ROWS = 16


def _cmul(ar, ai, br, bi):
    return ar * br - ai * bi, ar * bi + ai * br


def _s5_prep_kernel(lr_ref, li_ref, ldt_ref, brt_ref, bit_ref, pr_ref, pi_ref, bbr_ref, bbi_ref):
    lr = lr_ref[...]
    li = li_ref[...]
    dt = jnp.exp(ldt_ref[...])
    mag = jnp.exp(lr * dt)
    ar = mag * jnp.cos(li * dt)
    ai = mag * jnp.sin(li * dt)
    den = lr * lr + li * li
    cr = ((ar - 1.0) * lr + ai * li) / den
    ci = (ai * lr - (ar - 1.0) * li) / den
    brt = brt_ref[...]
    bit = bit_ref[...]
    bbr_ref[...] = cr * brt - ci * bit
    bbi_ref[...] = cr * bit + ci * brt
    pows = [(ar, ai)]
    for _ in range(7):
        pows.append(_cmul(*pows[-1], ar, ai))
    pows += [pows[0], pows[1], pows[3]]
    pows += [pows[0]] * (S5_POW_ROWS - len(pows))
    for i, (p_r, p_i) in enumerate(pows):
        pr_ref[i:i + 1, :] = p_r
        pi_ref[i:i + 1, :] = p_i


def _s5_prep(lam_re, lam_im, log_dt, b_re, b_im):
    n = S5_GROUPS * S5_STATE
    flat = lambda t: t.reshape(1, n)
    b_t = lambda t: jnp.transpose(t, (2, 0, 1)).reshape(S5_GROUP, n)
    ldt = jnp.repeat(log_dt, S5_STATE).reshape(1, n)
    return pl.pallas_call(
        _s5_prep_kernel,
        out_shape=[jax.ShapeDtypeStruct((S5_POW_ROWS, n), F32)] * 2
        + [jax.ShapeDtypeStruct((S5_GROUP, n), F32)] * 2,
        compiler_params=pltpu.CompilerParams(vmem_limit_bytes=VMEM_LIMIT_BYTES),
        name="s5_prep",
    )(flat(lam_re), flat(lam_im), ldt, b_t(b_re), b_t(b_im))


def _s5_block_diag(bbr, bbi, c_re, c_im):
    eye = jnp.eye(S5_BLK_GROUPS, dtype=F32)
    def b_blk(t):
        t = t.reshape(S5_GROUP, S5_NBLK, S5_BLK_GROUPS, S5_STATE)
        t = jnp.einsum('cbgp,hg->bhcgp', t, eye)
        return t.reshape(S5_NBLK, S5_BLK_CH, S5_BLK_ST).astype(BF16)
    def c_blk(t):
        t = t.reshape(S5_NBLK, S5_BLK_GROUPS, S5_GROUP, S5_STATE)
        t = jnp.einsum('bgcp,hg->bhpgc', t, eye)
        return t.reshape(S5_NBLK, S5_BLK_ST, S5_BLK_CH).astype(BF16)
    return b_blk(bbr), b_blk(bbi), c_blk(c_re), c_blk(-c_im)


def _s5_readout(u, hr, hi, cr_ref, ci_ref, d_ref):
    y = (jnp.dot(hr.astype(BF16), cr_ref[0], preferred_element_type=F32)
         + jnp.dot(hi.astype(BF16), ci_ref[0], preferred_element_type=F32)
         + d_ref[...] * u)
    return jax.nn.gelu(y)


def _s5_prompt_kernel(u_ref, br_ref, bi_ref, cr_ref, ci_ref, pr_ref, pi_ref, d_ref,
                      z_ref, hr_out, hi_out, xr_ref, xi_ref, cr_carry, ci_carry):
    tb = pl.program_id(2)

    @pl.when(tb == 0)
    def _():
        cr_carry[...] = jnp.zeros_like(cr_carry)
        ci_carry[...] = jnp.zeros_like(ci_carry)

    u = u_ref[...]
    ub = u.astype(BF16)
    xr_ref[...] = jnp.dot(ub, br_ref[0], preferred_element_type=F32)
    xi_ref[...] = jnp.dot(ub, bi_ref[0], preferred_element_type=F32)
    pw_r = pr_ref[0:8, :]
    pw_i = pi_ref[0:8, :]
    steps = [(1 << s, pr_ref[8 + s:9 + s, :], pi_ref[8 + s:9 + s, :]) for s in range(3)]
    row = lax.broadcasted_iota(jnp.int32, (8, S5_BLK_ST), 0)

    def tile(r, carry):
        rows = pl.ds(pl.multiple_of(r * 8, 8), 8)
        xr = xr_ref[rows, :]
        xi = xi_ref[rows, :]
        for sh, a_r, a_i in steps:
            sr = jnp.where(row >= sh, pltpu.roll(xr, sh, 0), 0.0)
            si = jnp.where(row >= sh, pltpu.roll(xi, sh, 0), 0.0)
            dr, di = _cmul(a_r, a_i, sr, si)
            xr, xi = xr + dr, xi + di
        hr, hi = carry
        dr, di = _cmul(pw_r, pw_i, hr, hi)
        xr, xi = xr + dr, xi + di
        xr_ref[rows, :] = xr
        xi_ref[rows, :] = xi
        return xr[7:8, :], xi[7:8, :]

    hr, hi = lax.fori_loop(0, S5_ROWS // 8, tile, (cr_carry[...], ci_carry[...]))
    cr_carry[...] = hr
    ci_carry[...] = hi
    z_ref[...] = _s5_readout(u, xr_ref[...], xi_ref[...], cr_ref, ci_ref, d_ref).astype(z_ref.dtype)

    @pl.when(tb == pl.num_programs(2) - 1)
    def _():
        hr_out[0, 0] = hr
        hi_out[0, 0] = hi


def _s5_prompt(u, mats, pows, d, batch, seq):
    b_r, b_i, c_r, c_i = mats
    p_r, p_i = pows
    nt = seq // S5_ROWS
    blk3 = lambda r, c: pl.BlockSpec((1, r, c), lambda b, j, i: (j, 0, 0))
    st_spec = pl.BlockSpec((1, 1, 1, S5_BLK_ST), lambda b, j, i: (b, j, 0, 0))
    st_shape = jax.ShapeDtypeStruct((batch, S5_NBLK, 1, S5_BLK_ST), F32)
    return pl.pallas_call(
        _s5_prompt_kernel,
        grid=(batch, S5_NBLK, nt),
        in_specs=[pl.BlockSpec((S5_ROWS, S5_BLK_CH), lambda b, j, i: (b * nt + i, j)),
                  blk3(S5_BLK_CH, S5_BLK_ST), blk3(S5_BLK_CH, S5_BLK_ST),
                  blk3(S5_BLK_ST, S5_BLK_CH), blk3(S5_BLK_ST, S5_BLK_CH),
                  pl.BlockSpec((S5_POW_ROWS, S5_BLK_ST), lambda b, j, i: (0, j)),
                  pl.BlockSpec((S5_POW_ROWS, S5_BLK_ST), lambda b, j, i: (0, j)),
                  pl.BlockSpec((1, S5_BLK_CH), lambda b, j, i: (0, j))],
        out_specs=[pl.BlockSpec((S5_ROWS, S5_BLK_CH), lambda b, j, i: (b * nt + i, j)), st_spec, st_spec],
        out_shape=[jax.ShapeDtypeStruct(u.shape, BF16), st_shape, st_shape],
        scratch_shapes=[pltpu.VMEM((S5_ROWS, S5_BLK_ST), F32), pltpu.VMEM((S5_ROWS, S5_BLK_ST), F32),
                        pltpu.VMEM((1, S5_BLK_ST), F32), pltpu.VMEM((1, S5_BLK_ST), F32)],
        compiler_params=_cparams(3),
        name="s5_prompt",
    )(u, b_r, b_i, c_r, c_i, p_r, p_i, d.reshape(1, D_MODEL))


def _s5_sample_kernel(u_ref, br_ref, bi_ref, cr_ref, ci_ref, pr_ref, pi_ref, d_ref, h0r_ref, h0i_ref,
                      z_in_ref, z_ref, hr_out, hi_out):
    del z_in_ref
    u = u_ref[...]
    ub = u.astype(BF16)
    dr, di = _cmul(pr_ref[0:1, :], pi_ref[0:1, :], h0r_ref[...], h0i_ref[...])
    hr = jnp.dot(ub, br_ref[0], preferred_element_type=F32) + dr
    hi = jnp.dot(ub, bi_ref[0], preferred_element_type=F32) + di
    hr_out[...] = hr
    hi_out[...] = hi
    z_ref[...] = _s5_readout(u, hr, hi, cr_ref, ci_ref, d_ref).astype(z_ref.dtype)


def _s5_sample(u, z, mats, pows, d, h0r, h0i, row0):
    b_r, b_i, c_r, c_i = mats
    p_r, p_i = pows
    nb = h0r.shape[0]
    r0 = row0 // nb
    blk3 = lambda r, c: pl.BlockSpec((1, r, c), lambda j: (j, 0, 0))
    st_spec = pl.BlockSpec((nb, S5_BLK_ST), lambda j: (0, j))
    st_shape = jax.ShapeDtypeStruct(h0r.shape, F32)
    return pl.pallas_call(
        _s5_sample_kernel,
        grid=(S5_NBLK,),
        in_specs=[pl.BlockSpec((nb, S5_BLK_CH), lambda j: (r0, j)),
                  blk3(S5_BLK_CH, S5_BLK_ST), blk3(S5_BLK_CH, S5_BLK_ST),
                  blk3(S5_BLK_ST, S5_BLK_CH), blk3(S5_BLK_ST, S5_BLK_CH),
                  pl.BlockSpec((S5_POW_ROWS, S5_BLK_ST), lambda j: (0, j)),
                  pl.BlockSpec((S5_POW_ROWS, S5_BLK_ST), lambda j: (0, j)),
                  pl.BlockSpec((1, S5_BLK_CH), lambda j: (0, j)),
                  st_spec, st_spec,
                  pl.BlockSpec(memory_space=pl.ANY)],
        out_specs=[pl.BlockSpec((nb, S5_BLK_CH), lambda j: (r0, j)), st_spec, st_spec],
        out_shape=[jax.ShapeDtypeStruct(z.shape, z.dtype), st_shape, st_shape],
        input_output_aliases={10: 0},
        compiler_params=_cparams(1),
        name="s5_sample",
    )(u, b_r, b_i, c_r, c_i, p_r, p_i, d.reshape(1, D_MODEL), h0r, h0i, z)


def _rope_tables(pos):
    inv = ROPE_BASE ** (-jnp.arange(RET_HALF, dtype=F32) / RET_HALF)
    ang = pos[:, None] * inv[None, :]
    return jnp.cos(ang), jnp.sin(ang)


def _ffn(x, norm_g, wg, wu, wd):
    h = _rmsnorm(x, norm_g, [BF16])[0]
    t = _matmul(h, [wg, wu], _epi_swiglu, BF16, name="ffn_gate_up")
    return _matmul(t, [wd], _epi_id, F32, res=x, tn=256, name="ffn_down")


def kernel(x_prompt, x_sample, state_ret, state_hgrn, state_s5_re, state_s5_im, attn_norm_g, w_in, ret_gn_g, hg_lb, hg_gn_g, w_out, ssm_norm_g, s5_lam_re, s5_lam_im, s5_log_dt, s5_b_re, s5_b_im, s5_c_re, s5_c_im, s5_d, w_glu_a, w_glu_b, ffn_norm_g, w_ffn_gate, w_ffn_up, w_ffn_down, final_norm_g):
    x = jnp.concatenate([x_prompt.reshape(N_PROMPT, D_MODEL), x_sample.reshape(DEC_BATCH, D_MODEL)], axis=0)
    cos_p, sin_p = _rope_tables(jnp.arange(SEQ, dtype=F32))
    cos_s, sin_s = _rope_tables(jnp.full((1,), float(PAST_LEN), F32))

    h = _rmsnorm(x, attn_norm_g[0], [BF16])[0]
    proj = _matmul(h, [w_in[0]], _epi_id, F32, name="w_in")
    mix, ret_p = _ret_prompt(proj, cos_p, sin_p, ret_gn_g[0], BATCH, SEQ)
    mix, hg_p = _hg_prompt(proj, mix, hg_lb, hg_gn_g[0], 0, BATCH, SEQ)
    mix, ret_s = _ret_sample(proj, mix, state_ret[0], cos_s, sin_s, ret_gn_g[0], N_PROMPT)
    mix, hg_s = _hg_sample(proj, mix, state_hgrn[0], hg_lb, hg_gn_g[0], 0, N_PROMPT)
    x = _matmul(mix, [w_out[0]], _epi_id, F32, res=x, name="w_out")
    x = _ffn(x, ffn_norm_g[0], w_ffn_gate[0], w_ffn_up[0], w_ffn_down[0])

    u = _rmsnorm(x, ssm_norm_g[0], [F32])[0]
    p_r, p_i, bbr, bbi = _s5_prep(s5_lam_re[0], s5_lam_im[0], s5_log_dt[0], s5_b_re[0], s5_b_im[0])
    mats = _s5_block_diag(bbr, bbi, s5_c_re[0], s5_c_im[0])
    z, s5r_p, s5i_p = _s5_prompt(u, mats, (p_r, p_i), s5_d[0], BATCH, SEQ)
    n_st = S5_GROUPS * S5_STATE
    z, s5r_s, s5i_s = _s5_sample(u, z, mats, (p_r, p_i), s5_d[0], state_s5_re[0].reshape(DEC_BATCH, n_st),
                                 state_s5_im[0].reshape(DEC_BATCH, n_st), N_PROMPT)
    x = _matmul(z, [w_glu_a[0], w_glu_b[0]], _epi_glu, F32, res=x, name="glu")
    x = _ffn(x, ffn_norm_g[1], w_ffn_gate[1], w_ffn_up[1], w_ffn_down[1])

    y = _rmsnorm(x, final_norm_g, [F32])[0]
    st = lambda t, b: t.reshape(1, b, S5_GROUPS, S5_STATE)
    return (y[:N_PROMPT].reshape(BATCH, SEQ, D_MODEL), y[N_PROMPT:].reshape(DEC_BATCH, 1, D_MODEL),
            ret_p[None], ret_s[None], hg_p[None], hg_s[None],
            st(s5r_p, BATCH), st(s5i_p, BATCH), st(s5r_s, DEC_BATCH), st(s5i_s, DEC_BATCH))
```

```python
import functools
import math

import numpy as np
import jax
import jax.numpy as jnp
from jax import lax
from jax.experimental import pallas as pl
from jax.experimental.pallas import tpu as pltpu

F32 = jnp.float32
BF16 = jnp.bfloat16

D_MODEL = 2048
BATCH = 4
SEQ = 2048
DEC_BATCH = 128
PAST_LEN = 16384
N_PROMPT = BATCH * SEQ
N_ROWS = N_PROMPT + DEC_BATCH
MIX_HALF = D_MODEL // 2
RET_HEADS = 4
RET_DK = MIX_HALF // RET_HEADS
RET_HALF = RET_DK // 2
RET_CHUNK = 128
HG_HEADS = 8
HG_DK = MIX_HALF // HG_HEADS
HG_CHUNK = 64
HG_SUB = 16
HG_ROWS = 256
S5_GROUP = 16
S5_GROUPS = D_MODEL // S5_GROUP
S5_STATE = 64
S5_BLK_GROUPS = 16
S5_BLK_CH = S5_BLK_GROUPS * S5_GROUP
S5_BLK_ST = S5_BLK_GROUPS * S5_STATE
S5_NBLK = S5_GROUPS // S5_BLK_GROUPS
S5_SUB = 256
S5_NSUB = 2
S5_ROWS = S5_SUB * S5_NSUB
D_FF = 5632
ROPE_BASE = 10000.0
EPS = 1e-6
SAMPLE_TOK = 16

VMEM_LIMIT_BYTES = 56 * 1024 * 1024


def _cparams(n_axes):
    return pltpu.CompilerParams(dimension_semantics=("arbitrary",) * n_axes,
                                vmem_limit_bytes=VMEM_LIMIT_BYTES)


def _silu(x):
    return x * jax.nn.sigmoid(x)


def _rmsnorm_kernel(x_ref, g_ref, *o_refs):
    x = x_ref[...]
    y = x * lax.rsqrt(jnp.mean(x * x, axis=-1, keepdims=True) + EPS) * g_ref[...]
    for o_ref in o_refs:
        o_ref[...] = y.astype(o_ref.dtype)


def _rmsnorm(x, g, dtypes, tm=320):
    m, d = x.shape
    return pl.pallas_call(
        _rmsnorm_kernel,
        grid=(m // tm,),
        in_specs=[pl.BlockSpec((tm, d), lambda i: (i, 0)),
                  pl.BlockSpec((1, d), lambda i: (0, 0))],
        out_specs=[pl.BlockSpec((tm, d), lambda i: (i, 0)) for _ in dtypes],
        out_shape=[jax.ShapeDtypeStruct((m, d), dt) for dt in dtypes],
        compiler_params=_cparams(1),
        name="rmsnorm",
    )(x, g.reshape(1, d))


ROW_TILE = 128


def _rmsnorm_stack_kernel(n_prompt_tiles, xp_ref, xs_ref, g_ref, h_ref, x_ref):
    x = jnp.where(pl.program_id(0) < n_prompt_tiles, xp_ref[...], xs_ref[...])
    x_ref[...] = x
    y = x * lax.rsqrt(jnp.mean(x * x, axis=-1, keepdims=True) + EPS) * g_ref[...]
    h_ref[...] = y.astype(h_ref.dtype)


def _rmsnorm_stack(x_prompt, x_sample, g):
    (mp, d), ms = x_prompt.shape, x_sample.shape[0]
    tm = ROW_TILE
    npt = mp // tm
    m = mp + ms
    return pl.pallas_call(
        functools.partial(_rmsnorm_stack_kernel, npt),
        grid=(m // tm,),
        in_specs=[pl.BlockSpec((tm, d), lambda i: (jnp.minimum(i, npt - 1), 0)),
                  pl.BlockSpec((tm, d), lambda i: (jnp.maximum(i - npt, 0), 0)),
                  pl.BlockSpec((1, d), lambda i: (0, 0))],
        out_specs=[pl.BlockSpec((tm, d), lambda i: (i, 0))] * 2,
        out_shape=[jax.ShapeDtypeStruct((m, d), BF16), jax.ShapeDtypeStruct((m, d), F32)],
        compiler_params=_cparams(1),
        name="rmsnorm_stack",
    )(x_prompt, x_sample, g.reshape(1, d))


def _rmsnorm_split_kernel(n_prompt_tiles, x_ref, g_ref, yp_ref, ys_ref):
    x = x_ref[...]
    y = x * lax.rsqrt(jnp.mean(x * x, axis=-1, keepdims=True) + EPS) * g_ref[...]
    i = pl.program_id(0)

    @pl.when(i < n_prompt_tiles)
    def _():
        yp_ref[...] = y

    @pl.when(i >= n_prompt_tiles)
    def _():
        ys_ref[...] = y


def _rmsnorm_split(x, g, n_prompt):
    m, d = x.shape
    tm = ROW_TILE
    npt = n_prompt // tm
    return pl.pallas_call(
        functools.partial(_rmsnorm_split_kernel, npt),
        grid=(m // tm,),
        in_specs=[pl.BlockSpec((tm, d), lambda i: (i, 0)),
                  pl.BlockSpec((1, d), lambda i: (0, 0))],
        out_specs=[pl.BlockSpec((tm, d), lambda i: (jnp.minimum(i, npt - 1), 0)),
                   pl.BlockSpec((tm, d), lambda i: (jnp.maximum(i - npt, 0), 0))],
        out_shape=[jax.ShapeDtypeStruct((n_prompt, d), F32), jax.ShapeDtypeStruct((m - n_prompt, d), F32)],
        compiler_params=_cparams(1),
        name="rmsnorm_split",
    )(x, g.reshape(1, d))


def _mm_kernel(n_w, epilogue, has_res, a_ref, *refs):
    w_refs = refs[:n_w]
    res_ref = refs[n_w] if has_res else None
    o_ref = refs[n_w + has_res]
    wb_refs = refs[n_w + has_res + 1:]

    @pl.when(pl.program_id(1) == 0)
    def _():
        for w_ref, wb_ref in zip(w_refs, wb_refs):
            wb_ref[...] = w_ref[...].astype(BF16)

    a = a_ref[...]
    y = epilogue(*[jnp.dot(a, wb_ref[...], preferred_element_type=F32) for wb_ref in wb_refs])
    if has_res:
        y = res_ref[...] + y
    o_ref[...] = y.astype(o_ref.dtype)


def _matmul(a, ws, epilogue, out_dtype, res=None, tm=640, tn=512, name="matmul"):
    m, k = a.shape
    n = ws[0].shape[1]
    in_specs = [pl.BlockSpec((tm, k), lambda j, i: (i, 0))]
    in_specs += [pl.BlockSpec((k, tn), lambda j, i: (0, j)) for _ in ws]
    args = [a, *ws]
    if res is not None:
        in_specs.append(pl.BlockSpec((tm, tn), lambda j, i: (i, j)))
        args.append(res)
    return pl.pallas_call(
        functools.partial(_mm_kernel, len(ws), epilogue, res is not None),
        grid=(n // tn, m // tm),
        in_specs=in_specs,
        out_specs=pl.BlockSpec((tm, tn), lambda j, i: (i, j)),
        out_shape=jax.ShapeDtypeStruct((m, n), out_dtype),
        scratch_shapes=[pltpu.VMEM((k, tn), BF16) for _ in ws],
        compiler_params=_cparams(2),
        name=name,
    )(*args)


def _epi_id(y):
    return y


def _epi_swiglu(g, u):
    return _silu(g) * u


def _epi_glu(a, b):
    return a * jax.nn.sigmoid(b)


def _rotary(x, cos, sin):
    x1 = x[:, :RET_HALF]
    x2 = x[:, RET_HALF:]
    return jnp.concatenate([x1 * cos - x2 * sin, x1 * sin + x2 * cos], axis=1)


def _group_norm_gate(o, gn, g):
    mu = jnp.mean(o, axis=-1, keepdims=True)
    d = o - mu
    var = jnp.mean(d * d, axis=-1, keepdims=True)
    return d * lax.rsqrt(var + EPS) * gn * _silu(g)


def _ret_prompt_kernel(q_ref, k_ref, v_ref, g_ref, cos_ref, sin_ref, intra_ref, qdec_ref, kdec_ref,
                       cdec_ref, gn_ref, o_ref, s_ref):
    @pl.when(pl.program_id(2) == 0)
    def _():
        s_ref[...] = jnp.zeros_like(s_ref)

    cos = cos_ref[...]
    sin = sin_ref[...]
    q = _rotary(q_ref[...], cos, sin)
    k = _rotary(k_ref[...], cos, sin) * (RET_DK ** -0.5)
    v = v_ref[...].astype(BF16)
    s = s_ref[0, 0]
    att = lax.dot_general(q.astype(BF16), k.astype(BF16), (((1,), (1,)), ((), ())),
                          preferred_element_type=F32) * intra_ref[0]
    o = (jnp.dot(att.astype(BF16), v, preferred_element_type=F32)
         + jnp.dot((q * qdec_ref[0]).astype(BF16), s.astype(BF16), preferred_element_type=F32))
    s_ref[0, 0] = s * cdec_ref[0, 0:1, :] + lax.dot_general(
        (k * kdec_ref[0]).astype(BF16), v, (((0,), (0,)), ((), ())), preferred_element_type=F32)
    o_ref[...] = _group_norm_gate(o, gn_ref[...], g_ref[...]).astype(o_ref.dtype)


def _ret_decay_tables(c):
    lg = np.log(1.0 - 2.0 ** (-5.0 - np.arange(RET_HEADS, dtype=np.float64)))
    idx = np.arange(c, dtype=np.float64)
    diff = idx[:, None] - idx[None, :]
    intra = np.where(diff >= 0, np.exp(np.maximum(diff, 0.0)[None] * lg[:, None, None]), 0.0)
    ones = np.ones((1, 1, RET_DK))
    qdec = np.exp((idx[None, :, None] + 1.0) * lg[:, None, None]) * ones
    kdec = np.exp((c - 1.0 - idx[None, :, None]) * lg[:, None, None]) * ones
    cdec = np.exp(c * lg)[:, None, None] * np.ones((1, 8, RET_DK))
    return [jnp.asarray(t, F32) for t in (intra, qdec, kdec, cdec)]


def _ret_prompt(proj, cos, sin, gn, batch, seq):
    c = RET_CHUNK
    nc = seq // c
    intra, qdec, kdec, cdec = _ret_decay_tables(c)
    col = lambda off: pl.BlockSpec((c, RET_DK), lambda b, h, i: (b * nc + i, off + h))
    head3 = lambda r: pl.BlockSpec((1, r, RET_DK), lambda b, h, i: (h, 0, 0))
    return pl.pallas_call(
        _ret_prompt_kernel,
        grid=(batch, RET_HEADS, nc),
        in_specs=[col(0), col(RET_HEADS), col(2 * RET_HEADS), col(3 * RET_HEADS),
                  pl.BlockSpec((c, RET_HALF), lambda b, h, i: (i, 0)),
                  pl.BlockSpec((c, RET_HALF), lambda b, h, i: (i, 0)),
                  pl.BlockSpec((1, c, c), lambda b, h, i: (h, 0, 0)),
                  head3(c), head3(c), head3(8),
                  pl.BlockSpec((1, RET_DK), lambda b, h, i: (0, h))],
        out_specs=[pl.BlockSpec((c, RET_DK), lambda b, h, i: (b * nc + i, h)),
                   pl.BlockSpec((1, 1, RET_DK, RET_DK), lambda b, h, i: (b, h, 0, 0))],
        out_shape=[jax.ShapeDtypeStruct((proj.shape[0], D_MODEL), BF16),
                   jax.ShapeDtypeStruct((batch, RET_HEADS, RET_DK, RET_DK), F32)],
        compiler_params=_cparams(3),
        name="ret_prompt",
    )(proj, proj, proj, proj, cos, sin, intra, qdec, kdec, cdec, gn.reshape(1, MIX_HALF))


def _columns(x, n):
    t = x.shape[0]
    parts = [jnp.concatenate([x[:, i:i + 128]] * (128 // t), axis=0).T for i in range(0, n, 128)]
    return parts[0] if len(parts) == 1 else jnp.concatenate(parts, axis=0)


def _ret_sample_kernel(q_ref, k_ref, v_ref, g_ref, cos_ref, sin_ref, cdec_ref, gn_ref, s_ref, mix_ref,
                       o_ref, so_ref):
    del mix_ref
    cos = cos_ref[...]
    sin = sin_ref[...]
    q = _rotary(q_ref[...], cos, sin)
    k = _rotary(k_ref[...], cos, sin) * (RET_DK ** -0.5)
    v = v_ref[...]
    qt = _columns(q, RET_DK)
    kt = _columns(k, RET_DK)
    gamma = cdec_ref[0, 0:1, :]
    rows = []
    for t in range(SAMPLE_TOK):
        s_new = s_ref[t, 0] * gamma + kt[:, t:t + 1] * v[t:t + 1, :]
        so_ref[t, 0] = s_new
        rows.append(jnp.sum(qt[:, t:t + 1] * s_new, axis=0, keepdims=True))
    o = jnp.concatenate(rows, axis=0)
    o_ref[...] = _group_norm_gate(o, gn_ref[...], g_ref[...]).astype(o_ref.dtype)


def _ret_sample(proj, mix, state, cos, sin, gn, row0):
    nb = state.shape[0]
    t = SAMPLE_TOK
    r0 = row0 // t
    _, _, _, cdec = _ret_decay_tables(1)
    col = lambda off: pl.BlockSpec((t, RET_DK), lambda b, h: (r0 + b, off + h))
    st = pl.BlockSpec((t, 1, RET_DK, RET_DK), lambda b, h: (b, h, 0, 0))
    return pl.pallas_call(
        _ret_sample_kernel,
        grid=(nb // t, RET_HEADS),
        in_specs=[col(0), col(RET_HEADS), col(2 * RET_HEADS), col(3 * RET_HEADS),
                  pl.BlockSpec((1, RET_HALF), lambda b, h: (0, 0)),
                  pl.BlockSpec((1, RET_HALF), lambda b, h: (0, 0)),
                  pl.BlockSpec((1, 8, RET_DK), lambda b, h: (h, 0, 0)),
                  pl.BlockSpec((1, RET_DK), lambda b, h: (0, h)),
                  st,
                  pl.BlockSpec(memory_space=pl.ANY)],
        out_specs=[pl.BlockSpec((t, RET_DK), lambda b, h: (r0 + b, h)), st],
        out_shape=[jax.ShapeDtypeStruct(mix.shape, mix.dtype),
                   jax.ShapeDtypeStruct(state.shape, F32)],
        input_output_aliases={9: 0},
        compiler_params=_cparams(2),
        name="ret_sample",
    )(proj, proj, proj, proj, cos, sin, cdec, gn.reshape(1, MIX_HALF), state, mix)


def _hg_lower_bound(lb_ref, layer):
    x = lb_ref[...]
    e = jnp.exp(x - jnp.max(x, axis=0, keepdims=True))
    return jnp.sum(e[:layer + 1], axis=0, keepdims=True) / jnp.sum(e, axis=0, keepdims=True)


def _hg_gates(gq, gf, lb):
    f = lb + (1.0 - lb) * jax.nn.sigmoid(gf)
    return _silu(gq), 1.0 - f, f


def _hg_out(o, gn, gg):
    return o * lax.rsqrt(jnp.mean(o * o, axis=-1, keepdims=True) + EPS) * gn * _silu(gg)


def _split3(x):
    hi = x.astype(BF16)
    r = x - hi.astype(F32)
    mid = r.astype(BF16)
    lo = (r - mid.astype(F32)).astype(BF16)
    return hi, mid, lo


def _hg_chunk(qq, kk, lf, v, st, tri):
    c = HG_CHUNK
    hi, mid, lo = _split3(lf)
    b3 = jnp.dot(tri, jnp.concatenate([hi, mid, lo], axis=1), preferred_element_type=F32)
    b = b3[:, :HG_DK] + b3[:, HG_DK:2 * HG_DK] + b3[:, 2 * HG_DK:]
    vb = v.astype(BF16)
    nt = (((1,), (1,)), ((), ()))
    o = lax.dot_general((qq * jnp.exp(b)).astype(BF16), st.astype(BF16), nt, preferred_element_type=F32)
    b_last = b[c - 1:c, :]
    khat = (kk * jnp.exp(b_last - b)).astype(BF16)
    st_new = st * jnp.exp(b_last) + lax.dot_general(vb, khat, (((0,), (0,)), ((), ())),
                                                     preferred_element_type=F32)
    row = lax.broadcasted_iota(jnp.int32, (c, HG_DK), 0)
    sub_row = lax.broadcasted_iota(jnp.int32, (HG_SUB, c), 0)
    sub_col = lax.broadcasted_iota(jnp.int32, (HG_SUB, c), 1)
    att_rows = []
    for i0 in range(0, c, HG_SUB):
        q_i = qq[i0:i0 + HG_SUB]
        b_i = b[i0:i0 + HG_SUB]
        att_i = jnp.zeros((HG_SUB, c), F32)
        if i0 > 0:
            r = b[i0 - 1:i0, :]
            q_t = (q_i * jnp.exp(b_i - r)).astype(BF16)
            k_t = jnp.where(row < i0, kk * jnp.exp(jnp.minimum(r - b, 0.0)), 0.0).astype(BF16)
            att_i = lax.dot_general(q_t, k_t, nt, preferred_element_type=F32)
        for j in range(HG_SUB):
            jj = i0 + j
            p = q_i * (kk[jj:jj + 1] * jnp.exp(jnp.minimum(b_i - b[jj:jj + 1], 0.0)))
            s_j = jnp.sum(p, axis=1, keepdims=True)
            att_i = jnp.where((sub_col == jj) & (sub_row >= j), s_j, att_i)
        att_rows.append(att_i)
    att = jnp.concatenate(att_rows, axis=0).astype(BF16)
    o = o + jnp.dot(att, vb, preferred_element_type=F32)
    return o, st_new


def _hg_prompt_kernel(layer, gq_ref, gf_ref, gi_ref, gg_ref, lb_ref, gn_ref, tri_ref, mix_ref,
                      o_ref, s_ref, st_ref):
    del mix_ref

    @pl.when(pl.program_id(2) == 0)
    def _():
        st_ref[...] = jnp.zeros_like(st_ref)

    lb = _hg_lower_bound(lb_ref, layer)
    tri = tri_ref[...]
    st = st_ref[...]
    for c0 in range(0, HG_ROWS, HG_CHUNK):
        rows = pl.ds(c0, HG_CHUNK)
        qq, kk, f = _hg_gates(gq_ref[rows, :], gf_ref[rows, :], lb)
        o, st = _hg_chunk(qq, kk, jnp.log(f), gi_ref[rows, :], st, tri)
        o_ref[rows, :] = _hg_out(o, gn_ref[...], gg_ref[rows, :]).astype(o_ref.dtype)
    st_ref[...] = st

    @pl.when(pl.program_id(2) == pl.num_programs(2) - 1)
    def _():
        s_ref[0, 0] = st.T


def _hg_prompt(proj, mix, lb_raw, gn, layer, batch, seq):
    nt = seq // HG_ROWS
    c0 = MIX_HALF * 4 // HG_DK
    col = lambda off: pl.BlockSpec((HG_ROWS, HG_DK), lambda b, h, i: (b * nt + i, c0 + off + h))
    tri = jnp.asarray(np.tril(np.ones((HG_CHUNK, HG_CHUNK))), BF16)
    return pl.pallas_call(
        functools.partial(_hg_prompt_kernel, layer),
        grid=(batch, HG_HEADS, nt),
        in_specs=[col(0), col(HG_HEADS), col(2 * HG_HEADS), col(3 * HG_HEADS),
                  pl.BlockSpec((lb_raw.shape[0], HG_DK), lambda b, h, i: (0, h)),
                  pl.BlockSpec((1, HG_DK), lambda b, h, i: (0, h)),
                  pl.BlockSpec((HG_CHUNK, HG_CHUNK), lambda b, h, i: (0, 0)),
                  pl.BlockSpec(memory_space=pl.ANY)],
        out_specs=[pl.BlockSpec((HG_ROWS, HG_DK), lambda b, h, i: (b * nt + i, HG_HEADS + h)),
                   pl.BlockSpec((1, 1, HG_DK, HG_DK), lambda b, h, i: (b, h, 0, 0))],
        out_shape=[jax.ShapeDtypeStruct(mix.shape, mix.dtype),
                   jax.ShapeDtypeStruct((batch, HG_HEADS, HG_DK, HG_DK), F32)],
        scratch_shapes=[pltpu.VMEM((HG_DK, HG_DK), F32)],
        input_output_aliases={7: 0},
        compiler_params=_cparams(3),
        name="hgrn_prompt",
    )(proj, proj, proj, proj, lb_raw, gn.reshape(1, MIX_HALF), tri, mix)


def _hg_sample_kernel(layer, gq_ref, gf_ref, gi_ref, gg_ref, lb_ref, gn_ref, s_ref, mix_ref,
                      o_ref, so_ref):
    del mix_ref
    lb = _hg_lower_bound(lb_ref, layer)
    qq, kk, f = _hg_gates(gq_ref[...], gf_ref[...], lb)
    v = gi_ref[...]
    qt = _columns(qq, HG_DK)
    kt = _columns(kk, HG_DK)
    ft = _columns(f, HG_DK)
    rows = []
    for t in range(SAMPLE_TOK):
        s_new = s_ref[t, 0] * ft[:, t:t + 1] + kt[:, t:t + 1] * v[t:t + 1, :]
        so_ref[t, 0] = s_new
        rows.append(jnp.sum(qt[:, t:t + 1] * s_new, axis=0, keepdims=True))
    o = jnp.concatenate(rows, axis=0)
    o_ref[...] = _hg_out(o, gn_ref[...], gg_ref[...]).astype(o_ref.dtype)


def _hg_sample(proj, mix, state, lb_raw, gn, layer, row0):
    nb = state.shape[0]
    t = SAMPLE_TOK
    r0 = row0 // t
    c0 = MIX_HALF * 4 // HG_DK
    col = lambda off: pl.BlockSpec((t, HG_DK), lambda b, h: (r0 + b, c0 + off + h))
    st = pl.BlockSpec((t, 1, HG_DK, HG_DK), lambda b, h: (b, h, 0, 0))
    return pl.pallas_call(
        functools.partial(_hg_sample_kernel, layer),
        grid=(nb // t, HG_HEADS),
        in_specs=[col(0), col(HG_HEADS), col(2 * HG_HEADS), col(3 * HG_HEADS),
                  pl.BlockSpec((lb_raw.shape[0], HG_DK), lambda b, h: (0, h)),
                  pl.BlockSpec((1, HG_DK), lambda b, h: (0, h)),
                  st,
                  pl.BlockSpec(memory_space=pl.ANY)],
        out_specs=[pl.BlockSpec((t, HG_DK), lambda b, h: (r0 + b, HG_HEADS + h)), st],
        out_shape=[jax.ShapeDtypeStruct(mix.shape, mix.dtype),
                   jax.ShapeDtypeStruct(state.shape, F32)],
        input_output_aliases={7: 0},
        compiler_params=_cparams(2),
        name="hgrn_sample",
    )(proj, proj, proj, proj, lb_raw, gn.reshape(1, MIX_HALF), state, mix)


S5_SEG = S5_SUB // 8
S5_POW_ROWS = 16
assert S5_SEG & (S5_SEG - 1) == 0


def _cmul(ar, ai, br, bi):
    return ar * br - ai * bi, ar * bi + ai * br


def _s5_prep_kernel(lr_ref, li_ref, ldt_ref, brt_ref, bit_ref, pr_ref, pi_ref, bbr_ref, bbi_ref):
    lr = lr_ref[...]
    li = li_ref[...]
    dt = jnp.exp(ldt_ref[...])
    mag = jnp.exp(lr * dt)
    ar = mag * jnp.cos(li * dt)
    ai = mag * jnp.sin(li * dt)
    den = lr * lr + li * li
    cr = ((ar - 1.0) * lr + ai * li) / den
    ci = (ai * lr - (ar - 1.0) * li) / den
    brt = brt_ref[...]
    bit = bit_ref[...]
    bbr_ref[...] = cr * brt - ci * bit
    bbi_ref[...] = cr * bit + ci * brt
    n = pr_ref.shape[1]
    pr_ref[0:8, :] = jnp.broadcast_to(ar, (8, n))
    pi_ref[0:8, :] = jnp.broadcast_to(ai, (8, n))
    seg_pow = (ar, ai)
    for _ in range(S5_SEG.bit_length() - 1):
        seg_pow = _cmul(*seg_pow, *seg_pow)
    p = seg_pow
    for m in range(8):
        pr_ref[8 + m:9 + m, :] = p[0]
        pi_ref[8 + m:9 + m, :] = p[1]
        p = _cmul(*p, *seg_pow)


def _s5_prep(lam_re, lam_im, log_dt, b_re, b_im):
    n = S5_GROUPS * S5_STATE
    flat = lambda t: t.reshape(1, n)
    b_t = lambda t: jnp.transpose(t, (2, 0, 1)).reshape(S5_GROUP, n)
    ldt = jnp.repeat(log_dt, S5_STATE).reshape(1, n)
    return pl.pallas_call(
        _s5_prep_kernel,
        out_shape=[jax.ShapeDtypeStruct((S5_POW_ROWS, n), F32)] * 2
        + [jax.ShapeDtypeStruct((S5_GROUP, n), F32)] * 2,
        compiler_params=pltpu.CompilerParams(vmem_limit_bytes=VMEM_LIMIT_BYTES),
        name="s5_prep",
    )(flat(lam_re), flat(lam_im), ldt, b_t(b_re), b_t(b_im))


def _s5_block_diag(bbr, bbi, c_re, c_im):
    eye = jnp.eye(S5_BLK_GROUPS, dtype=F32)
    def b_blk(t):
        t = t.reshape(S5_GROUP, S5_NBLK, S5_BLK_GROUPS, S5_STATE)
        t = jnp.einsum('cbgp,hg->bhcgp', t, eye)
        return t.reshape(S5_NBLK, S5_BLK_CH, S5_BLK_ST).astype(BF16)
    def c_blk(t):
        t = t.reshape(S5_NBLK, S5_BLK_GROUPS, S5_GROUP, S5_STATE)
        t = jnp.einsum('bgcp,hg->bhpgc', t, eye)
        return t.reshape(S5_NBLK, S5_BLK_ST, S5_BLK_CH).astype(BF16)
    return (jnp.concatenate([b_blk(bbr), b_blk(bbi)], axis=2),
            jnp.concatenate([c_blk(c_re), c_blk(-c_im)], axis=1))


def _s5_readout(u, h_bf16, c_ref, d_ref):
    y = jnp.dot(h_bf16, c_ref[0], preferred_element_type=F32) + d_ref[...] * u
    return jax.nn.gelu(y)


def _s5_row_perm(rows):
    seg = rows // 8
    p = np.zeros((rows, rows), np.float32)
    t, s = np.meshgrid(np.arange(seg), np.arange(8), indexing="ij")
    p[(t * 8 + s).ravel(), (s * seg + t).ravel()] = 1.0
    return jnp.asarray(p, BF16), jnp.asarray(p.T, BF16)


def _s5_prompt_kernel(u_ref, p_ref, pt_ref, b_ref, c_ref, pr_ref, pi_ref, d_ref,
                      z_ref, hr_out, hi_out, *scratch):
    x_refs = scratch[:S5_NSUB]
    hb_refs = scratch[S5_NSUB:2 * S5_NSUB]
    us_ref, cr_carry, ci_carry = scratch[2 * S5_NSUB:]
    tb = pl.program_id(2)
    n = S5_BLK_ST
    ch = S5_BLK_CH

    @pl.when(tb == 0)
    def _():
        cr_carry[...] = jnp.zeros_like(cr_carry)
        ci_carry[...] = jnp.zeros_like(ci_carry)

    for q in range(S5_NSUB):
        rows = slice(q * S5_SUB, (q + 1) * S5_SUB)
        up = jnp.dot(p_ref[...], jnp.concatenate(_split3(u_ref[rows, :]), axis=1),
                     preferred_element_type=F32)
        us_ref[rows, :] = (up[:, :ch] + up[:, ch:2 * ch]) + up[:, 2 * ch:]
        x_refs[q][...] = jnp.dot(up[:, :ch].astype(BF16), b_ref[0], preferred_element_type=F32)

    a_r = pr_ref[0:8, :]
    a_i = pi_ref[0:8, :]
    seg0 = 8
    row = lax.broadcasted_iota(jnp.int32, (8, n), 0)
    h0r = cr_carry[...]
    h0i = ci_carry[...]
    for q in range(S5_NSUB):
        x_ref = x_refs[q]
        def advance(t, h, x_ref=x_ref):
            rows = pl.ds(pl.multiple_of(t * 8, 8), 8)
            dr, di = _cmul(a_r, a_i, *h)
            return dr + x_ref[rows, 0:n], di + x_ref[rows, n:2 * n]

        zero = jnp.zeros((8, n), F32)
        er, ei = lax.fori_loop(0, S5_SEG, advance, (zero, zero), unroll=4)
        for s in range(3):
            m = 1 << s
            g_r = pr_ref[seg0 + m - 1:seg0 + m, :]
            g_i = pi_ref[seg0 + m - 1:seg0 + m, :]
            sr = jnp.where(row >= m, pltpu.roll(er, m, 0), 0.0)
            si = jnp.where(row >= m, pltpu.roll(ei, m, 0), 0.0)
            dr, di = _cmul(g_r, g_i, sr, si)
            er, ei = er + dr, ei + di
        dr, di = _cmul(pr_ref[seg0:seg0 + 8, :], pi_ref[seg0:seg0 + 8, :], h0r, h0i)
        er, ei = er + dr, ei + di
        cm_r = jnp.where(row >= 1, pltpu.roll(er, 1, 0), h0r)
        cm_i = jnp.where(row >= 1, pltpu.roll(ei, 1, 0), h0i)
        h0r = er[7:8, :]
        h0i = ei[7:8, :]
        def emit(t2, h, hb_ref=hb_refs[q]):
            h1 = advance(2 * t2, h)
            h2 = advance(2 * t2 + 1, h1)
            rows = pl.ds(pl.multiple_of(t2 * 16, 16), 16)
            hb_ref[rows, 0:n] = jnp.concatenate([h1[0], h2[0]], axis=0).astype(BF16)
            hb_ref[rows, n:2 * n] = jnp.concatenate([h1[1], h2[1]], axis=0).astype(BF16)
            return h2

        lax.fori_loop(0, S5_SEG // 2, emit, (cm_r, cm_i), unroll=2)
        rows = slice(q * S5_SUB, (q + 1) * S5_SUB)
        zp = _s5_readout(us_ref[rows, :], hb_refs[q][...], c_ref, d_ref).astype(BF16)
        z_ref[rows, :] = jnp.dot(pt_ref[...], zp, preferred_element_type=F32).astype(z_ref.dtype)
    cr_carry[...] = h0r
    ci_carry[...] = h0i

    @pl.when(tb == pl.num_programs(2) - 1)
    def _():
        hr_out[0, 0] = h0r
        hi_out[0, 0] = h0i


def _s5_prompt(u, mats, pows, d, batch, seq):
    b_m, c_m = mats
    p_r, p_i = pows
    perm, perm_t = _s5_row_perm(S5_SUB)
    nt = seq // S5_ROWS
    blk3 = lambda r, c: pl.BlockSpec((1, r, c), lambda b, j, i: (j, 0, 0))
    whole = pl.BlockSpec((S5_SUB, S5_SUB), lambda b, j, i: (0, 0))
    st_spec = pl.BlockSpec((1, 1, 1, S5_BLK_ST), lambda b, j, i: (b, j, 0, 0))
    st_shape = jax.ShapeDtypeStruct((batch, S5_NBLK, 1, S5_BLK_ST), F32)
    return pl.pallas_call(
        _s5_prompt_kernel,
        grid=(batch, S5_NBLK, nt),
        in_specs=[pl.BlockSpec((S5_ROWS, S5_BLK_CH), lambda b, j, i: (b * nt + i, j)),
                  whole, whole,
                  blk3(S5_BLK_CH, 2 * S5_BLK_ST), blk3(2 * S5_BLK_ST, S5_BLK_CH),
                  pl.BlockSpec((S5_POW_ROWS, S5_BLK_ST), lambda b, j, i: (0, j)),
                  pl.BlockSpec((S5_POW_ROWS, S5_BLK_ST), lambda b, j, i: (0, j)),
                  pl.BlockSpec((1, S5_BLK_CH), lambda b, j, i: (0, j))],
        out_specs=[pl.BlockSpec((S5_ROWS, S5_BLK_CH), lambda b, j, i: (b * nt + i, j)), st_spec, st_spec],
        out_shape=[jax.ShapeDtypeStruct(u.shape, BF16), st_shape, st_shape],
        scratch_shapes=[pltpu.VMEM((S5_SUB, 2 * S5_BLK_ST), F32) for _ in range(S5_NSUB)]
        + [pltpu.VMEM((S5_SUB, 2 * S5_BLK_ST), BF16) for _ in range(S5_NSUB)]
        + [pltpu.VMEM((S5_ROWS, S5_BLK_CH), F32),
           pltpu.VMEM((1, S5_BLK_ST), F32), pltpu.VMEM((1, S5_BLK_ST), F32)],
        compiler_params=_cparams(3),
        name="s5_prompt",
    )(u, perm, perm_t, b_m, c_m, p_r, p_i, d.reshape(1, D_MODEL))


def _s5_sample_kernel(u_ref, b_ref, c_ref, pr_ref, pi_ref, d_ref, h0r_ref, h0i_ref,
                      z_in_ref, z_ref, hr_out, hi_out):
    del z_in_ref
    n = S5_BLK_ST
    u = u_ref[...]
    x = jnp.dot(u.astype(BF16), b_ref[0], preferred_element_type=F32)
    dr, di = _cmul(pr_ref[0:1, :], pi_ref[0:1, :], h0r_ref[...], h0i_ref[...])
    hr = x[:, 0:n] + dr
    hi = x[:, n:2 * n] + di
    hr_out[...] = hr
    hi_out[...] = hi
    h = jnp.concatenate([hr, hi], axis=1).astype(BF16)
    z_ref[...] = _s5_readout(u, h, c_ref, d_ref).astype(z_ref.dtype)


def _s5_sample(u, z, mats, pows, d, h0r, h0i, row0):
    b_m, c_m = mats
    p_r, p_i = pows
    nb = h0r.shape[0]
    r0 = row0 // nb
    blk3 = lambda r, c: pl.BlockSpec((1, r, c), lambda j: (j, 0, 0))
    st_spec = pl.BlockSpec((nb, S5_BLK_ST), lambda j: (0, j))
    st_shape = jax.ShapeDtypeStruct(h0r.shape, F32)
    return pl.pallas_call(
        _s5_sample_kernel,
        grid=(S5_NBLK,),
        in_specs=[pl.BlockSpec((nb, S5_BLK_CH), lambda j: (r0, j)),
                  blk3(S5_BLK_CH, 2 * S5_BLK_ST), blk3(2 * S5_BLK_ST, S5_BLK_CH),
                  pl.BlockSpec((S5_POW_ROWS, S5_BLK_ST), lambda j: (0, j)),
                  pl.BlockSpec((S5_POW_ROWS, S5_BLK_ST), lambda j: (0, j)),
                  pl.BlockSpec((1, S5_BLK_CH), lambda j: (0, j)),
                  st_spec, st_spec,
                  pl.BlockSpec(memory_space=pl.ANY)],
        out_specs=[pl.BlockSpec((nb, S5_BLK_CH), lambda j: (r0, j)), st_spec, st_spec],
        out_shape=[jax.ShapeDtypeStruct(z.shape, z.dtype), st_shape, st_shape],
        input_output_aliases={8: 0},
        compiler_params=_cparams(1),
        name="s5_sample",
    )(u, b_m, c_m, p_r, p_i, d.reshape(1, D_MODEL), h0r, h0i, z)


def _rope_tables(pos):
    inv = ROPE_BASE ** (-jnp.arange(RET_HALF, dtype=F32) / RET_HALF)
    ang = pos[:, None] * inv[None, :]
    return jnp.cos(ang), jnp.sin(ang)


def _ffn(x, norm_g, wg, wu, wd):
    h = _rmsnorm(x, norm_g, [BF16])[0]
    t = _matmul(h, [wg, wu], _epi_swiglu, BF16, name="ffn_gate_up")
    return _matmul(t, [wd], _epi_id, F32, res=x, tn=256, name="ffn_down")


def kernel(x_prompt, x_sample, state_ret, state_hgrn, state_s5_re, state_s5_im, attn_norm_g, w_in, ret_gn_g, hg_lb, hg_gn_g, w_out, ssm_norm_g, s5_lam_re, s5_lam_im, s5_log_dt, s5_b_re, s5_b_im, s5_c_re, s5_c_im, s5_d, w_glu_a, w_glu_b, ffn_norm_g, w_ffn_gate, w_ffn_up, w_ffn_down, final_norm_g):
    cos_p, sin_p = _rope_tables(jnp.arange(SEQ, dtype=F32))
    cos_s, sin_s = _rope_tables(jnp.full((1,), float(PAST_LEN), F32))

    h, x = _rmsnorm_stack(x_prompt.reshape(N_PROMPT, D_MODEL), x_sample.reshape(DEC_BATCH, D_MODEL),
                          attn_norm_g[0])
    proj = _matmul(h, [w_in[0]], _epi_id, F32, name="w_in")
    mix, ret_p = _ret_prompt(proj, cos_p, sin_p, ret_gn_g[0], BATCH, SEQ)
    mix, hg_p = _hg_prompt(proj, mix, hg_lb, hg_gn_g[0], 0, BATCH, SEQ)
    mix, ret_s = _ret_sample(proj, mix, state_ret[0], cos_s, sin_s, ret_gn_g[0], N_PROMPT)
    mix, hg_s = _hg_sample(proj, mix, state_hgrn[0], hg_lb, hg_gn_g[0], 0, N_PROMPT)
    x = _matmul(mix, [w_out[0]], _epi_id, F32, res=x, name="w_out")
    x = _ffn(x, ffn_norm_g[0], w_ffn_gate[0], w_ffn_up[0], w_ffn_down[0])

    u = _rmsnorm(x, ssm_norm_g[0], [F32])[0]
    p_r, p_i, bbr, bbi = _s5_prep(s5_lam_re[0], s5_lam_im[0], s5_log_dt[0], s5_b_re[0], s5_b_im[0])
    mats = _s5_block_diag(bbr, bbi, s5_c_re[0], s5_c_im[0])
    z, s5r_p, s5i_p = _s5_prompt(u, mats, (p_r, p_i), s5_d[0], BATCH, SEQ)
    n_st = S5_GROUPS * S5_STATE
    z, s5r_s, s5i_s = _s5_sample(u, z, mats, (p_r, p_i), s5_d[0], state_s5_re[0].reshape(DEC_BATCH, n_st),
                                 state_s5_im[0].reshape(DEC_BATCH, n_st), N_PROMPT)
    x = _matmul(z, [w_glu_a[0], w_glu_b[0]], _epi_glu, F32, res=x, name="glu")
    x = _ffn(x, ffn_norm_g[1], w_ffn_gate[1], w_ffn_up[1], w_ffn_down[1])

    y_p, y_s = _rmsnorm_split(x, final_norm_g, N_PROMPT)
    st = lambda t, b: t.reshape(1, b, S5_GROUPS, S5_STATE)
    return (y_p.reshape(BATCH, SEQ, D_MODEL), y_s.reshape(DEC_BATCH, 1, D_MODEL),
            ret_p[None], ret_s[None], hg_p[None], hg_s[None],
            st(s5r_p, BATCH), st(s5i_p, BATCH), st(s5r_s, DEC_BATCH), st(s5i_s, DEC_BATCH))
```

```python
import functools
import math

import numpy as np
import jax
import jax.numpy as jnp
from jax import lax
from jax.experimental import pallas as pl
from jax.experimental.pallas import tpu as pltpu

F32 = jnp.float32
BF16 = jnp.bfloat16

D_MODEL = 2048
BATCH = 4
SEQ = 2048
DEC_BATCH = 128
PAST_LEN = 16384
N_PROMPT = BATCH * SEQ
N_ROWS = N_PROMPT + DEC_BATCH
MIX_HALF = D_MODEL // 2
RET_HEADS = 4
RET_DK = MIX_HALF // RET_HEADS
RET_HALF = RET_DK // 2
RET_CHUNK = 128
HG_HEADS = 8
HG_DK = MIX_HALF // HG_HEADS
HG_CHUNK = 64
HG_SUB = 16
HG_ROWS = 256
S5_GROUP = 16
S5_GROUPS = D_MODEL // S5_GROUP
S5_STATE = 64
S5_BLK_GROUPS = 16
S5_BLK_CH = S5_BLK_GROUPS * S5_GROUP
S5_BLK_ST = S5_BLK_GROUPS * S5_STATE
S5_NBLK = S5_GROUPS // S5_BLK_GROUPS
S5_SUB = 256
S5_NSUB = 2
S5_ROWS = S5_SUB * S5_NSUB
D_FF = 5632
ROPE_BASE = 10000.0
EPS = 1e-6
SAMPLE_TOK = 16

VMEM_LIMIT_BYTES = 56 * 1024 * 1024


def _cparams(n_axes):
    return pltpu.CompilerParams(dimension_semantics=("arbitrary",) * n_axes,
                                vmem_limit_bytes=VMEM_LIMIT_BYTES)


def _silu(x):
    return x * jax.nn.sigmoid(x)


def _rmsnorm_kernel(x_ref, g_ref, *o_refs):
    x = x_ref[...]
    y = x * lax.rsqrt(jnp.mean(x * x, axis=-1, keepdims=True) + EPS) * g_ref[...]
    for o_ref in o_refs:
        o_ref[...] = y.astype(o_ref.dtype)


def _rmsnorm(x, g, dtypes, tm=320):
    m, d = x.shape
    return pl.pallas_call(
        _rmsnorm_kernel,
        grid=(m // tm,),
        in_specs=[pl.BlockSpec((tm, d), lambda i: (i, 0)),
                  pl.BlockSpec((1, d), lambda i: (0, 0))],
        out_specs=[pl.BlockSpec((tm, d), lambda i: (i, 0)) for _ in dtypes],
        out_shape=[jax.ShapeDtypeStruct((m, d), dt) for dt in dtypes],
        compiler_params=_cparams(1),
        name="rmsnorm",
    )(x, g.reshape(1, d))


ROW_TILE = 512


def _rms(x, g):
    return x * lax.rsqrt(jnp.mean(x * x, axis=-1, keepdims=True) + EPS) * g


def _rmsnorm_stack_kernel(n_prompt_tiles, n_sample, xp_ref, xs_ref, g_ref, h_ref, x_ref):
    i = pl.program_id(0)

    @pl.when(i < n_prompt_tiles)
    def _():
        x = xp_ref[...]
        x_ref[...] = x
        h_ref[...] = _rms(x, g_ref[...]).astype(h_ref.dtype)

    @pl.when(i >= n_prompt_tiles)
    def _():
        x = xs_ref[...]
        x_ref[0:n_sample, :] = x
        h_ref[0:n_sample, :] = _rms(x, g_ref[...]).astype(h_ref.dtype)


def _rmsnorm_stack(x_prompt, x_sample, g):
    (mp, d), ms = x_prompt.shape, x_sample.shape[0]
    tm = ROW_TILE
    npt = mp // tm
    m = mp + ms
    return pl.pallas_call(
        functools.partial(_rmsnorm_stack_kernel, npt, ms),
        grid=(npt + 1,),
        in_specs=[pl.BlockSpec((tm, d), lambda i: (jnp.minimum(i, npt - 1), 0)),
                  pl.BlockSpec((ms, d), lambda i: (0, 0)),
                  pl.BlockSpec((1, d), lambda i: (0, 0))],
        out_specs=[pl.BlockSpec((tm, d), lambda i: (i, 0))] * 2,
        out_shape=[jax.ShapeDtypeStruct((m, d), BF16), jax.ShapeDtypeStruct((m, d), F32)],
        compiler_params=_cparams(1),
        name="rmsnorm_stack",
    )(x_prompt, x_sample, g.reshape(1, d))


def _rmsnorm_split_kernel(n_prompt_tiles, n_sample, x_ref, g_ref, yp_ref, ys_ref):
    i = pl.program_id(0)

    @pl.when(i < n_prompt_tiles)
    def _():
        yp_ref[...] = _rms(x_ref[...], g_ref[...])

    @pl.when(i >= n_prompt_tiles)
    def _():
        ys_ref[...] = _rms(x_ref[0:n_sample, :], g_ref[...])


def _rmsnorm_split(x, g, n_prompt):
    m, d = x.shape
    tm = ROW_TILE
    npt = n_prompt // tm
    ms = m - n_prompt
    return pl.pallas_call(
        functools.partial(_rmsnorm_split_kernel, npt, ms),
        grid=(npt + 1,),
        in_specs=[pl.BlockSpec((tm, d), lambda i: (i, 0)),
                  pl.BlockSpec((1, d), lambda i: (0, 0))],
        out_specs=[pl.BlockSpec((tm, d), lambda i: (jnp.minimum(i, npt - 1), 0)),
                   pl.BlockSpec((ms, d), lambda i: (0, 0))],
        out_shape=[jax.ShapeDtypeStruct((n_prompt, d), F32), jax.ShapeDtypeStruct((ms, d), F32)],
        compiler_params=_cparams(1),
        name="rmsnorm_split",
    )(x, g.reshape(1, d))


def _mm_kernel(n_w, epilogue, has_res, a_ref, *refs):
    w_refs = refs[:n_w]
    res_ref = refs[n_w] if has_res else None
    o_ref = refs[n_w + has_res]
    wb_refs = refs[n_w + has_res + 1:]

    @pl.when(pl.program_id(1) == 0)
    def _():
        for w_ref, wb_ref in zip(w_refs, wb_refs):
            wb_ref[...] = w_ref[...].astype(BF16)

    a = a_ref[...]
    y = epilogue(*[jnp.dot(a, wb_ref[...], preferred_element_type=F32) for wb_ref in wb_refs])
    if has_res:
        y = res_ref[...] + y
    o_ref[...] = y.astype(o_ref.dtype)


def _matmul(a, ws, epilogue, out_dtype, res=None, tm=640, tn=512, name="matmul"):
    m, k = a.shape
    n = ws[0].shape[1]
    in_specs = [pl.BlockSpec((tm, k), lambda j, i: (i, 0))]
    in_specs += [pl.BlockSpec((k, tn), lambda j, i: (0, j)) for _ in ws]
    args = [a, *ws]
    if res is not None:
        in_specs.append(pl.BlockSpec((tm, tn), lambda j, i: (i, j)))
        args.append(res)
    return pl.pallas_call(
        functools.partial(_mm_kernel, len(ws), epilogue, res is not None),
        grid=(n // tn, m // tm),
        in_specs=in_specs,
        out_specs=pl.BlockSpec((tm, tn), lambda j, i: (i, j)),
        out_shape=jax.ShapeDtypeStruct((m, n), out_dtype),
        scratch_shapes=[pltpu.VMEM((k, tn), BF16) for _ in ws],
        compiler_params=_cparams(2),
        name=name,
    )(*args)


def _epi_id(y):
    return y


def _epi_swiglu(g, u):
    return _silu(g) * u


def _epi_glu(a, b):
    return a * jax.nn.sigmoid(b)


def _rotary(x, cos, sin):
    x1 = x[:, :RET_HALF]
    x2 = x[:, RET_HALF:]
    return jnp.concatenate([x1 * cos - x2 * sin, x1 * sin + x2 * cos], axis=1)


def _group_norm_gate(o, gn, g):
    mu = jnp.mean(o, axis=-1, keepdims=True)
    d = o - mu
    var = jnp.mean(d * d, axis=-1, keepdims=True)
    return d * lax.rsqrt(var + EPS) * gn * _silu(g)


def _ret_prompt_kernel(q_ref, k_ref, v_ref, g_ref, cos_ref, sin_ref, intra_ref, qdec_ref, kdec_ref,
                       cdec_ref, gn_ref, mix_ref, o_ref, s_ref):
    del mix_ref

    @pl.when(pl.program_id(2) == 0)
    def _():
        s_ref[...] = jnp.zeros_like(s_ref)

    cos = cos_ref[...]
    sin = sin_ref[...]
    q = _rotary(q_ref[...], cos, sin)
    k = _rotary(k_ref[...], cos, sin) * (RET_DK ** -0.5)
    v = v_ref[...].astype(BF16)
    s = s_ref[0, 0]
    att = lax.dot_general(q.astype(BF16), k.astype(BF16), (((1,), (1,)), ((), ())),
                          preferred_element_type=F32) * intra_ref[0]
    o = (jnp.dot(att.astype(BF16), v, preferred_element_type=F32)
         + jnp.dot((q * qdec_ref[0]).astype(BF16), s.astype(BF16), preferred_element_type=F32))
    s_ref[0, 0] = s * cdec_ref[0, 0:1, :] + lax.dot_general(
        (k * kdec_ref[0]).astype(BF16), v, (((0,), (0,)), ((), ())), preferred_element_type=F32)
    o_ref[...] = _group_norm_gate(o, gn_ref[...], g_ref[...]).astype(o_ref.dtype)


def _ret_decay_tables(c):
    lg = np.log(1.0 - 2.0 ** (-5.0 - np.arange(RET_HEADS, dtype=np.float64)))
    idx = np.arange(c, dtype=np.float64)
    diff = idx[:, None] - idx[None, :]
    intra = np.where(diff >= 0, np.exp(np.maximum(diff, 0.0)[None] * lg[:, None, None]), 0.0)
    ones = np.ones((1, 1, RET_DK))
    qdec = np.exp((idx[None, :, None] + 1.0) * lg[:, None, None]) * ones
    kdec = np.exp((c - 1.0 - idx[None, :, None]) * lg[:, None, None]) * ones
    cdec = np.exp(c * lg)[:, None, None] * np.ones((1, 8, RET_DK))
    return [jnp.asarray(t, F32) for t in (intra, qdec, kdec, cdec)]


def _ret_prompt(proj, mix, cos, sin, gn, batch, seq):
    c = RET_CHUNK
    nc = seq // c
    intra, qdec, kdec, cdec = _ret_decay_tables(c)
    col = lambda off: pl.BlockSpec((c, RET_DK), lambda b, h, i: (b * nc + i, off + h))
    head3 = lambda r: pl.BlockSpec((1, r, RET_DK), lambda b, h, i: (h, 0, 0))
    return pl.pallas_call(
        _ret_prompt_kernel,
        grid=(batch, RET_HEADS, nc),
        in_specs=[col(0), col(RET_HEADS), col(2 * RET_HEADS), col(3 * RET_HEADS),
                  pl.BlockSpec((c, RET_HALF), lambda b, h, i: (i, 0)),
                  pl.BlockSpec((c, RET_HALF), lambda b, h, i: (i, 0)),
                  pl.BlockSpec((1, c, c), lambda b, h, i: (h, 0, 0)),
                  head3(c), head3(c), head3(8),
                  pl.BlockSpec((1, RET_DK), lambda b, h, i: (0, h)),
                  pl.BlockSpec(memory_space=pl.ANY)],
        out_specs=[pl.BlockSpec((c, RET_DK), lambda b, h, i: (b * nc + i, h)),
                   pl.BlockSpec((1, 1, RET_DK, RET_DK), lambda b, h, i: (b, h, 0, 0))],
        out_shape=[jax.ShapeDtypeStruct(mix.shape, mix.dtype),
                   jax.ShapeDtypeStruct((batch, RET_HEADS, RET_DK, RET_DK), F32)],
        input_output_aliases={11: 0},
        compiler_params=_cparams(3),
        name="ret_prompt",
    )(proj, proj, proj, proj, cos, sin, intra, qdec, kdec, cdec, gn.reshape(1, MIX_HALF), mix)


def _columns(x, n):
    t = x.shape[0]
    parts = [jnp.concatenate([x[:, i:i + 128]] * (128 // t), axis=0).T for i in range(0, n, 128)]
    return parts[0] if len(parts) == 1 else jnp.concatenate(parts, axis=0)


def _ret_sample_kernel(q_ref, k_ref, v_ref, g_ref, cos_ref, sin_ref, cdec_ref, gn_ref, s_ref, mix_ref,
                       o_ref, so_ref):
    del mix_ref
    cos = cos_ref[...]
    sin = sin_ref[...]
    q = _rotary(q_ref[...], cos, sin)
    k = _rotary(k_ref[...], cos, sin) * (RET_DK ** -0.5)
    v = v_ref[...]
    qt = _columns(q, RET_DK)
    kt = _columns(k, RET_DK)
    gamma = cdec_ref[0, 0:1, :]
    rows = []
    for t in range(SAMPLE_TOK):
        s_new = s_ref[t, 0] * gamma + kt[:, t:t + 1] * v[t:t + 1, :]
        so_ref[t, 0] = s_new
        rows.append(jnp.sum(qt[:, t:t + 1] * s_new, axis=0, keepdims=True))
    o = jnp.concatenate(rows, axis=0)
    o_ref[...] = _group_norm_gate(o, gn_ref[...], g_ref[...]).astype(o_ref.dtype)


def _ret_sample(proj, mix, state, cos, sin, gn, row0):
    nb = state.shape[0]
    t = SAMPLE_TOK
    r0 = row0 // t
    _, _, _, cdec = _ret_decay_tables(1)
    col = lambda off: pl.BlockSpec((t, RET_DK), lambda b, h: (r0 + b, off + h))
    st = pl.BlockSpec((t, 1, RET_DK, RET_DK), lambda b, h: (b, h, 0, 0))
    return pl.pallas_call(
        _ret_sample_kernel,
        grid=(nb // t, RET_HEADS),
        in_specs=[col(0), col(RET_HEADS), col(2 * RET_HEADS), col(3 * RET_HEADS),
                  pl.BlockSpec((1, RET_HALF), lambda b, h: (0, 0)),
                  pl.BlockSpec((1, RET_HALF), lambda b, h: (0, 0)),
                  pl.BlockSpec((1, 8, RET_DK), lambda b, h: (h, 0, 0)),
                  pl.BlockSpec((1, RET_DK), lambda b, h: (0, h)),
                  st,
                  pl.BlockSpec(memory_space=pl.ANY)],
        out_specs=[pl.BlockSpec((t, RET_DK), lambda b, h: (r0 + b, h)), st],
        out_shape=[jax.ShapeDtypeStruct(mix.shape, mix.dtype),
                   jax.ShapeDtypeStruct(state.shape, F32)],
        input_output_aliases={9: 0},
        compiler_params=_cparams(2),
        name="ret_sample",
    )(proj, proj, proj, proj, cos, sin, cdec, gn.reshape(1, MIX_HALF), state, mix)


def _hg_lower_bound(lb_ref, layer):
    x = lb_ref[...]
    e = jnp.exp(x - jnp.max(x, axis=0, keepdims=True))
    return jnp.sum(e[:layer + 1], axis=0, keepdims=True) / jnp.sum(e, axis=0, keepdims=True)


def _hg_gates(gq, gf, lb):
    f = lb + (1.0 - lb) * jax.nn.sigmoid(gf)
    return _silu(gq), 1.0 - f, f


def _hg_out(o, gn, gg):
    return o * lax.rsqrt(jnp.mean(o * o, axis=-1, keepdims=True) + EPS) * gn * _silu(gg)


def _split3(x):
    hi = x.astype(BF16)
    r = x - hi.astype(F32)
    mid = r.astype(BF16)
    lo = (r - mid.astype(F32)).astype(BF16)
    return hi, mid, lo


def _hg_block(qq, kk, lf, v, st, tri):
    c = HG_CHUNK
    nch = qq.shape[0] // c
    hi, mid, lo = _split3(lf)
    b3 = jnp.dot(tri, jnp.concatenate([hi, mid, lo], axis=1), preferred_element_type=F32)
    b = b3[:, :HG_DK] + b3[:, HG_DK:2 * HG_DK] + b3[:, 2 * HG_DK:]
    split = lambda t: t.reshape(nch, c, t.shape[-1])
    q3, k3, bc = split(qq), split(kk), split(b)
    vb = split(v.astype(BF16))
    row = lax.broadcasted_iota(jnp.int32, (1, c, HG_DK), 1)
    sub_row = lax.broadcasted_iota(jnp.int32, (1, HG_SUB, c), 1)
    sub_col = lax.broadcasted_iota(jnp.int32, (1, HG_SUB, c), 2)
    att_rows = []
    for i0 in range(0, c, HG_SUB):
        q_i = q3[:, i0:i0 + HG_SUB]
        b_i = bc[:, i0:i0 + HG_SUB]
        att_i = jnp.zeros((nch, HG_SUB, c), F32)
        if i0 > 0:
            r = bc[:, i0 - 1:i0]
            q_t = (q_i * jnp.exp(b_i - r)).astype(BF16)
            k_t = jnp.where(row < i0, k3 * jnp.exp(jnp.minimum(r - bc, 0.0)), 0.0).astype(BF16)
            att_i = jnp.einsum('cid,cjd->cij', q_t, k_t, preferred_element_type=F32)
        for j in range(HG_SUB):
            jj = i0 + j
            p = q_i * (k3[:, jj:jj + 1] * jnp.exp(jnp.minimum(b_i - bc[:, jj:jj + 1], 0.0)))
            s_j = jnp.sum(p, axis=2, keepdims=True)
            att_i = jnp.where((sub_col == jj) & (sub_row >= j), s_j, att_i)
        att_rows.append(att_i)
    att = jnp.concatenate(att_rows, axis=1).astype(BF16)
    o_intra = jnp.einsum('cij,cjv->civ', att, vb, preferred_element_type=F32)
    b_last = bc[:, c - 1:c]
    qe = split((qq * jnp.exp(b)).astype(BF16))
    khat = (k3 * jnp.exp(b_last - bc)).astype(BF16)
    dec = jnp.exp(b_last)
    nt = (((1,), (1,)), ((), ()))
    outs = []
    for ci in range(nch):
        outs.append(o_intra[ci] + lax.dot_general(qe[ci], st.astype(BF16), nt, preferred_element_type=F32))
        st = st * dec[ci] + lax.dot_general(vb[ci], khat[ci], (((0,), (0,)), ((), ())),
                                            preferred_element_type=F32)
    return jnp.concatenate(outs, axis=0), st


def _hg_prompt_kernel(layer, gq_ref, gf_ref, gi_ref, gg_ref, lb_ref, gn_ref, tri_ref, mix_ref,
                      o_ref, s_ref, st_ref):
    del mix_ref

    @pl.when(pl.program_id(2) == 0)
    def _():
        st_ref[...] = jnp.zeros_like(st_ref)

    lb = _hg_lower_bound(lb_ref, layer)
    qq, kk, f = _hg_gates(gq_ref[...], gf_ref[...], lb)
    o, st = _hg_block(qq, kk, jnp.log(f), gi_ref[...], st_ref[...], tri_ref[...])
    o_ref[...] = _hg_out(o, gn_ref[...], gg_ref[...]).astype(o_ref.dtype)
    st_ref[...] = st

    @pl.when(pl.program_id(2) == pl.num_programs(2) - 1)
    def _():
        s_ref[0, 0] = st.T


def _hg_prompt(proj, mix, lb_raw, gn, layer, batch, seq):
    nt = seq // HG_ROWS
    c0 = MIX_HALF * 4 // HG_DK
    col = lambda off: pl.BlockSpec((HG_ROWS, HG_DK), lambda b, h, i: (b * nt + i, c0 + off + h))
    tri = jnp.asarray(np.kron(np.eye(HG_ROWS // HG_CHUNK), np.tril(np.ones((HG_CHUNK, HG_CHUNK)))), BF16)
    return pl.pallas_call(
        functools.partial(_hg_prompt_kernel, layer),
        grid=(batch, HG_HEADS, nt),
        in_specs=[col(0), col(HG_HEADS), col(2 * HG_HEADS), col(3 * HG_HEADS),
                  pl.BlockSpec((lb_raw.shape[0], HG_DK), lambda b, h, i: (0, h)),
                  pl.BlockSpec((1, HG_DK), lambda b, h, i: (0, h)),
                  pl.BlockSpec((HG_ROWS, HG_ROWS), lambda b, h, i: (0, 0)),
                  pl.BlockSpec(memory_space=pl.ANY)],
        out_specs=[pl.BlockSpec((HG_ROWS, HG_DK), lambda b, h, i: (b * nt + i, HG_HEADS + h)),
                   pl.BlockSpec((1, 1, HG_DK, HG_DK), lambda b, h, i: (b, h, 0, 0))],
        out_shape=[jax.ShapeDtypeStruct(mix.shape, mix.dtype),
                   jax.ShapeDtypeStruct((batch, HG_HEADS, HG_DK, HG_DK), F32)],
        scratch_shapes=[pltpu.VMEM((HG_DK, HG_DK), F32)],
        input_output_aliases={7: 0},
        compiler_params=_cparams(3),
        name="hgrn_prompt",
    )(proj, proj, proj, proj, lb_raw, gn.reshape(1, MIX_HALF), tri, mix)


def _hg_sample_kernel(layer, gq_ref, gf_ref, gi_ref, gg_ref, lb_ref, gn_ref, s_ref, mix_ref,
                      o_ref, so_ref):
    del mix_ref
    lb = _hg_lower_bound(lb_ref, layer)
    qq, kk, f = _hg_gates(gq_ref[...], gf_ref[...], lb)
    v = gi_ref[...]
    qt = _columns(qq, HG_DK)
    kt = _columns(kk, HG_DK)
    ft = _columns(f, HG_DK)
    rows = []
    for t in range(SAMPLE_TOK):
        s_new = s_ref[t, 0] * ft[:, t:t + 1] + kt[:, t:t + 1] * v[t:t + 1, :]
        so_ref[t, 0] = s_new
        rows.append(jnp.sum(qt[:, t:t + 1] * s_new, axis=0, keepdims=True))
    o = jnp.concatenate(rows, axis=0)
    o_ref[...] = _hg_out(o, gn_ref[...], gg_ref[...]).astype(o_ref.dtype)


def _hg_sample(proj, mix, state, lb_raw, gn, layer, row0):
    nb = state.shape[0]
    t = SAMPLE_TOK
    r0 = row0 // t
    c0 = MIX_HALF * 4 // HG_DK
    col = lambda off: pl.BlockSpec((t, HG_DK), lambda b, h: (r0 + b, c0 + off + h))
    st = pl.BlockSpec((t, 1, HG_DK, HG_DK), lambda b, h: (b, h, 0, 0))
    return pl.pallas_call(
        functools.partial(_hg_sample_kernel, layer),
        grid=(nb // t, HG_HEADS),
        in_specs=[col(0), col(HG_HEADS), col(2 * HG_HEADS), col(3 * HG_HEADS),
                  pl.BlockSpec((lb_raw.shape[0], HG_DK), lambda b, h: (0, h)),
                  pl.BlockSpec((1, HG_DK), lambda b, h: (0, h)),
                  st,
                  pl.BlockSpec(memory_space=pl.ANY)],
        out_specs=[pl.BlockSpec((t, HG_DK), lambda b, h: (r0 + b, HG_HEADS + h)), st],
        out_shape=[jax.ShapeDtypeStruct(mix.shape, mix.dtype),
                   jax.ShapeDtypeStruct(state.shape, F32)],
        input_output_aliases={7: 0},
        compiler_params=_cparams(2),
        name="hgrn_sample",
    )(proj, proj, proj, proj, lb_raw, gn.reshape(1, MIX_HALF), state, mix)


S5_SEG = S5_SUB // 8
S5_POW_ROWS = 16
assert S5_SEG & (S5_SEG - 1) == 0


def _cmul(ar, ai, br, bi):
    return ar * br - ai * bi, ar * bi + ai * br


def _s5_prep_kernel(lr_ref, li_ref, ldt_ref, brt_ref, bit_ref, pr_ref, pi_ref, bbr_ref, bbi_ref):
    lr = lr_ref[...]
    li = li_ref[...]
    dt = jnp.exp(ldt_ref[...])
    mag = jnp.exp(lr * dt)
    ar = mag * jnp.cos(li * dt)
    ai = mag * jnp.sin(li * dt)
    den = lr * lr + li * li
    cr = ((ar - 1.0) * lr + ai * li) / den
    ci = (ai * lr - (ar - 1.0) * li) / den
    brt = brt_ref[...]
    bit = bit_ref[...]
    bbr_ref[...] = cr * brt - ci * bit
    bbi_ref[...] = cr * bit + ci * brt
    n = pr_ref.shape[1]
    pr_ref[0:8, :] = jnp.broadcast_to(ar, (8, n))
    pi_ref[0:8, :] = jnp.broadcast_to(ai, (8, n))
    seg_pow = (ar, ai)
    for _ in range(S5_SEG.bit_length() - 1):
        seg_pow = _cmul(*seg_pow, *seg_pow)
    p = seg_pow
    for m in range(8):
        pr_ref[8 + m:9 + m, :] = p[0]
        pi_ref[8 + m:9 + m, :] = p[1]
        p = _cmul(*p, *seg_pow)


def _s5_prep(lam_re, lam_im, log_dt, b_re, b_im):
    n = S5_GROUPS * S5_STATE
    flat = lambda t: t.reshape(1, n)
    b_t = lambda t: jnp.transpose(t, (2, 0, 1)).reshape(S5_GROUP, n)
    ldt = jnp.repeat(log_dt, S5_STATE).reshape(1, n)
    return pl.pallas_call(
        _s5_prep_kernel,
        out_shape=[jax.ShapeDtypeStruct((S5_POW_ROWS, n), F32)] * 2
        + [jax.ShapeDtypeStruct((S5_GROUP, n), F32)] * 2,
        compiler_params=pltpu.CompilerParams(vmem_limit_bytes=VMEM_LIMIT_BYTES),
        name="s5_prep",
    )(flat(lam_re), flat(lam_im), ldt, b_t(b_re), b_t(b_im))


def _s5_block_diag(bbr, bbi, c_re, c_im):
    eye = jnp.eye(S5_BLK_GROUPS, dtype=F32)
    def b_blk(t):
        t = t.reshape(S5_GROUP, S5_NBLK, S5_BLK_GROUPS, S5_STATE)
        t = jnp.einsum('cbgp,hg->bhcgp', t, eye)
        return t.reshape(S5_NBLK, S5_BLK_CH, S5_BLK_ST).astype(BF16)
    def c_blk(t):
        t = t.reshape(S5_NBLK, S5_BLK_GROUPS, S5_GROUP, S5_STATE)
        t = jnp.einsum('bgcp,hg->bhpgc', t, eye)
        return t.reshape(S5_NBLK, S5_BLK_ST, S5_BLK_CH).astype(BF16)
    return (jnp.concatenate([b_blk(bbr), b_blk(bbi)], axis=2),
            jnp.concatenate([c_blk(c_re), c_blk(-c_im)], axis=1))


def _s5_readout(u, h_bf16, c_ref, d_ref):
    y = jnp.dot(h_bf16, c_ref[0], preferred_element_type=F32) + d_ref[...] * u
    return jax.nn.gelu(y)


def _s5_row_perm(rows):
    seg = rows // 8
    p = np.zeros((rows, rows), np.float32)
    t, s = np.meshgrid(np.arange(seg), np.arange(8), indexing="ij")
    p[(t * 8 + s).ravel(), (s * seg + t).ravel()] = 1.0
    return jnp.asarray(p, BF16), jnp.asarray(p.T, BF16)


def _s5_prompt_kernel(u_ref, p_ref, pt_ref, b_ref, c_ref, pr_ref, pi_ref, d_ref, z_in_ref,
                      z_ref, hr_out, hi_out, *scratch):
    del z_in_ref
    x_refs = scratch[:S5_NSUB]
    hb_refs = scratch[S5_NSUB:2 * S5_NSUB]
    us_ref, cr_carry, ci_carry = scratch[2 * S5_NSUB:]
    tb = pl.program_id(2)
    n = S5_BLK_ST
    ch = S5_BLK_CH

    @pl.when(tb == 0)
    def _():
        cr_carry[...] = jnp.zeros_like(cr_carry)
        ci_carry[...] = jnp.zeros_like(ci_carry)

    for q in range(S5_NSUB):
        rows = slice(q * S5_SUB, (q + 1) * S5_SUB)
        up = jnp.dot(p_ref[...], jnp.concatenate(_split3(u_ref[rows, :]), axis=1),
                     preferred_element_type=F32)
        us_ref[rows, :] = (up[:, :ch] + up[:, ch:2 * ch]) + up[:, 2 * ch:]
        x_refs[q][...] = jnp.dot(up[:, :ch].astype(BF16), b_ref[0], preferred_element_type=F32)

    a_r = pr_ref[0:8, :]
    a_i = pi_ref[0:8, :]
    seg0 = 8
    row = lax.broadcasted_iota(jnp.int32, (8, n), 0)
    h0r = cr_carry[...]
    h0i = ci_carry[...]
    for q in range(S5_NSUB):
        x_ref = x_refs[q]
        def advance(t, h, x_ref=x_ref):
            rows = pl.ds(pl.multiple_of(t * 8, 8), 8)
            dr, di = _cmul(a_r, a_i, *h)
            return dr + x_ref[rows, 0:n], di + x_ref[rows, n:2 * n]

        zero = jnp.zeros((8, n), F32)
        er, ei = lax.fori_loop(0, S5_SEG, advance, (zero, zero), unroll=4)
        for s in range(3):
            m = 1 << s
            g_r = pr_ref[seg0 + m - 1:seg0 + m, :]
            g_i = pi_ref[seg0 + m - 1:seg0 + m, :]
            sr = jnp.where(row >= m, pltpu.roll(er, m, 0), 0.0)
            si = jnp.where(row >= m, pltpu.roll(ei, m, 0), 0.0)
            dr, di = _cmul(g_r, g_i, sr, si)
            er, ei = er + dr, ei + di
        dr, di = _cmul(pr_ref[seg0:seg0 + 8, :], pi_ref[seg0:seg0 + 8, :], h0r, h0i)
        er, ei = er + dr, ei + di
        cm_r = jnp.where(row >= 1, pltpu.roll(er, 1, 0), h0r)
        cm_i = jnp.where(row >= 1, pltpu.roll(ei, 1, 0), h0i)
        h0r = er[7:8, :]
        h0i = ei[7:8, :]
        def emit(t2, h, hb_ref=hb_refs[q]):
            h1 = advance(2 * t2, h)
            h2 = advance(2 * t2 + 1, h1)
            rows = pl.ds(pl.multiple_of(t2 * 16, 16), 16)
            hb_ref[rows, 0:n] = jnp.concatenate([h1[0], h2[0]], axis=0).astype(BF16)
            hb_ref[rows, n:2 * n] = jnp.concatenate([h1[1], h2[1]], axis=0).astype(BF16)
            return h2

        lax.fori_loop(0, S5_SEG // 2, emit, (cm_r, cm_i), unroll=2)
        rows = slice(q * S5_SUB, (q + 1) * S5_SUB)
        zp = _s5_readout(us_ref[rows, :], hb_refs[q][...], c_ref, d_ref).astype(BF16)
        z_ref[rows, :] = jnp.dot(pt_ref[...], zp, preferred_element_type=F32).astype(z_ref.dtype)
    cr_carry[...] = h0r
    ci_carry[...] = h0i

    @pl.when(tb == pl.num_programs(2) - 1)
    def _():
        hr_out[0, 0] = h0r
        hi_out[0, 0] = h0i


def _s5_prompt(u, z, mats, pows, d, batch, seq):
    b_m, c_m = mats
    p_r, p_i = pows
    perm, perm_t = _s5_row_perm(S5_SUB)
    nt = seq // S5_ROWS
    blk3 = lambda r, c: pl.BlockSpec((1, r, c), lambda b, j, i: (j, 0, 0))
    whole = pl.BlockSpec((S5_SUB, S5_SUB), lambda b, j, i: (0, 0))
    st_spec = pl.BlockSpec((1, 1, 1, S5_BLK_ST), lambda b, j, i: (b, j, 0, 0))
    st_shape = jax.ShapeDtypeStruct((batch, S5_NBLK, 1, S5_BLK_ST), F32)
    return pl.pallas_call(
        _s5_prompt_kernel,
        grid=(batch, S5_NBLK, nt),
        in_specs=[pl.BlockSpec((S5_ROWS, S5_BLK_CH), lambda b, j, i: (b * nt + i, j)),
                  whole, whole,
                  blk3(S5_BLK_CH, 2 * S5_BLK_ST), blk3(2 * S5_BLK_ST, S5_BLK_CH),
                  pl.BlockSpec((S5_POW_ROWS, S5_BLK_ST), lambda b, j, i: (0, j)),
                  pl.BlockSpec((S5_POW_ROWS, S5_BLK_ST), lambda b, j, i: (0, j)),
                  pl.BlockSpec((1, S5_BLK_CH), lambda b, j, i: (0, j)),
                  pl.BlockSpec(memory_space=pl.ANY)],
        out_specs=[pl.BlockSpec((S5_ROWS, S5_BLK_CH), lambda b, j, i: (b * nt + i, j)), st_spec, st_spec],
        out_shape=[jax.ShapeDtypeStruct(z.shape, z.dtype), st_shape, st_shape],
        input_output_aliases={8: 0},
        scratch_shapes=[pltpu.VMEM((S5_SUB, 2 * S5_BLK_ST), F32) for _ in range(S5_NSUB)]
        + [pltpu.VMEM((S5_SUB, 2 * S5_BLK_ST), BF16) for _ in range(S5_NSUB)]
        + [pltpu.VMEM((S5_ROWS, S5_BLK_CH), F32),
           pltpu.VMEM((1, S5_BLK_ST), F32), pltpu.VMEM((1, S5_BLK_ST), F32)],
        compiler_params=_cparams(3),
        name="s5_prompt",
    )(u, perm, perm_t, b_m, c_m, p_r, p_i, d.reshape(1, D_MODEL), z)


def _s5_sample_kernel(u_ref, b_ref, c_ref, pr_ref, pi_ref, d_ref, h0r_ref, h0i_ref,
                      z_in_ref, z_ref, hr_out, hi_out):
    del z_in_ref
    n = S5_BLK_ST
    u = u_ref[...]
    x = jnp.dot(u.astype(BF16), b_ref[0], preferred_element_type=F32)
    dr, di = _cmul(pr_ref[0:1, :], pi_ref[0:1, :], h0r_ref[...], h0i_ref[...])
    hr = x[:, 0:n] + dr
    hi = x[:, n:2 * n] + di
    hr_out[...] = hr
    hi_out[...] = hi
    h = jnp.concatenate([hr, hi], axis=1).astype(BF16)
    z_ref[...] = _s5_readout(u, h, c_ref, d_ref).astype(z_ref.dtype)


def _s5_sample(u, z, mats, pows, d, h0r, h0i, row0):
    b_m, c_m = mats
    p_r, p_i = pows
    nb = h0r.shape[0]
    r0 = row0 // nb
    blk3 = lambda r, c: pl.BlockSpec((1, r, c), lambda j: (j, 0, 0))
    st_spec = pl.BlockSpec((nb, S5_BLK_ST), lambda j: (0, j))
    st_shape = jax.ShapeDtypeStruct(h0r.shape, F32)
    return pl.pallas_call(
        _s5_sample_kernel,
        grid=(S5_NBLK,),
        in_specs=[pl.BlockSpec((nb, S5_BLK_CH), lambda j: (r0, j)),
                  blk3(S5_BLK_CH, 2 * S5_BLK_ST), blk3(2 * S5_BLK_ST, S5_BLK_CH),
                  pl.BlockSpec((S5_POW_ROWS, S5_BLK_ST), lambda j: (0, j)),
                  pl.BlockSpec((S5_POW_ROWS, S5_BLK_ST), lambda j: (0, j)),
                  pl.BlockSpec((1, S5_BLK_CH), lambda j: (0, j)),
                  st_spec, st_spec,
                  pl.BlockSpec(memory_space=pl.ANY)],
        out_specs=[pl.BlockSpec((nb, S5_BLK_CH), lambda j: (r0, j)), st_spec, st_spec],
        out_shape=[jax.ShapeDtypeStruct(z.shape, z.dtype), st_shape, st_shape],
        input_output_aliases={8: 0},
        compiler_params=_cparams(1),
        name="s5_sample",
    )(u, b_m, c_m, p_r, p_i, d.reshape(1, D_MODEL), h0r, h0i, z)


def _rope_tables(pos):
    inv = ROPE_BASE ** (-jnp.arange(RET_HALF, dtype=F32) / RET_HALF)
    ang = pos[:, None] * inv[None, :]
    return jnp.cos(ang), jnp.sin(ang)


def _ffn(x, norm_g, wg, wu, wd):
    h = _rmsnorm(x, norm_g, [BF16])[0]
    t = _matmul(h, [wg, wu], _epi_swiglu, BF16, name="ffn_gate_up")
    return _matmul(t, [wd], _epi_id, F32, res=x, name="ffn_down")


def kernel(x_prompt, x_sample, state_ret, state_hgrn, state_s5_re, state_s5_im, attn_norm_g, w_in, ret_gn_g, hg_lb, hg_gn_g, w_out, ssm_norm_g, s5_lam_re, s5_lam_im, s5_log_dt, s5_b_re, s5_b_im, s5_c_re, s5_c_im, s5_d, w_glu_a, w_glu_b, ffn_norm_g, w_ffn_gate, w_ffn_up, w_ffn_down, final_norm_g):
    one = lambda t: t.reshape(t.shape[1:])
    (state_ret, state_hgrn, state_s5_re, state_s5_im, attn_norm_g, w_in, ret_gn_g, hg_gn_g, w_out, ssm_norm_g,
     s5_lam_re, s5_lam_im, s5_log_dt, s5_b_re, s5_b_im, s5_c_re, s5_c_im, s5_d, w_glu_a, w_glu_b) = map(one, (
         state_ret, state_hgrn, state_s5_re, state_s5_im, attn_norm_g, w_in, ret_gn_g, hg_gn_g, w_out, ssm_norm_g,
         s5_lam_re, s5_lam_im, s5_log_dt, s5_b_re, s5_b_im, s5_c_re, s5_c_im, s5_d, w_glu_a, w_glu_b))
    cos_p, sin_p = _rope_tables(jnp.arange(SEQ, dtype=F32))
    cos_s, sin_s = _rope_tables(jnp.full((1,), float(PAST_LEN), F32))

    h, x = _rmsnorm_stack(x_prompt.reshape(N_PROMPT, D_MODEL), x_sample.reshape(DEC_BATCH, D_MODEL), attn_norm_g)
    proj = _matmul(h, [w_in], _epi_id, F32, tn=1024, name="w_in")
    mix = jnp.zeros((N_ROWS, D_MODEL), BF16)
    mix, ret_p = _ret_prompt(proj, mix, cos_p, sin_p, ret_gn_g, BATCH, SEQ)
    mix, hg_p = _hg_prompt(proj, mix, hg_lb, hg_gn_g, 0, BATCH, SEQ)
    mix, ret_s = _ret_sample(proj, mix, state_ret, cos_s, sin_s, ret_gn_g, N_PROMPT)
    mix, hg_s = _hg_sample(proj, mix, state_hgrn, hg_lb, hg_gn_g, 0, N_PROMPT)
    x = _matmul(mix, [w_out], _epi_id, F32, res=x, name="w_out")
    x = _ffn(x, ffn_norm_g[0], w_ffn_gate[0], w_ffn_up[0], w_ffn_down[0])

    u = _rmsnorm(x, ssm_norm_g, [F32])[0]
    p_r, p_i, bbr, bbi = _s5_prep(s5_lam_re, s5_lam_im, s5_log_dt, s5_b_re, s5_b_im)
    mats = _s5_block_diag(bbr, bbi, s5_c_re, s5_c_im)
    z = jnp.zeros((N_ROWS, D_MODEL), BF16)
    z, s5r_p, s5i_p = _s5_prompt(u, z, mats, (p_r, p_i), s5_d, BATCH, SEQ)
    n_st = S5_GROUPS * S5_STATE
    z, s5r_s, s5i_s = _s5_sample(u, z, mats, (p_r, p_i), s5_d, state_s5_re.reshape(DEC_BATCH, n_st),
                                 state_s5_im.reshape(DEC_BATCH, n_st), N_PROMPT)
    x = _matmul(z, [w_glu_a, w_glu_b], _epi_glu, F32, res=x, name="glu")
    x = _ffn(x, ffn_norm_g[1], w_ffn_gate[1], w_ffn_up[1], w_ffn_down[1])

    y_p, y_s = _rmsnorm_split(x, final_norm_g, N_PROMPT)
    st = lambda t, b: t.reshape(1, b, S5_GROUPS, S5_STATE)
    return (y_p.reshape(BATCH, SEQ, D_MODEL), y_s.reshape(DEC_BATCH, 1, D_MODEL),
            ret_p[None], ret_s[None], hg_p[None], hg_s[None],
            st(s5r_p, BATCH), st(s5i_p, BATCH), st(s5r_s, DEC_BATCH), st(s5i_s, DEC_BATCH))
```

```python
import functools
import math

import numpy as np
import jax
import jax.numpy as jnp
from jax import lax
from jax.experimental import pallas as pl
from jax.experimental.pallas import tpu as pltpu

F32 = jnp.float32
BF16 = jnp.bfloat16

D_MODEL = 2048
BATCH = 4
SEQ = 2048
DEC_BATCH = 128
PAST_LEN = 16384
N_PROMPT = BATCH * SEQ
N_ROWS = N_PROMPT + DEC_BATCH
MIX_HALF = D_MODEL // 2
RET_HEADS = 4
RET_DK = MIX_HALF // RET_HEADS
RET_HALF = RET_DK // 2
RET_CHUNK = 128
RET_ROWS = 256
HG_HEADS = 8
HG_DK = MIX_HALF // HG_HEADS
HG_CHUNK = 64
HG_SUB = 16
HG_ROWS = 256
S5_GROUP = 16
S5_GROUPS = D_MODEL // S5_GROUP
S5_STATE = 64
S5_BLK_GROUPS = 16
S5_BLK_CH = S5_BLK_GROUPS * S5_GROUP
S5_BLK_ST = S5_BLK_GROUPS * S5_STATE
S5_NBLK = S5_GROUPS // S5_BLK_GROUPS
S5_SUB = 256
S5_NSUB = 2
S5_ROWS = S5_SUB * S5_NSUB
D_FF = 5632
ROPE_BASE = 10000.0
EPS = 1e-6
SAMPLE_TOK = 16

VMEM_LIMIT_BYTES = 56 * 1024 * 1024


def _cparams(n_axes):
    return pltpu.CompilerParams(dimension_semantics=("arbitrary",) * n_axes,
                                vmem_limit_bytes=VMEM_LIMIT_BYTES)


def _silu(x):
    return x * jax.nn.sigmoid(x)


def _rmsnorm_kernel(x_ref, g_ref, *o_refs):
    x = x_ref[...]
    y = x * lax.rsqrt(jnp.mean(x * x, axis=-1, keepdims=True) + EPS) * g_ref[...]
    for o_ref in o_refs:
        o_ref[...] = y.astype(o_ref.dtype)


def _rmsnorm(x, g, dtypes, tm=320):
    m, d = x.shape
    return pl.pallas_call(
        _rmsnorm_kernel,
        grid=(m // tm,),
        in_specs=[pl.BlockSpec((tm, d), lambda i: (i, 0)),
                  pl.BlockSpec((1, d), lambda i: (0, 0))],
        out_specs=[pl.BlockSpec((tm, d), lambda i: (i, 0)) for _ in dtypes],
        out_shape=[jax.ShapeDtypeStruct((m, d), dt) for dt in dtypes],
        compiler_params=_cparams(1),
        name="rmsnorm",
    )(x, g.reshape(1, d))


ROW_TILE = 512


def _rms(x, g):
    return x * lax.rsqrt(jnp.mean(x * x, axis=-1, keepdims=True) + EPS) * g


def _rmsnorm_stack_kernel(n_prompt_tiles, n_sample, xp_ref, xs_ref, g_ref, h_ref, x_ref):
    i = pl.program_id(0)

    @pl.when(i < n_prompt_tiles)
    def _():
        x = xp_ref[...]
        x_ref[...] = x
        h_ref[...] = _rms(x, g_ref[...]).astype(h_ref.dtype)

    @pl.when(i >= n_prompt_tiles)
    def _():
        x = xs_ref[...]
        x_ref[0:n_sample, :] = x
        h_ref[0:n_sample, :] = _rms(x, g_ref[...]).astype(h_ref.dtype)


def _rmsnorm_stack(x_prompt, x_sample, g):
    (mp, d), ms = x_prompt.shape, x_sample.shape[0]
    tm = ROW_TILE
    npt = mp // tm
    m = mp + ms
    return pl.pallas_call(
        functools.partial(_rmsnorm_stack_kernel, npt, ms),
        grid=(npt + 1,),
        in_specs=[pl.BlockSpec((tm, d), lambda i: (jnp.minimum(i, npt - 1), 0)),
                  pl.BlockSpec((ms, d), lambda i: (0, 0)),
                  pl.BlockSpec((1, d), lambda i: (0, 0))],
        out_specs=[pl.BlockSpec((tm, d), lambda i: (i, 0))] * 2,
        out_shape=[jax.ShapeDtypeStruct((m, d), BF16), jax.ShapeDtypeStruct((m, d), F32)],
        compiler_params=_cparams(1),
        name="rmsnorm_stack",
    )(x_prompt, x_sample, g.reshape(1, d))


def _rmsnorm_split_kernel(n_prompt_tiles, n_sample, x_ref, g_ref, yp_ref, ys_ref):
    i = pl.program_id(0)

    @pl.when(i < n_prompt_tiles)
    def _():
        yp_ref[...] = _rms(x_ref[...], g_ref[...])

    @pl.when(i >= n_prompt_tiles)
    def _():
        ys_ref[...] = _rms(x_ref[0:n_sample, :], g_ref[...])


def _rmsnorm_split(x, g, n_prompt):
    m, d = x.shape
    tm = ROW_TILE
    npt = n_prompt // tm
    ms = m - n_prompt
    return pl.pallas_call(
        functools.partial(_rmsnorm_split_kernel, npt, ms),
        grid=(npt + 1,),
        in_specs=[pl.BlockSpec((tm, d), lambda i: (i, 0)),
                  pl.BlockSpec((1, d), lambda i: (0, 0))],
        out_specs=[pl.BlockSpec((tm, d), lambda i: (jnp.minimum(i, npt - 1), 0)),
                   pl.BlockSpec((ms, d), lambda i: (0, 0))],
        out_shape=[jax.ShapeDtypeStruct((n_prompt, d), F32), jax.ShapeDtypeStruct((ms, d), F32)],
        compiler_params=_cparams(1),
        name="rmsnorm_split",
    )(x, g.reshape(1, d))


def _mm_kernel(n_w, epilogue, has_res, a_ref, *refs):
    w_refs = refs[:n_w]
    res_ref = refs[n_w] if has_res else None
    o_ref = refs[n_w + has_res]
    wb_refs = refs[n_w + has_res + 1:]

    @pl.when(pl.program_id(1) == 0)
    def _():
        for w_ref, wb_ref in zip(w_refs, wb_refs):
            wb_ref[...] = w_ref[...].astype(BF16)

    a = a_ref[...]
    y = epilogue(*[jnp.dot(a, wb_ref[...], preferred_element_type=F32) for wb_ref in wb_refs])
    if has_res:
        y = res_ref[...] + y
    o_ref[...] = y.astype(o_ref.dtype)


def _matmul(a, ws, epilogue, out_dtype, res=None, layer=None, tm=640, tn=512, name="matmul"):
    m, k = a.shape
    n = ws[0].shape[-1]
    in_specs = [pl.BlockSpec((tm, k), lambda j, i: (i, 0))]
    if layer is None:
        in_specs += [pl.BlockSpec((k, tn), lambda j, i: (0, j)) for _ in ws]
    else:
        in_specs += [pl.BlockSpec((None, k, tn), lambda j, i: (layer, 0, j)) for _ in ws]
    args = [a, *ws]
    if res is not None:
        in_specs.append(pl.BlockSpec((tm, tn), lambda j, i: (i, j)))
        args.append(res)
    return pl.pallas_call(
        functools.partial(_mm_kernel, len(ws), epilogue, res is not None),
        grid=(n // tn, m // tm),
        in_specs=in_specs,
        out_specs=pl.BlockSpec((tm, tn), lambda j, i: (i, j)),
        out_shape=jax.ShapeDtypeStruct((m, n), out_dtype),
        scratch_shapes=[pltpu.VMEM((k, tn), BF16) for _ in ws],
        compiler_params=_cparams(2),
        name=name,
    )(*args)


def _epi_id(y):
    return y


def _epi_swiglu(g, u):
    return _silu(g) * u


def _epi_glu(a, b):
    return a * jax.nn.sigmoid(b)


def _rotary(x, cos, sin):
    x1 = x[:, :RET_HALF]
    x2 = x[:, RET_HALF:]
    return jnp.concatenate([x1 * cos - x2 * sin, x1 * sin + x2 * cos], axis=1)


def _group_norm_gate(o, gn, g):
    mu = jnp.mean(o, axis=-1, keepdims=True)
    d = o - mu
    var = jnp.mean(d * d, axis=-1, keepdims=True)
    return d * lax.rsqrt(var + EPS) * gn * _silu(g)


def _ret_prompt_kernel(q_ref, k_ref, v_ref, g_ref, cos_ref, sin_ref, intra_ref, qdec_ref, kdec_ref,
                       cdec_ref, gn_ref, mix_ref, o_ref, s_ref):
    del mix_ref

    @pl.when(pl.program_id(2) == 0)
    def _():
        s_ref[...] = jnp.zeros_like(s_ref)

    s = s_ref[0, 0]
    for c0 in range(0, RET_ROWS, RET_CHUNK):
        rows = pl.ds(c0, RET_CHUNK)
        cos = cos_ref[rows, :]
        sin = sin_ref[rows, :]
        q = _rotary(q_ref[rows, :], cos, sin)
        k = _rotary(k_ref[rows, :], cos, sin) * (RET_DK ** -0.5)
        v = v_ref[rows, :].astype(BF16)
        att = lax.dot_general(q.astype(BF16), k.astype(BF16), (((1,), (1,)), ((), ())),
                              preferred_element_type=F32) * intra_ref[0]
        o = (jnp.dot(att.astype(BF16), v, preferred_element_type=F32)
             + jnp.dot((q * qdec_ref[0]).astype(BF16), s.astype(BF16), preferred_element_type=F32))
        s = s * cdec_ref[0, 0:1, :] + lax.dot_general(
            (k * kdec_ref[0]).astype(BF16), v, (((0,), (0,)), ((), ())), preferred_element_type=F32)
        o_ref[rows, :] = _group_norm_gate(o, gn_ref[...], g_ref[rows, :]).astype(o_ref.dtype)
    s_ref[0, 0] = s


def _ret_decay_tables(c):
    lg = np.log(1.0 - 2.0 ** (-5.0 - np.arange(RET_HEADS, dtype=np.float64)))
    idx = np.arange(c, dtype=np.float64)
    diff = idx[:, None] - idx[None, :]
    intra = np.where(diff >= 0, np.exp(np.maximum(diff, 0.0)[None] * lg[:, None, None]), 0.0)
    ones = np.ones((1, 1, RET_DK))
    qdec = np.exp((idx[None, :, None] + 1.0) * lg[:, None, None]) * ones
    kdec = np.exp((c - 1.0 - idx[None, :, None]) * lg[:, None, None]) * ones
    cdec = np.exp(c * lg)[:, None, None] * np.ones((1, 8, RET_DK))
    return [jnp.asarray(t, F32) for t in (intra, qdec, kdec, cdec)]


def _ret_prompt(proj, mix, cos, sin, gn, batch, seq):
    c = RET_CHUNK
    r = RET_ROWS
    nc = seq // r
    intra, qdec, kdec, cdec = _ret_decay_tables(c)
    col = lambda off: pl.BlockSpec((r, RET_DK), lambda b, h, i: (b * nc + i, off + h))
    head3 = lambda r: pl.BlockSpec((1, r, RET_DK), lambda b, h, i: (h, 0, 0))
    return pl.pallas_call(
        _ret_prompt_kernel,
        grid=(batch, RET_HEADS, nc),
        in_specs=[col(0), col(RET_HEADS), col(2 * RET_HEADS), col(3 * RET_HEADS),
                  pl.BlockSpec((r, RET_HALF), lambda b, h, i: (i, 0)),
                  pl.BlockSpec((r, RET_HALF), lambda b, h, i: (i, 0)),
                  pl.BlockSpec((1, c, c), lambda b, h, i: (h, 0, 0)),
                  head3(c), head3(c), head3(8),
                  pl.BlockSpec((1, RET_DK), lambda b, h, i: (0, h)),
                  pl.BlockSpec(memory_space=pl.ANY)],
        out_specs=[pl.BlockSpec((r, RET_DK), lambda b, h, i: (b * nc + i, h)),
                   pl.BlockSpec((1, 1, RET_DK, RET_DK), lambda b, h, i: (b, h, 0, 0))],
        out_shape=[jax.ShapeDtypeStruct(mix.shape, mix.dtype),
                   jax.ShapeDtypeStruct((batch, RET_HEADS, RET_DK, RET_DK), F32)],
        input_output_aliases={11: 0},
        compiler_params=_cparams(3),
        name="ret_prompt",
    )(proj, proj, proj, proj, cos, sin, intra, qdec, kdec, cdec, gn.reshape(1, MIX_HALF), mix)


def _columns(x, n):
    t = x.shape[0]
    parts = [jnp.concatenate([x[:, i:i + 128]] * (128 // t), axis=0).T for i in range(0, n, 128)]
    return parts[0] if len(parts) == 1 else jnp.concatenate(parts, axis=0)


def _ret_sample_kernel(q_ref, k_ref, v_ref, g_ref, cos_ref, sin_ref, cdec_ref, gn_ref, s_ref, mix_ref,
                       o_ref, so_ref):
    del mix_ref
    cos = cos_ref[...]
    sin = sin_ref[...]
    q = _rotary(q_ref[...], cos, sin)
    k = _rotary(k_ref[...], cos, sin) * (RET_DK ** -0.5)
    v = v_ref[...]
    qt = _columns(q, RET_DK)
    kt = _columns(k, RET_DK)
    gamma = cdec_ref[0, 0:1, :]
    rows = []
    for t in range(SAMPLE_TOK):
        s_new = s_ref[t, 0] * gamma + kt[:, t:t + 1] * v[t:t + 1, :]
        so_ref[t, 0] = s_new
        rows.append(jnp.sum(qt[:, t:t + 1] * s_new, axis=0, keepdims=True))
    o = jnp.concatenate(rows, axis=0)
    o_ref[...] = _group_norm_gate(o, gn_ref[...], g_ref[...]).astype(o_ref.dtype)


def _ret_sample(proj, mix, state, cos, sin, gn, row0):
    nb = state.shape[0]
    t = SAMPLE_TOK
    r0 = row0 // t
    _, _, _, cdec = _ret_decay_tables(1)
    col = lambda off: pl.BlockSpec((t, RET_DK), lambda b, h: (r0 + b, off + h))
    st = pl.BlockSpec((t, 1, RET_DK, RET_DK), lambda b, h: (b, h, 0, 0))
    return pl.pallas_call(
        _ret_sample_kernel,
        grid=(nb // t, RET_HEADS),
        in_specs=[col(0), col(RET_HEADS), col(2 * RET_HEADS), col(3 * RET_HEADS),
                  pl.BlockSpec((1, RET_HALF), lambda b, h: (0, 0)),
                  pl.BlockSpec((1, RET_HALF), lambda b, h: (0, 0)),
                  pl.BlockSpec((1, 8, RET_DK), lambda b, h: (h, 0, 0)),
                  pl.BlockSpec((1, RET_DK), lambda b, h: (0, h)),
                  st,
                  pl.BlockSpec(memory_space=pl.ANY)],
        out_specs=[pl.BlockSpec((t, RET_DK), lambda b, h: (r0 + b, h)), st],
        out_shape=[jax.ShapeDtypeStruct(mix.shape, mix.dtype),
                   jax.ShapeDtypeStruct(state.shape, F32)],
        input_output_aliases={9: 0},
        compiler_params=_cparams(2),
        name="ret_sample",
    )(proj, proj, proj, proj, cos, sin, cdec, gn.reshape(1, MIX_HALF), state, mix)


def _hg_lower_bound(lb_ref, layer):
    x = lb_ref[...]
    e = jnp.exp(x - jnp.max(x, axis=0, keepdims=True))
    return jnp.sum(e[:layer + 1], axis=0, keepdims=True) / jnp.sum(e, axis=0, keepdims=True)


def _hg_gates(gq, gf, lb):
    f = lb + (1.0 - lb) * jax.nn.sigmoid(gf)
    return _silu(gq), 1.0 - f, f


def _hg_out(o, gn, gg):
    return o * lax.rsqrt(jnp.mean(o * o, axis=-1, keepdims=True) + EPS) * gn * _silu(gg)


def _split3(x):
    hi = x.astype(BF16)
    r = x - hi.astype(F32)
    mid = r.astype(BF16)
    lo = (r - mid.astype(F32)).astype(BF16)
    return hi, mid, lo


def _hg_block(qq, kk, lf, v, st, tri):
    c = HG_CHUNK
    nch = qq.shape[0] // c
    hi, mid, lo = _split3(lf)
    b3 = jnp.dot(tri, jnp.concatenate([hi, mid, lo], axis=1), preferred_element_type=F32)
    b = b3[:, :HG_DK] + b3[:, HG_DK:2 * HG_DK] + b3[:, 2 * HG_DK:]
    split = lambda t: t.reshape(nch, c, t.shape[-1])
    q3, k3, bc = split(qq), split(kk), split(b)
    vb = split(v.astype(BF16))
    row = lax.broadcasted_iota(jnp.int32, (1, c, HG_DK), 1)
    sub_row = lax.broadcasted_iota(jnp.int32, (1, HG_SUB, c), 1)
    sub_col = lax.broadcasted_iota(jnp.int32, (1, HG_SUB, c), 2)
    att_rows = []
    for i0 in range(0, c, HG_SUB):
        q_i = q3[:, i0:i0 + HG_SUB]
        b_i = bc[:, i0:i0 + HG_SUB]
        att_i = jnp.zeros((nch, HG_SUB, c), F32)
        if i0 > 0:
            r = bc[:, i0 - 1:i0]
            q_t = (q_i * jnp.exp(b_i - r)).astype(BF16)
            k_t = jnp.where(row < i0, k3 * jnp.exp(jnp.minimum(r - bc, 0.0)), 0.0).astype(BF16)
            att_i = jnp.einsum('cid,cjd->cij', q_t, k_t, preferred_element_type=F32)
        for j in range(HG_SUB):
            jj = i0 + j
            p = q_i * (k3[:, jj:jj + 1] * jnp.exp(jnp.minimum(b_i - bc[:, jj:jj + 1], 0.0)))
            s_j = jnp.sum(p, axis=2, keepdims=True)
            att_i = jnp.where((sub_col == jj) & (sub_row >= j), s_j, att_i)
        att_rows.append(att_i)
    att = jnp.concatenate(att_rows, axis=1).astype(BF16)
    o_intra = jnp.einsum('cij,cjv->civ', att, vb, preferred_element_type=F32)
    b_last = bc[:, c - 1:c]
    qe = split((qq * jnp.exp(b)).astype(BF16))
    khat = (k3 * jnp.exp(b_last - bc)).astype(BF16)
    dec = jnp.exp(b_last)
    nt = (((1,), (1,)), ((), ()))
    outs = []
    for ci in range(nch):
        outs.append(o_intra[ci] + lax.dot_general(qe[ci], st.astype(BF16), nt, preferred_element_type=F32))
        st = st * dec[ci] + lax.dot_general(vb[ci], khat[ci], (((0,), (0,)), ((), ())),
                                            preferred_element_type=F32)
    return jnp.concatenate(outs, axis=0), st


def _hg_prompt_kernel(layer, gq_ref, gf_ref, gi_ref, gg_ref, lb_ref, gn_ref, tri_ref, mix_ref,
                      o_ref, s_ref, st_ref):
    del mix_ref

    @pl.when(pl.program_id(2) == 0)
    def _():
        st_ref[...] = jnp.zeros_like(st_ref)

    lb = _hg_lower_bound(lb_ref, layer)
    qq, kk, f = _hg_gates(gq_ref[...], gf_ref[...], lb)
    o, st = _hg_block(qq, kk, jnp.log(f), gi_ref[...], st_ref[...], tri_ref[...])
    o_ref[...] = _hg_out(o, gn_ref[...], gg_ref[...]).astype(o_ref.dtype)
    st_ref[...] = st

    @pl.when(pl.program_id(2) == pl.num_programs(2) - 1)
    def _():
        s_ref[0, 0] = st.T


def _hg_prompt(proj, mix, lb_raw, gn, layer, batch, seq):
    nt = seq // HG_ROWS
    c0 = MIX_HALF * 4 // HG_DK
    col = lambda off: pl.BlockSpec((HG_ROWS, HG_DK), lambda b, h, i: (b * nt + i, c0 + off + h))
    tri = jnp.asarray(np.kron(np.eye(HG_ROWS // HG_CHUNK), np.tril(np.ones((HG_CHUNK, HG_CHUNK)))), BF16)
    return pl.pallas_call(
        functools.partial(_hg_prompt_kernel, layer),
        grid=(batch, HG_HEADS, nt),
        in_specs=[col(0), col(HG_HEADS), col(2 * HG_HEADS), col(3 * HG_HEADS),
                  pl.BlockSpec((lb_raw.shape[0], HG_DK), lambda b, h, i: (0, h)),
                  pl.BlockSpec((1, HG_DK), lambda b, h, i: (0, h)),
                  pl.BlockSpec((HG_ROWS, HG_ROWS), lambda b, h, i: (0, 0)),
                  pl.BlockSpec(memory_space=pl.ANY)],
        out_specs=[pl.BlockSpec((HG_ROWS, HG_DK), lambda b, h, i: (b * nt + i, HG_HEADS + h)),
                   pl.BlockSpec((1, 1, HG_DK, HG_DK), lambda b, h, i: (b, h, 0, 0))],
        out_shape=[jax.ShapeDtypeStruct(mix.shape, mix.dtype),
                   jax.ShapeDtypeStruct((batch, HG_HEADS, HG_DK, HG_DK), F32)],
        scratch_shapes=[pltpu.VMEM((HG_DK, HG_DK), F32)],
        input_output_aliases={7: 0},
        compiler_params=_cparams(3),
        name="hgrn_prompt",
    )(proj, proj, proj, proj, lb_raw, gn.reshape(1, MIX_HALF), tri, mix)


def _hg_sample_kernel(layer, gq_ref, gf_ref, gi_ref, gg_ref, lb_ref, gn_ref, s_ref, mix_ref,
                      o_ref, so_ref):
    del mix_ref
    lb = _hg_lower_bound(lb_ref, layer)
    qq, kk, f = _hg_gates(gq_ref[...], gf_ref[...], lb)
    v = gi_ref[...]
    qt = _columns(qq, HG_DK)
    kt = _columns(kk, HG_DK)
    ft = _columns(f, HG_DK)
    rows = []
    for t in range(SAMPLE_TOK):
        s_new = s_ref[t, 0] * ft[:, t:t + 1] + kt[:, t:t + 1] * v[t:t + 1, :]
        so_ref[t, 0] = s_new
        rows.append(jnp.sum(qt[:, t:t + 1] * s_new, axis=0, keepdims=True))
    o = jnp.concatenate(rows, axis=0)
    o_ref[...] = _hg_out(o, gn_ref[...], gg_ref[...]).astype(o_ref.dtype)


def _hg_sample(proj, mix, state, lb_raw, gn, layer, row0):
    nb = state.shape[0]
    t = SAMPLE_TOK
    r0 = row0 // t
    c0 = MIX_HALF * 4 // HG_DK
    col = lambda off: pl.BlockSpec((t, HG_DK), lambda b, h: (r0 + b, c0 + off + h))
    st = pl.BlockSpec((t, 1, HG_DK, HG_DK), lambda b, h: (b, h, 0, 0))
    return pl.pallas_call(
        functools.partial(_hg_sample_kernel, layer),
        grid=(nb // t, HG_HEADS),
        in_specs=[col(0), col(HG_HEADS), col(2 * HG_HEADS), col(3 * HG_HEADS),
                  pl.BlockSpec((lb_raw.shape[0], HG_DK), lambda b, h: (0, h)),
                  pl.BlockSpec((1, HG_DK), lambda b, h: (0, h)),
                  st,
                  pl.BlockSpec(memory_space=pl.ANY)],
        out_specs=[pl.BlockSpec((t, HG_DK), lambda b, h: (r0 + b, HG_HEADS + h)), st],
        out_shape=[jax.ShapeDtypeStruct(mix.shape, mix.dtype),
                   jax.ShapeDtypeStruct(state.shape, F32)],
        input_output_aliases={7: 0},
        compiler_params=_cparams(2),
        name="hgrn_sample",
    )(proj, proj, proj, proj, lb_raw, gn.reshape(1, MIX_HALF), state, mix)


S5_SEG = S5_SUB // 8
S5_POW_ROWS = 16
assert S5_SEG & (S5_SEG - 1) == 0


def _cmul(ar, ai, br, bi):
    return ar * br - ai * bi, ar * bi + ai * br


def _s5_prep_kernel(lr_ref, li_ref, ldt_ref, brt_ref, bit_ref, pr_ref, pi_ref, bbr_ref, bbi_ref):
    lr = lr_ref[...]
    li = li_ref[...]
    dt = jnp.exp(ldt_ref[...])
    mag = jnp.exp(lr * dt)
    ar = mag * jnp.cos(li * dt)
    ai = mag * jnp.sin(li * dt)
    den = lr * lr + li * li
    cr = ((ar - 1.0) * lr + ai * li) / den
    ci = (ai * lr - (ar - 1.0) * li) / den
    brt = brt_ref[...]
    bit = bit_ref[...]
    bbr_ref[...] = cr * brt - ci * bit
    bbi_ref[...] = cr * bit + ci * brt
    n = pr_ref.shape[1]
    pr_ref[0:8, :] = jnp.broadcast_to(ar, (8, n))
    pi_ref[0:8, :] = jnp.broadcast_to(ai, (8, n))
    seg_pow = (ar, ai)
    for _ in range(S5_SEG.bit_length() - 1):
        seg_pow = _cmul(*seg_pow, *seg_pow)
    p = seg_pow
    for m in range(8):
        pr_ref[8 + m:9 + m, :] = p[0]
        pi_ref[8 + m:9 + m, :] = p[1]
        p = _cmul(*p, *seg_pow)


def _s5_prep(lam_re, lam_im, log_dt, b_re, b_im):
    n = S5_GROUPS * S5_STATE
    flat = lambda t: t.reshape(1, n)
    b_t = lambda t: jnp.transpose(t, (2, 0, 1)).reshape(S5_GROUP, n)
    ldt = jnp.repeat(log_dt, S5_STATE).reshape(1, n)
    return pl.pallas_call(
        _s5_prep_kernel,
        out_shape=[jax.ShapeDtypeStruct((S5_POW_ROWS, n), F32)] * 2
        + [jax.ShapeDtypeStruct((S5_GROUP, n), F32)] * 2,
        compiler_params=pltpu.CompilerParams(vmem_limit_bytes=VMEM_LIMIT_BYTES),
        name="s5_prep",
    )(flat(lam_re), flat(lam_im), ldt, b_t(b_re), b_t(b_im))


def _s5_block_diag(bbr, bbi, c_re, c_im):
    eye = jnp.eye(S5_BLK_GROUPS, dtype=F32)
    def b_blk(t):
        t = t.reshape(S5_GROUP, S5_NBLK, S5_BLK_GROUPS, S5_STATE)
        t = jnp.einsum('cbgp,hg->bhcgp', t, eye)
        return t.reshape(S5_NBLK, S5_BLK_CH, S5_BLK_ST).astype(BF16)
    def c_blk(t):
        t = t.reshape(S5_NBLK, S5_BLK_GROUPS, S5_GROUP, S5_STATE)
        t = jnp.einsum('bgcp,hg->bhpgc', t, eye)
        return t.reshape(S5_NBLK, S5_BLK_ST, S5_BLK_CH).astype(BF16)
    return (jnp.concatenate([b_blk(bbr), b_blk(bbi)], axis=2),
            jnp.concatenate([c_blk(c_re), c_blk(-c_im)], axis=1))


def _s5_readout(u, h_bf16, c_ref, d_ref):
    y = jnp.dot(h_bf16, c_ref[0], preferred_element_type=F32) + d_ref[...] * u
    return jax.nn.gelu(y)


def _s5_row_perm(rows):
    seg = rows // 8
    p = np.zeros((rows, rows), np.float32)
    t, s = np.meshgrid(np.arange(seg), np.arange(8), indexing="ij")
    p[(t * 8 + s).ravel(), (s * seg + t).ravel()] = 1.0
    return jnp.asarray(p, BF16), jnp.asarray(p.T, BF16)


def _s5_prompt_kernel(u_ref, p_ref, pt_ref, b_ref, c_ref, pr_ref, pi_ref, d_ref, z_in_ref,
                      z_ref, hr_out, hi_out, *scratch):
    del z_in_ref
    x_refs = scratch[:S5_NSUB]
    hb_refs = scratch[S5_NSUB:2 * S5_NSUB]
    us_ref, cr_carry, ci_carry = scratch[2 * S5_NSUB:]
    tb = pl.program_id(2)
    n = S5_BLK_ST
    ch = S5_BLK_CH

    @pl.when(tb == 0)
    def _():
        cr_carry[...] = jnp.zeros_like(cr_carry)
        ci_carry[...] = jnp.zeros_like(ci_carry)

    for q in range(S5_NSUB):
        rows = slice(q * S5_SUB, (q + 1) * S5_SUB)
        up = jnp.dot(p_ref[...], jnp.concatenate(_split3(u_ref[rows, :]), axis=1),
                     preferred_element_type=F32)
        us_ref[rows, :] = (up[:, :ch] + up[:, ch:2 * ch]) + up[:, 2 * ch:]
        x_refs[q][...] = jnp.dot(up[:, :ch].astype(BF16), b_ref[0], preferred_element_type=F32)

    a_r = pr_ref[0:8, :]
    a_i = pi_ref[0:8, :]
    seg0 = 8
    row = lax.broadcasted_iota(jnp.int32, (8, n), 0)
    h0r = cr_carry[...]
    h0i = ci_carry[...]
    for q in range(S5_NSUB):
        x_ref = x_refs[q]
        def advance(t, h, x_ref=x_ref):
            rows = pl.ds(pl.multiple_of(t * 8, 8), 8)
            dr, di = _cmul(a_r, a_i, *h)
            return dr + x_ref[rows, 0:n], di + x_ref[rows, n:2 * n]

        zero = jnp.zeros((8, n), F32)
        er, ei = lax.fori_loop(0, S5_SEG, advance, (zero, zero), unroll=4)
        for s in range(3):
            m = 1 << s
            g_r = pr_ref[seg0 + m - 1:seg0 + m, :]
            g_i = pi_ref[seg0 + m - 1:seg0 + m, :]
            sr = jnp.where(row >= m, pltpu.roll(er, m, 0), 0.0)
            si = jnp.where(row >= m, pltpu.roll(ei, m, 0), 0.0)
            dr, di = _cmul(g_r, g_i, sr, si)
            er, ei = er + dr, ei + di
        dr, di = _cmul(pr_ref[seg0:seg0 + 8, :], pi_ref[seg0:seg0 + 8, :], h0r, h0i)
        er, ei = er + dr, ei + di
        cm_r = jnp.where(row >= 1, pltpu.roll(er, 1, 0), h0r)
        cm_i = jnp.where(row >= 1, pltpu.roll(ei, 1, 0), h0i)
        h0r = er[7:8, :]
        h0i = ei[7:8, :]
        def emit(t2, h, hb_ref=hb_refs[q]):
            h1 = advance(2 * t2, h)
            h2 = advance(2 * t2 + 1, h1)
            rows = pl.ds(pl.multiple_of(t2 * 16, 16), 16)
            hb_ref[rows, 0:n] = jnp.concatenate([h1[0], h2[0]], axis=0).astype(BF16)
            hb_ref[rows, n:2 * n] = jnp.concatenate([h1[1], h2[1]], axis=0).astype(BF16)
            return h2

        lax.fori_loop(0, S5_SEG // 2, emit, (cm_r, cm_i), unroll=2)
        rows = slice(q * S5_SUB, (q + 1) * S5_SUB)
        zp = _s5_readout(us_ref[rows, :], hb_refs[q][...], c_ref, d_ref).astype(BF16)
        z_ref[rows, :] = jnp.dot(pt_ref[...], zp, preferred_element_type=F32).astype(z_ref.dtype)
    cr_carry[...] = h0r
    ci_carry[...] = h0i

    @pl.when(tb == pl.num_programs(2) - 1)
    def _():
        hr_out[0, 0] = h0r
        hi_out[0, 0] = h0i


def _s5_prompt(u, z, mats, pows, d, batch, seq):
    b_m, c_m = mats
    p_r, p_i = pows
    perm, perm_t = _s5_row_perm(S5_SUB)
    nt = seq // S5_ROWS
    blk3 = lambda r, c: pl.BlockSpec((1, r, c), lambda b, j, i: (j, 0, 0))
    whole = pl.BlockSpec((S5_SUB, S5_SUB), lambda b, j, i: (0, 0))
    st_spec = pl.BlockSpec((1, 1, 1, S5_BLK_ST), lambda b, j, i: (b, j, 0, 0))
    st_shape = jax.ShapeDtypeStruct((batch, S5_NBLK, 1, S5_BLK_ST), F32)
    return pl.pallas_call(
        _s5_prompt_kernel,
        grid=(batch, S5_NBLK, nt),
        in_specs=[pl.BlockSpec((S5_ROWS, S5_BLK_CH), lambda b, j, i: (b * nt + i, j)),
                  whole, whole,
                  blk3(S5_BLK_CH, 2 * S5_BLK_ST), blk3(2 * S5_BLK_ST, S5_BLK_CH),
                  pl.BlockSpec((S5_POW_ROWS, S5_BLK_ST), lambda b, j, i: (0, j)),
                  pl.BlockSpec((S5_POW_ROWS, S5_BLK_ST), lambda b, j, i: (0, j)),
                  pl.BlockSpec((1, S5_BLK_CH), lambda b, j, i: (0, j)),
                  pl.BlockSpec(memory_space=pl.ANY)],
        out_specs=[pl.BlockSpec((S5_ROWS, S5_BLK_CH), lambda b, j, i: (b * nt + i, j)), st_spec, st_spec],
        out_shape=[jax.ShapeDtypeStruct(z.shape, z.dtype), st_shape, st_shape],
        input_output_aliases={8: 0},
        scratch_shapes=[pltpu.VMEM((S5_SUB, 2 * S5_BLK_ST), F32) for _ in range(S5_NSUB)]
        + [pltpu.VMEM((S5_SUB, 2 * S5_BLK_ST), BF16) for _ in range(S5_NSUB)]
        + [pltpu.VMEM((S5_ROWS, S5_BLK_CH), F32),
           pltpu.VMEM((1, S5_BLK_ST), F32), pltpu.VMEM((1, S5_BLK_ST), F32)],
        compiler_params=_cparams(3),
        name="s5_prompt",
    )(u, perm, perm_t, b_m, c_m, p_r, p_i, d.reshape(1, D_MODEL), z)


S5_T = 128
S5_HALF_ST = S5_BLK_ST // 2


def _s5_tb_mats(bbr, bbi, c_re, c_im, a_r, a_i, batch):
    hg = S5_BLK_GROUPS // 2
    eye = jnp.eye(hg, dtype=F32)
    def b_blk(t):
        t = t.reshape(S5_GROUP, S5_NBLK, 2, hg, S5_STATE)
        return jnp.einsum('cbhgp,kg->bhgckp', t, eye)
    def c_blk(t):
        t = t.reshape(S5_NBLK, 2, hg, S5_GROUP, S5_STATE)
        return jnp.einsum('bhgcp,kg->bkphgc', t, eye)
    b_m = jnp.stack([b_blk(bbr), b_blk(bbi)], axis=4)
    c_m = jnp.stack([c_blk(c_re), c_blk(-c_im)], axis=1)
    tile_a = lambda a: jnp.broadcast_to(a.reshape(S5_NBLK, 2, 1, S5_HALF_ST),
                                        (S5_NBLK, 2, batch, S5_HALF_ST)).reshape(S5_NBLK, 2 * batch, S5_HALF_ST)
    return (b_m.reshape(S5_NBLK, S5_BLK_CH, S5_BLK_ST).astype(BF16),
            c_m.reshape(S5_NBLK, S5_BLK_ST, S5_BLK_CH).astype(BF16), tile_a(a_r), tile_a(a_i))


S5_NB = 2


def _s5_tb_kernel(u_ref, b_ref, c_ref, ar_ref, ai_ref, d_ref, z_in_ref, z_ref, hr_out, hi_out,
                  x_ref, hb_ref, hr_ref, hi_ref):
    del z_in_ref
    i = pl.program_id(1)
    n = S5_HALF_ST
    ch = S5_BLK_CH
    t_steps = u_ref.shape[0] // 4
    half_rows = 4 * t_steps

    @pl.when(i == 0)
    def _():
        hr_ref[...] = jnp.zeros_like(hr_ref)
        hi_ref[...] = jnp.zeros_like(hi_ref)

    first4 = lax.broadcasted_iota(jnp.int32, (1, 8, ch), 1) < 4
    half0 = lax.broadcasted_iota(jnp.int32, (1, 8, ch), 2) < ch // 2
    for k in range(S5_NB):
        u3 = u_ref[:, k * ch:(k + 1) * ch].reshape(t_steps // 2, 8, ch)
        r3 = pltpu.roll(u3, 4, 1)
        even = jnp.where(first4, jnp.where(half0, u3, 0.0), jnp.where(half0, 0.0, r3))
        odd = jnp.where(first4, jnp.where(half0, r3, 0.0), jnp.where(half0, 0.0, u3))
        lhs = jnp.stack([even, odd], axis=1).reshape(8 * t_steps, ch).astype(BF16)
        for r0 in (0, half_rows):
            x_ref[k, r0:r0 + half_rows, :] = jnp.dot(lhs[r0:r0 + half_rows], b_ref[k],
                                                     preferred_element_type=F32)
    a = [(ar_ref[k], ai_ref[k]) for k in range(S5_NB)]

    def advance(t, hs):
        rows = pl.ds(pl.multiple_of(t * 8, 8), 8)
        out = []
        for k in range(S5_NB):
            dr, di = _cmul(*a[k], *hs[k])
            out.append((dr + x_ref[k, rows, 0:n], di + x_ref[k, rows, n:2 * n]))
        return tuple(out)

    def emit(t2, hs):
        h1 = advance(2 * t2, hs)
        h2 = advance(2 * t2 + 1, h1)
        rows = pl.ds(pl.multiple_of(t2 * 16, 16), 16)
        for k in range(S5_NB):
            hb_ref[k, rows, 0:n] = jnp.concatenate([h1[k][0], h2[k][0]], axis=0).astype(BF16)
            hb_ref[k, rows, n:2 * n] = jnp.concatenate([h1[k][1], h2[k][1]], axis=0).astype(BF16)
        return h2

    hs = lax.fori_loop(0, t_steps // 2, emit, tuple((hr_ref[k], hi_ref[k]) for k in range(S5_NB)), unroll=2)
    for k in range(S5_NB):
        hr_ref[k] = hs[k][0]
        hi_ref[k] = hs[k][1]

    pick = lambda yv: jnp.where(half0, yv, pltpu.roll(yv, 4, 1))
    for k in range(S5_NB):
        y2 = jnp.concatenate([jnp.dot(hb_ref[k, r0:r0 + half_rows, :], c_ref[k], preferred_element_type=F32)
                              for r0 in (0, half_rows)], axis=0)
        y4 = y2.reshape(t_steps // 2, 2, 8, ch)
        y = jnp.where(first4, pick(y4[:, 0]), pltpu.roll(pick(y4[:, 1]), 4, 1)).reshape(4 * t_steps, ch)
        cols = slice(k * ch, (k + 1) * ch)
        z_ref[:, cols] = jax.nn.gelu(y + d_ref[:, cols] * u_ref[:, cols]).astype(z_ref.dtype)

    @pl.when(i == pl.num_programs(1) - 1)
    def _():
        for k in range(S5_NB):
            for h in range(2):
                cols = slice((2 * k + h) * n, (2 * k + h + 1) * n)
                hr_out[:, cols] = hs[k][0][4 * h:4 * h + 4, :]
                hi_out[:, cols] = hs[k][1][4 * h:4 * h + 4, :]


def _s5_prompt_tb(u, z, mats, d, batch, seq):
    assert batch == 4
    b_m, c_m, a_r, a_i = mats
    rows = batch * S5_T
    nt = seq // S5_T
    nb = S5_NB
    blk3 = lambda r, c: pl.BlockSpec((nb, r, c), lambda j, i: (j, 0, 0))
    st_spec = pl.BlockSpec((batch, nb * S5_BLK_ST), lambda j, i: (0, j))
    st_shape = jax.ShapeDtypeStruct((batch, S5_GROUPS * S5_STATE), F32)
    return pl.pallas_call(
        _s5_tb_kernel,
        grid=(S5_NBLK // nb, nt),
        in_specs=[pl.BlockSpec((rows, nb * S5_BLK_CH), lambda j, i: (i, j)),
                  blk3(S5_BLK_CH, S5_BLK_ST), blk3(S5_BLK_ST, S5_BLK_CH),
                  blk3(2 * batch, S5_HALF_ST), blk3(2 * batch, S5_HALF_ST),
                  pl.BlockSpec((1, nb * S5_BLK_CH), lambda j, i: (0, j)),
                  pl.BlockSpec(memory_space=pl.ANY)],
        out_specs=[pl.BlockSpec((rows, nb * S5_BLK_CH), lambda j, i: (i, j)), st_spec, st_spec],
        out_shape=[jax.ShapeDtypeStruct(z.shape, z.dtype), st_shape, st_shape],
        scratch_shapes=[pltpu.VMEM((nb, 2 * rows, S5_BLK_ST), F32), pltpu.VMEM((nb, 2 * rows, S5_BLK_ST), BF16),
                        pltpu.VMEM((nb, 2 * batch, S5_HALF_ST), F32), pltpu.VMEM((nb, 2 * batch, S5_HALF_ST), F32)],
        input_output_aliases={6: 0},
        compiler_params=_cparams(2),
        name="s5_prompt",
    )(u, b_m, c_m, a_r, a_i, d.reshape(1, D_MODEL), z)


def _s5_sample_kernel(u_ref, b_ref, c_ref, pr_ref, pi_ref, d_ref, h0r_ref, h0i_ref,
                      z_in_ref, z_ref, hr_out, hi_out):
    del z_in_ref
    n = S5_BLK_ST
    u = u_ref[...]
    x = jnp.dot(u.astype(BF16), b_ref[0], preferred_element_type=F32)
    dr, di = _cmul(pr_ref[0:1, :], pi_ref[0:1, :], h0r_ref[...], h0i_ref[...])
    hr = x[:, 0:n] + dr
    hi = x[:, n:2 * n] + di
    hr_out[...] = hr
    hi_out[...] = hi
    h = jnp.concatenate([hr, hi], axis=1).astype(BF16)
    z_ref[...] = _s5_readout(u, h, c_ref, d_ref).astype(z_ref.dtype)


def _s5_sample(u, z, mats, pows, d, h0r, h0i, row0):
    b_m, c_m = mats
    p_r, p_i = pows
    nb = h0r.shape[0]
    r0 = row0 // nb
    blk3 = lambda r, c: pl.BlockSpec((1, r, c), lambda j: (j, 0, 0))
    st_spec = pl.BlockSpec((nb, S5_BLK_ST), lambda j: (0, j))
    st_shape = jax.ShapeDtypeStruct(h0r.shape, F32)
    return pl.pallas_call(
        _s5_sample_kernel,
        grid=(S5_NBLK,),
        in_specs=[pl.BlockSpec((nb, S5_BLK_CH), lambda j: (r0, j)),
                  blk3(S5_BLK_CH, 2 * S5_BLK_ST), blk3(2 * S5_BLK_ST, S5_BLK_CH),
                  pl.BlockSpec((S5_POW_ROWS, S5_BLK_ST), lambda j: (0, j)),
                  pl.BlockSpec((S5_POW_ROWS, S5_BLK_ST), lambda j: (0, j)),
                  pl.BlockSpec((1, S5_BLK_CH), lambda j: (0, j)),
                  st_spec, st_spec,
                  pl.BlockSpec(memory_space=pl.ANY)],
        out_specs=[pl.BlockSpec((nb, S5_BLK_CH), lambda j: (r0, j)), st_spec, st_spec],
        out_shape=[jax.ShapeDtypeStruct(z.shape, z.dtype), st_shape, st_shape],
        input_output_aliases={8: 0},
        compiler_params=_cparams(1),
        name="s5_sample",
    )(u, b_m, c_m, p_r, p_i, d.reshape(1, D_MODEL), h0r, h0i, z)


TB_T = 64
LANES = 128


def _interleave_norm_kernel(n_steps, n_sample, batch, *refs):
    x_refs = refs[:batch]
    xs_ref, g_ref, xt_ref, ut_ref, slab_ref = refs[batch:]
    i = pl.program_id(0)
    d = xt_ref.shape[1]

    @pl.when(i < n_steps)
    def _():
        for dst_ref, norm in ((xt_ref, False), (ut_ref, True)):
            for b, x_ref in enumerate(x_refs):
                v = x_ref[...]
                if norm:
                    v = _rms(v, g_ref[...])
                for s in range(d // LANES):
                    slab_ref[s, pl.ds(b, TB_T, stride=batch), :] = v[:, s * LANES:(s + 1) * LANES]
            for s in range(d // LANES):
                dst_ref[:, s * LANES:(s + 1) * LANES] = slab_ref[s]

    @pl.when(i >= n_steps)
    def _():
        v = xs_ref[...]
        xt_ref[0:n_sample, :] = v
        ut_ref[0:n_sample, :] = _rms(v, g_ref[...])


def _interleave_norm(x, g, batch, seq):
    m, d = x.shape
    n_steps = seq // TB_T
    ms = m - batch * seq
    rows = batch * TB_T
    per_batch = lambda b: pl.BlockSpec((TB_T, d), lambda i: (b * n_steps + jnp.minimum(i, n_steps - 1), 0))
    out = pl.BlockSpec((rows, d), lambda i: (i, 0))
    return pl.pallas_call(
        functools.partial(_interleave_norm_kernel, n_steps, ms, batch),
        grid=(n_steps + 1,),
        in_specs=[per_batch(b) for b in range(batch)]
        + [pl.BlockSpec((ms, d), lambda i: (batch * seq // ms, 0)), pl.BlockSpec((1, d), lambda i: (0, 0))],
        out_specs=[out, out],
        out_shape=[jax.ShapeDtypeStruct((m, d), F32)] * 2,
        scratch_shapes=[pltpu.VMEM((d // LANES, rows, LANES), F32)],
        compiler_params=_cparams(1),
        name="interleave_norm",
    )(*([x] * (batch + 1)), g.reshape(1, d))


def _deinterleave_norm_kernel(n_steps, n_sample, batch, x_ref, g_ref, y_ref, ys_ref, slab_ref):
    i = pl.program_id(0)
    d = x_ref.shape[1]

    @pl.when(i < n_steps)
    def _():
        y = _rms(x_ref[...], g_ref[...])
        for s in range(d // LANES):
            slab_ref[s] = y[:, s * LANES:(s + 1) * LANES]
        for b in range(batch):
            for s in range(d // LANES):
                y_ref[b, :, s * LANES:(s + 1) * LANES] = slab_ref[s, pl.ds(b, TB_T, stride=batch), :]

    @pl.when(i >= n_steps)
    def _():
        ys_ref[...] = _rms(x_ref[0:n_sample, :], g_ref[...])


def _deinterleave_norm(x, g, batch, seq):
    m, d = x.shape
    n_steps = seq // TB_T
    ms = m - batch * seq
    rows = batch * TB_T
    return pl.pallas_call(
        functools.partial(_deinterleave_norm_kernel, n_steps, ms, batch),
        grid=(n_steps + 1,),
        in_specs=[pl.BlockSpec((rows, d), lambda i: (i, 0)), pl.BlockSpec((1, d), lambda i: (0, 0))],
        out_specs=[pl.BlockSpec((batch, TB_T, d), lambda i: (0, jnp.minimum(i, n_steps - 1), 0)),
                   pl.BlockSpec((ms, d), lambda i: (0, 0))],
        out_shape=[jax.ShapeDtypeStruct((batch, seq, d), F32), jax.ShapeDtypeStruct((ms, d), F32)],
        scratch_shapes=[pltpu.VMEM((d // LANES, rows, LANES), F32)],
        compiler_params=_cparams(1),
        name="deinterleave_norm",
    )(x, g.reshape(1, d))


def _rope_tables(pos):
    inv = ROPE_BASE ** (-jnp.arange(RET_HALF, dtype=F32) / RET_HALF)
    ang = pos[:, None] * inv[None, :]
    return jnp.cos(ang), jnp.sin(ang)


def _ffn(x, norm_g, wg, wu, wd, layer):
    h = _rmsnorm(x, norm_g[layer], [BF16])[0]
    t = _matmul(h, [wg, wu], _epi_swiglu, BF16, layer=layer, name="ffn_gate_up")
    return _matmul(t, [wd], _epi_id, F32, res=x, layer=layer, name="ffn_down")


def kernel(x_prompt, x_sample, state_ret, state_hgrn, state_s5_re, state_s5_im, attn_norm_g, w_in, ret_gn_g, hg_lb, hg_gn_g, w_out, ssm_norm_g, s5_lam_re, s5_lam_im, s5_log_dt, s5_b_re, s5_b_im, s5_c_re, s5_c_im, s5_d, w_glu_a, w_glu_b, ffn_norm_g, w_ffn_gate, w_ffn_up, w_ffn_down, final_norm_g):
    one = lambda t: t.reshape(t.shape[1:])
    (state_ret, state_hgrn, state_s5_re, state_s5_im, attn_norm_g, w_in, ret_gn_g, hg_gn_g, w_out, ssm_norm_g,
     s5_lam_re, s5_lam_im, s5_log_dt, s5_b_re, s5_b_im, s5_c_re, s5_c_im, s5_d, w_glu_a, w_glu_b) = map(one, (
         state_ret, state_hgrn, state_s5_re, state_s5_im, attn_norm_g, w_in, ret_gn_g, hg_gn_g, w_out, ssm_norm_g,
         s5_lam_re, s5_lam_im, s5_log_dt, s5_b_re, s5_b_im, s5_c_re, s5_c_im, s5_d, w_glu_a, w_glu_b))
    cos_p, sin_p = _rope_tables(jnp.arange(SEQ, dtype=F32))
    cos_s, sin_s = _rope_tables(jnp.full((1,), float(PAST_LEN), F32))

    h, x = _rmsnorm_stack(x_prompt.reshape(N_PROMPT, D_MODEL), x_sample.reshape(DEC_BATCH, D_MODEL), attn_norm_g)
    proj = _matmul(h, [w_in], _epi_id, F32, tn=1024, name="w_in")
    mix = jnp.zeros((N_ROWS, D_MODEL), BF16)
    mix, ret_p = _ret_prompt(proj, mix, cos_p, sin_p, ret_gn_g, BATCH, SEQ)
    mix, hg_p = _hg_prompt(proj, mix, hg_lb, hg_gn_g, 0, BATCH, SEQ)
    mix, ret_s = _ret_sample(proj, mix, state_ret, cos_s, sin_s, ret_gn_g, N_PROMPT)
    mix, hg_s = _hg_sample(proj, mix, state_hgrn, hg_lb, hg_gn_g, 0, N_PROMPT)
    x = _matmul(mix, [w_out], _epi_id, F32, res=x, name="w_out")
    x = _ffn(x, ffn_norm_g, w_ffn_gate, w_ffn_up, w_ffn_down, 0)

    x, u = _interleave_norm(x, ssm_norm_g, BATCH, SEQ)
    p_r, p_i, bbr, bbi = _s5_prep(s5_lam_re, s5_lam_im, s5_log_dt, s5_b_re, s5_b_im)
    mats = _s5_block_diag(bbr, bbi, s5_c_re, s5_c_im)
    mats_tb = _s5_tb_mats(bbr, bbi, s5_c_re, s5_c_im, p_r[0], p_i[0], BATCH)
    z = jnp.zeros((N_ROWS, D_MODEL), BF16)
    z, s5r_p, s5i_p = _s5_prompt_tb(u, z, mats_tb, s5_d, BATCH, SEQ)
    n_st = S5_GROUPS * S5_STATE
    z, s5r_s, s5i_s = _s5_sample(u, z, mats, (p_r, p_i), s5_d, state_s5_re.reshape(DEC_BATCH, n_st),
                                 state_s5_im.reshape(DEC_BATCH, n_st), N_PROMPT)
    x = _matmul(z, [w_glu_a, w_glu_b], _epi_glu, F32, res=x, name="glu")
    x = _ffn(x, ffn_norm_g, w_ffn_gate, w_ffn_up, w_ffn_down, 1)

    y_p, y_s = _deinterleave_norm(x, final_norm_g, BATCH, SEQ)
    st = lambda t, b: t.reshape(1, b, S5_GROUPS, S5_STATE)
    return (y_p, y_s.reshape(DEC_BATCH, 1, D_MODEL),
            ret_p[None], ret_s[None], hg_p[None], hg_s[None],
            st(s5r_p, BATCH), st(s5i_p, BATCH), st(s5r_s, DEC_BATCH), st(s5i_s, DEC_BATCH))
```

```python
import functools
import math

import numpy as np
import jax
import jax.numpy as jnp
from jax import lax
from jax.experimental import pallas as pl
from jax.experimental.pallas import tpu as pltpu

F32 = jnp.float32
BF16 = jnp.bfloat16

D_MODEL = 2048
BATCH = 4
SEQ = 2048
DEC_BATCH = 128
PAST_LEN = 16384
N_PROMPT = BATCH * SEQ
N_ROWS = N_PROMPT + DEC_BATCH
MIX_HALF = D_MODEL // 2
RET_HEADS = 4
RET_DK = MIX_HALF // RET_HEADS
RET_HALF = RET_DK // 2
RET_CHUNK = 128
RET_ROWS = 512
HG_HEADS = 8
HG_DK = MIX_HALF // HG_HEADS
HG_CHUNK = 64
HG_SUB = 16
HG_ROWS = 256
HG_NH = 4
HG_SAFE_SPAN = 40.0
S5_GROUP = 16
S5_GROUPS = D_MODEL // S5_GROUP
S5_STATE = 64
S5_BLK_GROUPS = 16
S5_BLK_CH = S5_BLK_GROUPS * S5_GROUP
S5_BLK_ST = S5_BLK_GROUPS * S5_STATE
S5_NBLK = S5_GROUPS // S5_BLK_GROUPS
S5_SUB = 256
S5_NSUB = 2
S5_ROWS = S5_SUB * S5_NSUB
D_FF = 5632
ROPE_BASE = 10000.0
EPS = 1e-6
SAMPLE_TOK = 16

VMEM_LIMIT_BYTES = 56 * 1024 * 1024


def _cparams(n_axes):
    return pltpu.CompilerParams(dimension_semantics=("arbitrary",) * n_axes,
                                vmem_limit_bytes=VMEM_LIMIT_BYTES)


def _silu(x):
    return x * jax.nn.sigmoid(x)


def _rmsnorm_kernel(x_ref, g_ref, *o_refs):
    x = x_ref[...]
    y = x * lax.rsqrt(jnp.mean(x * x, axis=-1, keepdims=True) + EPS) * g_ref[...]
    for o_ref in o_refs:
        o_ref[...] = y.astype(o_ref.dtype)


def _rmsnorm(x, g, dtypes, tm=320):
    m, d = x.shape
    return pl.pallas_call(
        _rmsnorm_kernel,
        grid=(m // tm,),
        in_specs=[pl.BlockSpec((tm, d), lambda i: (i, 0)),
                  pl.BlockSpec((1, d), lambda i: (0, 0))],
        out_specs=[pl.BlockSpec((tm, d), lambda i: (i, 0)) for _ in dtypes],
        out_shape=[jax.ShapeDtypeStruct((m, d), dt) for dt in dtypes],
        compiler_params=_cparams(1),
        name="rmsnorm",
    )(x, g.reshape(1, d))


ROW_TILE = 512


def _rms(x, g):
    return x * lax.rsqrt(jnp.mean(x * x, axis=-1, keepdims=True) + EPS) * g


def _rmsnorm_stack_kernel(n_prompt_tiles, n_sample, xp_ref, xs_ref, g_ref, h_ref, x_ref):
    i = pl.program_id(0)

    @pl.when(i < n_prompt_tiles)
    def _():
        x = xp_ref[...]
        x_ref[...] = x
        h_ref[...] = _rms(x, g_ref[...]).astype(h_ref.dtype)

    @pl.when(i >= n_prompt_tiles)
    def _():
        x = xs_ref[...]
        x_ref[0:n_sample, :] = x
        h_ref[0:n_sample, :] = _rms(x, g_ref[...]).astype(h_ref.dtype)


def _rmsnorm_stack(x_prompt, x_sample, g):
    (mp, d), ms = x_prompt.shape, x_sample.shape[0]
    tm = ROW_TILE
    npt = mp // tm
    m = mp + ms
    return pl.pallas_call(
        functools.partial(_rmsnorm_stack_kernel, npt, ms),
        grid=(npt + 1,),
        in_specs=[pl.BlockSpec((tm, d), lambda i: (jnp.minimum(i, npt - 1), 0)),
                  pl.BlockSpec((ms, d), lambda i: (0, 0)),
                  pl.BlockSpec((1, d), lambda i: (0, 0))],
        out_specs=[pl.BlockSpec((tm, d), lambda i: (i, 0))] * 2,
        out_shape=[jax.ShapeDtypeStruct((m, d), BF16), jax.ShapeDtypeStruct((m, d), F32)],
        compiler_params=_cparams(1),
        name="rmsnorm_stack",
    )(x_prompt, x_sample, g.reshape(1, d))


def _rmsnorm_split_kernel(n_prompt_tiles, n_sample, x_ref, g_ref, yp_ref, ys_ref):
    i = pl.program_id(0)

    @pl.when(i < n_prompt_tiles)
    def _():
        yp_ref[...] = _rms(x_ref[...], g_ref[...])

    @pl.when(i >= n_prompt_tiles)
    def _():
        ys_ref[...] = _rms(x_ref[0:n_sample, :], g_ref[...])


def _rmsnorm_split(x, g, n_prompt):
    m, d = x.shape
    tm = ROW_TILE
    npt = n_prompt // tm
    ms = m - n_prompt
    return pl.pallas_call(
        functools.partial(_rmsnorm_split_kernel, npt, ms),
        grid=(npt + 1,),
        in_specs=[pl.BlockSpec((tm, d), lambda i: (i, 0)),
                  pl.BlockSpec((1, d), lambda i: (0, 0))],
        out_specs=[pl.BlockSpec((tm, d), lambda i: (jnp.minimum(i, npt - 1), 0)),
                   pl.BlockSpec((ms, d), lambda i: (0, 0))],
        out_shape=[jax.ShapeDtypeStruct((n_prompt, d), F32), jax.ShapeDtypeStruct((ms, d), F32)],
        compiler_params=_cparams(1),
        name="rmsnorm_split",
    )(x, g.reshape(1, d))


def _mm_kernel(n_w, epilogue, has_res, a_ref, *refs):
    w_refs = refs[:n_w]
    res_ref = refs[n_w] if has_res else None
    o_ref = refs[n_w + has_res]
    wb_refs = refs[n_w + has_res + 1:]

    @pl.when(pl.program_id(1) == 0)
    def _():
        for w_ref, wb_ref in zip(w_refs, wb_refs):
            wb_ref[...] = w_ref[...].astype(BF16)

    a = a_ref[...]
    y = epilogue(*[jnp.dot(a, wb_ref[...], preferred_element_type=F32) for wb_ref in wb_refs])
    if has_res:
        y = res_ref[...] + y
    o_ref[...] = y.astype(o_ref.dtype)


def _matmul(a, ws, epilogue, out_dtype, res=None, layer=None, tm=640, tn=512, name="matmul"):
    m, k = a.shape
    n = ws[0].shape[-1]
    in_specs = [pl.BlockSpec((tm, k), lambda j, i: (i, 0))]
    if layer is None:
        in_specs += [pl.BlockSpec((k, tn), lambda j, i: (0, j)) for _ in ws]
    else:
        in_specs += [pl.BlockSpec((None, k, tn), lambda j, i: (layer, 0, j)) for _ in ws]
    args = [a, *ws]
    if res is not None:
        in_specs.append(pl.BlockSpec((tm, tn), lambda j, i: (i, j)))
        args.append(res)
    return pl.pallas_call(
        functools.partial(_mm_kernel, len(ws), epilogue, res is not None),
        grid=(n // tn, m // tm),
        in_specs=in_specs,
        out_specs=pl.BlockSpec((tm, tn), lambda j, i: (i, j)),
        out_shape=jax.ShapeDtypeStruct((m, n), out_dtype),
        scratch_shapes=[pltpu.VMEM((k, tn), BF16) for _ in ws],
        compiler_params=_cparams(2),
        name=name,
    )(*args)


def _epi_id(y):
    return y


def _epi_swiglu(g, u):
    return _silu(g) * u


def _epi_glu(a, b):
    return a * jax.nn.sigmoid(b)


def _rotary(x, cos, sin):
    x1 = x[:, :RET_HALF]
    x2 = x[:, RET_HALF:]
    return jnp.concatenate([x1 * cos - x2 * sin, x1 * sin + x2 * cos], axis=1)


def _group_norm_gate(o, gn, g):
    mu = jnp.mean(o, axis=-1, keepdims=True)
    d = o - mu
    var = jnp.mean(d * d, axis=-1, keepdims=True)
    return d * lax.rsqrt(var + EPS) * gn * _silu(g)


def _ret_prompt_kernel(q_ref, k_ref, v_ref, g_ref, cos_ref, sin_ref, intra_ref, qdec_ref, kdec_ref,
                       cdec_ref, gn_ref, mix_ref, o_ref, s_ref):
    del mix_ref

    @pl.when(pl.program_id(2) == 0)
    def _():
        s_ref[...] = jnp.zeros_like(s_ref)

    s = s_ref[0, 0]
    for c0 in range(0, RET_ROWS, RET_CHUNK):
        rows = pl.ds(c0, RET_CHUNK)
        cos = cos_ref[rows, :]
        sin = sin_ref[rows, :]
        q = _rotary(q_ref[rows, :], cos, sin)
        k = _rotary(k_ref[rows, :], cos, sin) * (RET_DK ** -0.5)
        v = v_ref[rows, :].astype(BF16)
        att = lax.dot_general(q.astype(BF16), k.astype(BF16), (((1,), (1,)), ((), ())),
                              preferred_element_type=F32) * intra_ref[0]
        o = (jnp.dot(att.astype(BF16), v, preferred_element_type=F32)
             + jnp.dot((q * qdec_ref[0]).astype(BF16), s.astype(BF16), preferred_element_type=F32))
        s = s * cdec_ref[0, 0:1, :] + lax.dot_general(
            (k * kdec_ref[0]).astype(BF16), v, (((0,), (0,)), ((), ())), preferred_element_type=F32)
        o_ref[rows, :] = _group_norm_gate(o, gn_ref[...], g_ref[rows, :]).astype(o_ref.dtype)
    s_ref[0, 0] = s


def _ret_decay_tables(c):
    lg = np.log(1.0 - 2.0 ** (-5.0 - np.arange(RET_HEADS, dtype=np.float64)))
    idx = np.arange(c, dtype=np.float64)
    diff = idx[:, None] - idx[None, :]
    intra = np.where(diff >= 0, np.exp(np.maximum(diff, 0.0)[None] * lg[:, None, None]), 0.0)
    ones = np.ones((1, 1, RET_DK))
    qdec = np.exp((idx[None, :, None] + 1.0) * lg[:, None, None]) * ones
    kdec = np.exp((c - 1.0 - idx[None, :, None]) * lg[:, None, None]) * ones
    cdec = np.exp(c * lg)[:, None, None] * np.ones((1, 8, RET_DK))
    return [jnp.asarray(t, F32) for t in (intra, qdec, kdec, cdec)]


def _ret_prompt(proj, mix, cos, sin, gn, batch, seq):
    c = RET_CHUNK
    r = RET_ROWS
    nc = seq // r
    intra, qdec, kdec, cdec = _ret_decay_tables(c)
    col = lambda off: pl.BlockSpec((r, RET_DK), lambda b, h, i: (b * nc + i, off + h))
    head3 = lambda r: pl.BlockSpec((1, r, RET_DK), lambda b, h, i: (h, 0, 0))
    return pl.pallas_call(
        _ret_prompt_kernel,
        grid=(batch, RET_HEADS, nc),
        in_specs=[col(0), col(RET_HEADS), col(2 * RET_HEADS), col(3 * RET_HEADS),
                  pl.BlockSpec((r, RET_HALF), lambda b, h, i: (i, 0)),
                  pl.BlockSpec((r, RET_HALF), lambda b, h, i: (i, 0)),
                  pl.BlockSpec((1, c, c), lambda b, h, i: (h, 0, 0)),
                  head3(c), head3(c), head3(8),
                  pl.BlockSpec((1, RET_DK), lambda b, h, i: (0, h)),
                  pl.BlockSpec(memory_space=pl.ANY)],
        out_specs=[pl.BlockSpec((r, RET_DK), lambda b, h, i: (b * nc + i, h)),
                   pl.BlockSpec((1, 1, RET_DK, RET_DK), lambda b, h, i: (b, h, 0, 0))],
        out_shape=[jax.ShapeDtypeStruct(mix.shape, mix.dtype),
                   jax.ShapeDtypeStruct((batch, RET_HEADS, RET_DK, RET_DK), F32)],
        input_output_aliases={11: 0},
        compiler_params=_cparams(3),
        name="ret_prompt",
    )(proj, proj, proj, proj, cos, sin, intra, qdec, kdec, cdec, gn.reshape(1, MIX_HALF), mix)


def _columns(x, n):
    t = x.shape[0]
    parts = [jnp.concatenate([x[:, i:i + 128]] * (128 // t), axis=0).T for i in range(0, n, 128)]
    return parts[0] if len(parts) == 1 else jnp.concatenate(parts, axis=0)


def _ret_sample_kernel(q_ref, k_ref, v_ref, g_ref, cos_ref, sin_ref, cdec_ref, gn_ref, s_ref, mix_ref,
                       o_ref, so_ref):
    del mix_ref
    cos = cos_ref[...]
    sin = sin_ref[...]
    q = _rotary(q_ref[...], cos, sin)
    k = _rotary(k_ref[...], cos, sin) * (RET_DK ** -0.5)
    v = v_ref[...]
    kt = _columns(k, RET_DK)
    gamma = cdec_ref[0, 0:1, :]
    qb = q.astype(BF16)
    tok = lax.broadcasted_iota(jnp.int32, q.shape, 0)
    o = jnp.zeros(q.shape, F32)
    for t in range(SAMPLE_TOK):
        s_new = s_ref[t, 0] * gamma + kt[:, t:t + 1] * v[t:t + 1, :]
        so_ref[t, 0] = s_new
        o = jnp.where(tok == t, jnp.dot(qb, s_new.astype(BF16), preferred_element_type=F32), o)
    o_ref[...] = _group_norm_gate(o, gn_ref[...], g_ref[...]).astype(o_ref.dtype)


def _ret_sample(proj, mix, state, cos, sin, gn, row0):
    nb = state.shape[0]
    t = SAMPLE_TOK
    r0 = row0 // t
    _, _, _, cdec = _ret_decay_tables(1)
    col = lambda off: pl.BlockSpec((t, RET_DK), lambda b, h: (r0 + b, off + h))
    st = pl.BlockSpec((t, 1, RET_DK, RET_DK), lambda b, h: (b, h, 0, 0))
    return pl.pallas_call(
        _ret_sample_kernel,
        grid=(nb // t, RET_HEADS),
        in_specs=[col(0), col(RET_HEADS), col(2 * RET_HEADS), col(3 * RET_HEADS),
                  pl.BlockSpec((1, RET_HALF), lambda b, h: (0, 0)),
                  pl.BlockSpec((1, RET_HALF), lambda b, h: (0, 0)),
                  pl.BlockSpec((1, 8, RET_DK), lambda b, h: (h, 0, 0)),
                  pl.BlockSpec((1, RET_DK), lambda b, h: (0, h)),
                  st,
                  pl.BlockSpec(memory_space=pl.ANY)],
        out_specs=[pl.BlockSpec((t, RET_DK), lambda b, h: (r0 + b, h)), st],
        out_shape=[jax.ShapeDtypeStruct(mix.shape, mix.dtype),
                   jax.ShapeDtypeStruct(state.shape, F32)],
        input_output_aliases={9: 0},
        compiler_params=_cparams(2),
        name="ret_sample",
    )(proj, proj, proj, proj, cos, sin, cdec, gn.reshape(1, MIX_HALF), state, mix)


def _hg_lower_bound(lb_ref, layer):
    x = lb_ref[...]
    e = jnp.exp(x - jnp.max(x, axis=0, keepdims=True))
    return jnp.sum(e[:layer + 1], axis=0, keepdims=True) / jnp.sum(e, axis=0, keepdims=True)


def _hg_gates(gq, gf, lb):
    f = lb + (1.0 - lb) * jax.nn.sigmoid(gf)
    return _silu(gq), 1.0 - f, f


def _hg_out(o, gn, gg):
    return o * lax.rsqrt(jnp.mean(o * o, axis=-1, keepdims=True) + EPS) * gn * _silu(gg)


def _split3(x):
    hi = x.astype(BF16)
    r = x - hi.astype(F32)
    mid = r.astype(BF16)
    lo = (r - mid.astype(F32)).astype(BF16)
    return hi, mid, lo


def _hg_block(qq, kk, lf, v, st, tri):
    c = HG_CHUNK
    nch = qq.shape[0] // c
    hi, mid, lo = _split3(lf)
    b3 = jnp.dot(tri, jnp.concatenate([hi, mid, lo], axis=1), preferred_element_type=F32)
    b = b3[:, :HG_DK] + b3[:, HG_DK:2 * HG_DK] + b3[:, 2 * HG_DK:]
    split = lambda t: t.reshape(nch, c, t.shape[-1])
    q3, k3, bc = split(qq), split(kk), split(b)
    vb = split(v.astype(BF16))
    refs = [jnp.zeros_like(bc[:, 0:1])] + [bc[:, i0 - 1:i0] for i0 in range(HG_SUB, c, HG_SUB)]
    span = jnp.concatenate([jnp.broadcast_to(r, (nch, HG_SUB, HG_DK)) for r in refs], axis=1) - bc
    q_t = (q3 * jnp.exp(-span)).astype(BF16)
    row = lax.broadcasted_iota(jnp.int32, (1, c, HG_DK), 1)
    att_rows = [jnp.zeros((nch, HG_SUB, c), F32)]
    for i0 in range(HG_SUB, c, HG_SUB):
        r = refs[i0 // HG_SUB]
        k_t = jnp.where(row < i0, k3 * jnp.exp(jnp.minimum(r - bc, 0.0)), 0.0).astype(BF16)
        att_rows.append(jnp.einsum('cid,cjd->cij', q_t[:, i0:i0 + HG_SUB], k_t, preferred_element_type=F32))
    att_off = jnp.concatenate(att_rows, axis=1)

    def diag_factored():
        ri = lax.broadcasted_iota(jnp.int32, (1, c, c), 1)
        cj = lax.broadcasted_iota(jnp.int32, (1, c, c), 2)
        shift = HG_SUB.bit_length() - 1
        same = (lax.shift_right_logical(ri, shift) == lax.shift_right_logical(cj, shift)) & (ri >= cj)
        k_d = (k3 * jnp.exp(span)).astype(BF16)
        return jnp.where(same, jnp.einsum('cid,cjd->cij', q_t, k_d, preferred_element_type=F32), 0.0)

    def diag_pairwise():
        sub_row = lax.broadcasted_iota(jnp.int32, (1, HG_SUB, c), 1)
        sub_col = lax.broadcasted_iota(jnp.int32, (1, HG_SUB, c), 2)
        rows = []
        for i0 in range(0, c, HG_SUB):
            q_i = q3[:, i0:i0 + HG_SUB]
            b_i = bc[:, i0:i0 + HG_SUB]
            att_i = jnp.zeros((nch, HG_SUB, c), F32)
            for j in range(HG_SUB):
                jj = i0 + j
                p = q_i * (k3[:, jj:jj + 1] * jnp.exp(jnp.minimum(b_i - bc[:, jj:jj + 1], 0.0)))
                s_j = jnp.sum(p, axis=2, keepdims=True)
                att_i = jnp.where((sub_col == jj) & (sub_row >= j), s_j, att_i)
            rows.append(att_i)
        return jnp.concatenate(rows, axis=1)

    att_diag = lax.cond(jnp.max(span) < HG_SAFE_SPAN, diag_factored, diag_pairwise)
    att = (att_off + att_diag).astype(BF16)
    o_intra = jnp.einsum('cij,cjv->civ', att, vb, preferred_element_type=F32)
    b_last = bc[:, c - 1:c]
    qe = split((qq * jnp.exp(b)).astype(BF16))
    khat = (k3 * jnp.exp(b_last - bc)).astype(BF16)
    dec = jnp.exp(b_last)
    upd = jnp.einsum('cjv,cjd->cvd', vb, khat, preferred_element_type=F32)
    starts = []
    for ci in range(nch):
        starts.append(st.astype(BF16))
        st = st * dec[ci] + upd[ci]
    o_cross = jnp.einsum('cid,cvd->civ', qe, jnp.stack(starts), preferred_element_type=F32)
    return (o_intra + o_cross).reshape(nch * c, HG_DK), st


def _hg_prompt_kernel(layer, gq_ref, gf_ref, gi_ref, gg_ref, lb_ref, gn_ref, tri_ref, mix_ref,
                      o_ref, s_ref, st_ref):
    del mix_ref

    @pl.when(pl.program_id(2) == 0)
    def _():
        st_ref[...] = jnp.zeros_like(st_ref)

    lb_all = _hg_lower_bound(lb_ref, layer)
    for k in range(HG_NH):
        cols = slice(k * HG_DK, (k + 1) * HG_DK)
        qq, kk, f = _hg_gates(gq_ref[:, cols], gf_ref[:, cols], lb_all[:, cols])
        o, st = _hg_block(qq, kk, jnp.log(f), gi_ref[:, cols], st_ref[k], tri_ref[...])
        o_ref[:, cols] = _hg_out(o, gn_ref[:, cols], gg_ref[:, cols]).astype(o_ref.dtype)
        st_ref[k] = st

        @pl.when(pl.program_id(2) == pl.num_programs(2) - 1)
        def _(k=k, st=st):
            s_ref[0, k] = st.T


def _hg_prompt(proj, mix, lb_raw, gn, layer, batch, seq):
    nt = seq // HG_ROWS
    c0 = MIX_HALF * 4 // HG_DK
    nh = HG_NH
    w = nh * HG_DK
    c0 = c0 // nh
    col = lambda off: pl.BlockSpec((HG_ROWS, w), lambda b, h, i: (b * nt + i, c0 + off // nh + h))
    tri = jnp.asarray(np.kron(np.eye(HG_ROWS // HG_CHUNK), np.tril(np.ones((HG_CHUNK, HG_CHUNK)))), BF16)
    return pl.pallas_call(
        functools.partial(_hg_prompt_kernel, layer),
        grid=(batch, HG_HEADS // nh, nt),
        in_specs=[col(0), col(HG_HEADS), col(2 * HG_HEADS), col(3 * HG_HEADS),
                  pl.BlockSpec((lb_raw.shape[0], w), lambda b, h, i: (0, h)),
                  pl.BlockSpec((1, w), lambda b, h, i: (0, h)),
                  pl.BlockSpec((HG_ROWS, HG_ROWS), lambda b, h, i: (0, 0)),
                  pl.BlockSpec(memory_space=pl.ANY)],
        out_specs=[pl.BlockSpec((HG_ROWS, w), lambda b, h, i: (b * nt + i, HG_HEADS // nh + h)),
                   pl.BlockSpec((1, nh, HG_DK, HG_DK), lambda b, h, i: (b, h, 0, 0))],
        out_shape=[jax.ShapeDtypeStruct(mix.shape, mix.dtype),
                   jax.ShapeDtypeStruct((batch, HG_HEADS, HG_DK, HG_DK), F32)],
        scratch_shapes=[pltpu.VMEM((nh, HG_DK, HG_DK), F32)],
        input_output_aliases={7: 0},
        compiler_params=_cparams(3),
        name="hgrn_prompt",
    )(proj, proj, proj, proj, lb_raw, gn.reshape(1, MIX_HALF), tri, mix)


def _hg_sample_kernel(layer, gq_ref, gf_ref, gi_ref, gg_ref, lb_ref, gn_ref, s_ref, mix_ref,
                      o_ref, so_ref):
    del mix_ref
    lb = _hg_lower_bound(lb_ref, layer)
    qq, kk, f = _hg_gates(gq_ref[...], gf_ref[...], lb)
    v = gi_ref[...]
    kt = _columns(kk, HG_DK)
    ft = _columns(f, HG_DK)
    qb = qq.astype(BF16)
    tok = lax.broadcasted_iota(jnp.int32, qq.shape, 0)
    o = jnp.zeros(qq.shape, F32)
    for t in range(SAMPLE_TOK):
        s_new = s_ref[t, 0] * ft[:, t:t + 1] + kt[:, t:t + 1] * v[t:t + 1, :]
        so_ref[t, 0] = s_new
        o = jnp.where(tok == t, jnp.dot(qb, s_new.astype(BF16), preferred_element_type=F32), o)
    o_ref[...] = _hg_out(o, gn_ref[...], gg_ref[...]).astype(o_ref.dtype)


def _hg_sample(proj, mix, state, lb_raw, gn, layer, row0):
    nb = state.shape[0]
    t = SAMPLE_TOK
    r0 = row0 // t
    c0 = MIX_HALF * 4 // HG_DK
    col = lambda off: pl.BlockSpec((t, HG_DK), lambda b, h: (r0 + b, c0 + off + h))
    st = pl.BlockSpec((t, 1, HG_DK, HG_DK), lambda b, h: (b, h, 0, 0))
    return pl.pallas_call(
        functools.partial(_hg_sample_kernel, layer),
        grid=(nb // t, HG_HEADS),
        in_specs=[col(0), col(HG_HEADS), col(2 * HG_HEADS), col(3 * HG_HEADS),
                  pl.BlockSpec((lb_raw.shape[0], HG_DK), lambda b, h: (0, h)),
                  pl.BlockSpec((1, HG_DK), lambda b, h: (0, h)),
                  st,
                  pl.BlockSpec(memory_space=pl.ANY)],
        out_specs=[pl.BlockSpec((t, HG_DK), lambda b, h: (r0 + b, HG_HEADS + h)), st],
        out_shape=[jax.ShapeDtypeStruct(mix.shape, mix.dtype),
                   jax.ShapeDtypeStruct(state.shape, F32)],
        input_output_aliases={7: 0},
        compiler_params=_cparams(2),
        name="hgrn_sample",
    )(proj, proj, proj, proj, lb_raw, gn.reshape(1, MIX_HALF), state, mix)


S5_SEG = S5_SUB // 8
S5_POW_ROWS = 16
assert S5_SEG & (S5_SEG - 1) == 0


def _cmul(ar, ai, br, bi):
    return ar * br - ai * bi, ar * bi + ai * br


def _s5_prep_kernel(lr_ref, li_ref, ldt_ref, brt_ref, bit_ref, pr_ref, pi_ref, bbr_ref, bbi_ref):
    lr = lr_ref[...]
    li = li_ref[...]
    dt = jnp.exp(ldt_ref[...])
    mag = jnp.exp(lr * dt)
    ar = mag * jnp.cos(li * dt)
    ai = mag * jnp.sin(li * dt)
    den = lr * lr + li * li
    cr = ((ar - 1.0) * lr + ai * li) / den
    ci = (ai * lr - (ar - 1.0) * li) / den
    brt = brt_ref[...]
    bit = bit_ref[...]
    bbr_ref[...] = cr * brt - ci * bit
    bbi_ref[...] = cr * bit + ci * brt
    n = pr_ref.shape[1]
    pr_ref[0:8, :] = jnp.broadcast_to(ar, (8, n))
    pi_ref[0:8, :] = jnp.broadcast_to(ai, (8, n))
    seg_pow = (ar, ai)
    for _ in range(S5_SEG.bit_length() - 1):
        seg_pow = _cmul(*seg_pow, *seg_pow)
    p = seg_pow
    for m in range(8):
        pr_ref[8 + m:9 + m, :] = p[0]
        pi_ref[8 + m:9 + m, :] = p[1]
        p = _cmul(*p, *seg_pow)


def _s5_prep(lam_re, lam_im, log_dt, b_re, b_im):
    n = S5_GROUPS * S5_STATE
    flat = lambda t: t.reshape(1, n)
    b_t = lambda t: jnp.transpose(t, (2, 0, 1)).reshape(S5_GROUP, n)
    ldt = jnp.repeat(log_dt, S5_STATE).reshape(1, n)
    return pl.pallas_call(
        _s5_prep_kernel,
        out_shape=[jax.ShapeDtypeStruct((S5_POW_ROWS, n), F32)] * 2
        + [jax.ShapeDtypeStruct((S5_GROUP, n), F32)] * 2,
        compiler_params=pltpu.CompilerParams(vmem_limit_bytes=VMEM_LIMIT_BYTES),
        name="s5_prep",
    )(flat(lam_re), flat(lam_im), ldt, b_t(b_re), b_t(b_im))


def _s5_block_diag(bbr, bbi, c_re, c_im):
    eye = jnp.eye(S5_BLK_GROUPS, dtype=F32)
    def b_blk(t):
        t = t.reshape(S5_GROUP, S5_NBLK, S5_BLK_GROUPS, S5_STATE)
        t = jnp.einsum('cbgp,hg->bhcgp', t, eye)
        return t.reshape(S5_NBLK, S5_BLK_CH, S5_BLK_ST).astype(BF16)
    def c_blk(t):
        t = t.reshape(S5_NBLK, S5_BLK_GROUPS, S5_GROUP, S5_STATE)
        t = jnp.einsum('bgcp,hg->bhpgc', t, eye)
        return t.reshape(S5_NBLK, S5_BLK_ST, S5_BLK_CH).astype(BF16)
    return (jnp.concatenate([b_blk(bbr), b_blk(bbi)], axis=2),
            jnp.concatenate([c_blk(c_re), c_blk(-c_im)], axis=1))


def _s5_readout(u, h_bf16, c_ref, d_ref):
    y = jnp.dot(h_bf16, c_ref[0], preferred_element_type=F32) + d_ref[...] * u
    return jax.nn.gelu(y)


def _s5_row_perm(rows):
    seg = rows // 8
    p = np.zeros((rows, rows), np.float32)
    t, s = np.meshgrid(np.arange(seg), np.arange(8), indexing="ij")
    p[(t * 8 + s).ravel(), (s * seg + t).ravel()] = 1.0
    return jnp.asarray(p, BF16), jnp.asarray(p.T, BF16)


def _s5_prompt_kernel(u_ref, p_ref, pt_ref, b_ref, c_ref, pr_ref, pi_ref, d_ref, z_in_ref,
                      z_ref, hr_out, hi_out, *scratch):
    del z_in_ref
    x_refs = scratch[:S5_NSUB]
    hb_refs = scratch[S5_NSUB:2 * S5_NSUB]
    us_ref, cr_carry, ci_carry = scratch[2 * S5_NSUB:]
    tb = pl.program_id(2)
    n = S5_BLK_ST
    ch = S5_BLK_CH

    @pl.when(tb == 0)
    def _():
        cr_carry[...] = jnp.zeros_like(cr_carry)
        ci_carry[...] = jnp.zeros_like(ci_carry)

    for q in range(S5_NSUB):
        rows = slice(q * S5_SUB, (q + 1) * S5_SUB)
        up = jnp.dot(p_ref[...], jnp.concatenate(_split3(u_ref[rows, :]), axis=1),
                     preferred_element_type=F32)
        us_ref[rows, :] = (up[:, :ch] + up[:, ch:2 * ch]) + up[:, 2 * ch:]
        x_refs[q][...] = jnp.dot(up[:, :ch].astype(BF16), b_ref[0], preferred_element_type=F32)

    a_r = pr_ref[0:8, :]
    a_i = pi_ref[0:8, :]
    seg0 = 8
    row = lax.broadcasted_iota(jnp.int32, (8, n), 0)
    h0r = cr_carry[...]
    h0i = ci_carry[...]
    for q in range(S5_NSUB):
        x_ref = x_refs[q]
        def advance(t, h, x_ref=x_ref):
            rows = pl.ds(pl.multiple_of(t * 8, 8), 8)
            dr, di = _cmul(a_r, a_i, *h)
            return dr + x_ref[rows, 0:n], di + x_ref[rows, n:2 * n]

        zero = jnp.zeros((8, n), F32)
        er, ei = lax.fori_loop(0, S5_SEG, advance, (zero, zero), unroll=4)
        for s in range(3):
            m = 1 << s
            g_r = pr_ref[seg0 + m - 1:seg0 + m, :]
            g_i = pi_ref[seg0 + m - 1:seg0 + m, :]
            sr = jnp.where(row >= m, pltpu.roll(er, m, 0), 0.0)
            si = jnp.where(row >= m, pltpu.roll(ei, m, 0), 0.0)
            dr, di = _cmul(g_r, g_i, sr, si)
            er, ei = er + dr, ei + di
        dr, di = _cmul(pr_ref[seg0:seg0 + 8, :], pi_ref[seg0:seg0 + 8, :], h0r, h0i)
        er, ei = er + dr, ei + di
        cm_r = jnp.where(row >= 1, pltpu.roll(er, 1, 0), h0r)
        cm_i = jnp.where(row >= 1, pltpu.roll(ei, 1, 0), h0i)
        h0r = er[7:8, :]
        h0i = ei[7:8, :]
        def emit(t2, h, hb_ref=hb_refs[q]):
            h1 = advance(2 * t2, h)
            h2 = advance(2 * t2 + 1, h1)
            rows = pl.ds(pl.multiple_of(t2 * 16, 16), 16)
            hb_ref[rows, 0:n] = jnp.concatenate([h1[0], h2[0]], axis=0).astype(BF16)
            hb_ref[rows, n:2 * n] = jnp.concatenate([h1[1], h2[1]], axis=0).astype(BF16)
            return h2

        lax.fori_loop(0, S5_SEG // 2, emit, (cm_r, cm_i), unroll=2)
        rows = slice(q * S5_SUB, (q + 1) * S5_SUB)
        zp = _s5_readout(us_ref[rows, :], hb_refs[q][...], c_ref, d_ref).astype(BF16)
        z_ref[rows, :] = jnp.dot(pt_ref[...], zp, preferred_element_type=F32).astype(z_ref.dtype)
    cr_carry[...] = h0r
    ci_carry[...] = h0i

    @pl.when(tb == pl.num_programs(2) - 1)
    def _():
        hr_out[0, 0] = h0r
        hi_out[0, 0] = h0i


def _s5_prompt(u, z, mats, pows, d, batch, seq):
    b_m, c_m = mats
    p_r, p_i = pows
    perm, perm_t = _s5_row_perm(S5_SUB)
    nt = seq // S5_ROWS
    blk3 = lambda r, c: pl.BlockSpec((1, r, c), lambda b, j, i: (j, 0, 0))
    whole = pl.BlockSpec((S5_SUB, S5_SUB), lambda b, j, i: (0, 0))
    st_spec = pl.BlockSpec((1, 1, 1, S5_BLK_ST), lambda b, j, i: (b, j, 0, 0))
    st_shape = jax.ShapeDtypeStruct((batch, S5_NBLK, 1, S5_BLK_ST), F32)
    return pl.pallas_call(
        _s5_prompt_kernel,
        grid=(batch, S5_NBLK, nt),
        in_specs=[pl.BlockSpec((S5_ROWS, S5_BLK_CH), lambda b, j, i: (b * nt + i, j)),
                  whole, whole,
                  blk3(S5_BLK_CH, 2 * S5_BLK_ST), blk3(2 * S5_BLK_ST, S5_BLK_CH),
                  pl.BlockSpec((S5_POW_ROWS, S5_BLK_ST), lambda b, j, i: (0, j)),
                  pl.BlockSpec((S5_POW_ROWS, S5_BLK_ST), lambda b, j, i: (0, j)),
                  pl.BlockSpec((1, S5_BLK_CH), lambda b, j, i: (0, j)),
                  pl.BlockSpec(memory_space=pl.ANY)],
        out_specs=[pl.BlockSpec((S5_ROWS, S5_BLK_CH), lambda b, j, i: (b * nt + i, j)), st_spec, st_spec],
        out_shape=[jax.ShapeDtypeStruct(z.shape, z.dtype), st_shape, st_shape],
        input_output_aliases={8: 0},
        scratch_shapes=[pltpu.VMEM((S5_SUB, 2 * S5_BLK_ST), F32) for _ in range(S5_NSUB)]
        + [pltpu.VMEM((S5_SUB, 2 * S5_BLK_ST), BF16) for _ in range(S5_NSUB)]
        + [pltpu.VMEM((S5_ROWS, S5_BLK_CH), F32),
           pltpu.VMEM((1, S5_BLK_ST), F32), pltpu.VMEM((1, S5_BLK_ST), F32)],
        compiler_params=_cparams(3),
        name="s5_prompt",
    )(u, perm, perm_t, b_m, c_m, p_r, p_i, d.reshape(1, D_MODEL), z)


S5_T = 256
S5_HALF_ST = S5_BLK_ST // 2


def _s5_tb_mats(bbr, bbi, c_re, c_im, a_r, a_i, batch):
    hg = S5_BLK_GROUPS // 2
    eye = jnp.eye(hg, dtype=F32)
    def b_blk(t):
        t = t.reshape(S5_GROUP, S5_NBLK, 2, hg, S5_STATE)
        return jnp.einsum('cbhgp,kg->bhgckp', t, eye)
    def c_blk(t):
        t = t.reshape(S5_NBLK, 2, hg, S5_GROUP, S5_STATE)
        return jnp.einsum('bhgcp,kg->bkphgc', t, eye)
    b_m = jnp.stack([b_blk(bbr), b_blk(bbi)], axis=4)
    c_m = jnp.stack([c_blk(c_re), c_blk(-c_im)], axis=1)
    tile_a = lambda a: jnp.broadcast_to(a.reshape(S5_NBLK, 2, 1, S5_HALF_ST),
                                        (S5_NBLK, 2, batch, S5_HALF_ST)).reshape(S5_NBLK, 2 * batch, S5_HALF_ST)
    return (b_m.reshape(S5_NBLK, S5_BLK_CH, S5_BLK_ST).astype(BF16),
            c_m.reshape(S5_NBLK, S5_BLK_ST, S5_BLK_CH).astype(BF16), tile_a(a_r), tile_a(a_i))


S5_NB = 2


def _s5_tb_kernel(u_ref, b_ref, c_ref, ar_ref, ai_ref, d_ref, z_in_ref, z_ref, hr_out, hi_out,
                  x_ref, hb_ref, hr_ref, hi_ref):
    del z_in_ref
    i = pl.program_id(1)
    n = S5_HALF_ST
    ch = S5_BLK_CH
    t_steps = u_ref.shape[0] // 4
    half_rows = 4 * t_steps

    @pl.when(i == 0)
    def _():
        hr_ref[...] = jnp.zeros_like(hr_ref)
        hi_ref[...] = jnp.zeros_like(hi_ref)

    first4 = lax.broadcasted_iota(jnp.int32, (1, 8, ch), 1) < 4
    half0 = lax.broadcasted_iota(jnp.int32, (1, 8, ch), 2) < ch // 2
    for k in range(S5_NB):
        u3 = u_ref[:, k * ch:(k + 1) * ch].reshape(t_steps // 2, 8, ch)
        r3 = pltpu.roll(u3, 4, 1)
        even = jnp.where(first4, jnp.where(half0, u3, 0.0), jnp.where(half0, 0.0, r3))
        odd = jnp.where(first4, jnp.where(half0, r3, 0.0), jnp.where(half0, 0.0, u3))
        lhs = jnp.stack([even, odd], axis=1).reshape(8 * t_steps, ch).astype(BF16)
        for r0 in (0, half_rows):
            x_ref[k, r0:r0 + half_rows, :] = jnp.dot(lhs[r0:r0 + half_rows], b_ref[k],
                                                     preferred_element_type=F32)
    a = [(ar_ref[k], ai_ref[k]) for k in range(S5_NB)]

    def advance(t, hs):
        rows = pl.ds(pl.multiple_of(t * 8, 8), 8)
        out = []
        for k in range(S5_NB):
            dr, di = _cmul(*a[k], *hs[k])
            out.append((dr + x_ref[k, rows, 0:n], di + x_ref[k, rows, n:2 * n]))
        return tuple(out)

    def emit(t2, hs):
        h1 = advance(2 * t2, hs)
        h2 = advance(2 * t2 + 1, h1)
        rows = pl.ds(pl.multiple_of(t2 * 16, 16), 16)
        for k in range(S5_NB):
            hb_ref[k, rows, 0:n] = jnp.concatenate([h1[k][0], h2[k][0]], axis=0).astype(BF16)
            hb_ref[k, rows, n:2 * n] = jnp.concatenate([h1[k][1], h2[k][1]], axis=0).astype(BF16)
        return h2

    hs = lax.fori_loop(0, t_steps // 2, emit, tuple((hr_ref[k], hi_ref[k]) for k in range(S5_NB)), unroll=2)
    for k in range(S5_NB):
        hr_ref[k] = hs[k][0]
        hi_ref[k] = hs[k][1]

    pick = lambda yv: jnp.where(half0, yv, pltpu.roll(yv, 4, 1))
    for k in range(S5_NB):
        y2 = jnp.concatenate([jnp.dot(hb_ref[k, r0:r0 + half_rows, :], c_ref[k], preferred_element_type=F32)
                              for r0 in (0, half_rows)], axis=0)
        y4 = y2.reshape(t_steps // 2, 2, 8, ch)
        y = jnp.where(first4, pick(y4[:, 0]), pltpu.roll(pick(y4[:, 1]), 4, 1)).reshape(4 * t_steps, ch)
        cols = slice(k * ch, (k + 1) * ch)
        z_ref[:, cols] = jax.nn.gelu(y + d_ref[:, cols] * u_ref[:, cols]).astype(z_ref.dtype)

    @pl.when(i == pl.num_programs(1) - 1)
    def _():
        for k in range(S5_NB):
            for h in range(2):
                cols = slice((2 * k + h) * n, (2 * k + h + 1) * n)
                hr_out[:, cols] = hs[k][0][4 * h:4 * h + 4, :]
                hi_out[:, cols] = hs[k][1][4 * h:4 * h + 4, :]


def _s5_prompt_tb(u, z, mats, d, batch, seq):
    assert batch == 4
    b_m, c_m, a_r, a_i = mats
    rows = batch * S5_T
    nt = seq // S5_T
    nb = S5_NB
    blk3 = lambda r, c: pl.BlockSpec((nb, r, c), lambda j, i: (j, 0, 0))
    st_spec = pl.BlockSpec((batch, nb * S5_BLK_ST), lambda j, i: (0, j))
    st_shape = jax.ShapeDtypeStruct((batch, S5_GROUPS * S5_STATE), F32)
    return pl.pallas_call(
        _s5_tb_kernel,
        grid=(S5_NBLK // nb, nt),
        in_specs=[pl.BlockSpec((rows, nb * S5_BLK_CH), lambda j, i: (i, j)),
                  blk3(S5_BLK_CH, S5_BLK_ST), blk3(S5_BLK_ST, S5_BLK_CH),
                  blk3(2 * batch, S5_HALF_ST), blk3(2 * batch, S5_HALF_ST),
                  pl.BlockSpec((1, nb * S5_BLK_CH), lambda j, i: (0, j)),
                  pl.BlockSpec(memory_space=pl.ANY)],
        out_specs=[pl.BlockSpec((rows, nb * S5_BLK_CH), lambda j, i: (i, j)), st_spec, st_spec],
        out_shape=[jax.ShapeDtypeStruct(z.shape, z.dtype), st_shape, st_shape],
        scratch_shapes=[pltpu.VMEM((nb, 2 * rows, S5_BLK_ST), F32), pltpu.VMEM((nb, 2 * rows, S5_BLK_ST), BF16),
                        pltpu.VMEM((nb, 2 * batch, S5_HALF_ST), F32), pltpu.VMEM((nb, 2 * batch, S5_HALF_ST), F32)],
        input_output_aliases={6: 0},
        compiler_params=_cparams(2),
        name="s5_prompt",
    )(u, b_m, c_m, a_r, a_i, d.reshape(1, D_MODEL), z)


def _s5_sample_kernel(u_ref, b_ref, c_ref, ar_ref, ai_ref, d_ref, h0r_ref, h0i_ref,
                      z_in_ref, z_ref, hr_out, hi_out):
    del z_in_ref
    n = S5_HALF_ST
    u = u_ref[...]
    nb = u.shape[0]
    half0 = lax.broadcasted_iota(jnp.int32, u.shape, 1) < S5_BLK_CH // 2
    lhs = jnp.concatenate([jnp.where(half0, u, 0.0), jnp.where(half0, 0.0, u)], axis=0).astype(BF16)
    x = jnp.dot(lhs, b_ref[0], preferred_element_type=F32)
    halves = []
    for h in range(2):
        rows = slice(h * nb, (h + 1) * nb)
        cols = slice(h * n, (h + 1) * n)
        row_a = slice(h * (ar_ref.shape[1] // 2), h * (ar_ref.shape[1] // 2) + 1)
        dr, di = _cmul(ar_ref[0, row_a, :], ai_ref[0, row_a, :], h0r_ref[:, cols], h0i_ref[:, cols])
        hr = x[rows, 0:n] + dr
        hi = x[rows, n:2 * n] + di
        hr_out[:, cols] = hr
        hi_out[:, cols] = hi
        halves.append(jnp.concatenate([hr, hi], axis=1))
    y2 = jnp.dot(jnp.concatenate(halves, axis=0).astype(BF16), c_ref[0], preferred_element_type=F32)
    y = jnp.where(half0, y2[0:nb], y2[nb:2 * nb])
    z_ref[...] = jax.nn.gelu(y + d_ref[...] * u).astype(z_ref.dtype)


def _s5_sample(u, z, mats, d, h0r, h0i, row0):
    b_m, c_m, a_r, a_i = mats
    nb = h0r.shape[0]
    r0 = row0 // nb
    blk3 = lambda r, c: pl.BlockSpec((1, r, c), lambda j: (j, 0, 0))
    st_spec = pl.BlockSpec((nb, S5_BLK_ST), lambda j: (0, j))
    st_shape = jax.ShapeDtypeStruct(h0r.shape, F32)
    return pl.pallas_call(
        _s5_sample_kernel,
        grid=(S5_NBLK,),
        in_specs=[pl.BlockSpec((nb, S5_BLK_CH), lambda j: (r0, j)),
                  blk3(S5_BLK_CH, S5_BLK_ST), blk3(S5_BLK_ST, S5_BLK_CH),
                  blk3(a_r.shape[1], S5_HALF_ST), blk3(a_r.shape[1], S5_HALF_ST),
                  pl.BlockSpec((1, S5_BLK_CH), lambda j: (0, j)),
                  st_spec, st_spec,
                  pl.BlockSpec(memory_space=pl.ANY)],
        out_specs=[pl.BlockSpec((nb, S5_BLK_CH), lambda j: (r0, j)), st_spec, st_spec],
        out_shape=[jax.ShapeDtypeStruct(z.shape, z.dtype), st_shape, st_shape],
        input_output_aliases={8: 0},
        compiler_params=_cparams(1),
        name="s5_sample",
    )(u, b_m, c_m, a_r, a_i, d.reshape(1, D_MODEL), h0r, h0i, z)


TB_T = 64
LANES = 128


def _interleave_norm_kernel(n_steps, n_sample, batch, *refs):
    x_refs = refs[:batch]
    xs_ref, g_ref, xt_ref, ut_ref, slab_ref = refs[batch:]
    i = pl.program_id(0)
    d = xt_ref.shape[1]

    @pl.when(i < n_steps)
    def _():
        for dst_ref, norm in ((xt_ref, False), (ut_ref, True)):
            for b, x_ref in enumerate(x_refs):
                v = x_ref[...]
                if norm:
                    v = _rms(v, g_ref[...])
                for s in range(d // LANES):
                    slab_ref[s, pl.ds(b, TB_T, stride=batch), :] = v[:, s * LANES:(s + 1) * LANES]
            for s in range(d // LANES):
                dst_ref[:, s * LANES:(s + 1) * LANES] = slab_ref[s]

    @pl.when(i >= n_steps)
    def _():
        v = xs_ref[...]
        xt_ref[0:n_sample, :] = v
        ut_ref[0:n_sample, :] = _rms(v, g_ref[...])


def _interleave_norm(x, g, batch, seq):
    m, d = x.shape
    n_steps = seq // TB_T
    ms = m - batch * seq
    rows = batch * TB_T
    per_batch = lambda b: pl.BlockSpec((TB_T, d), lambda i: (b * n_steps + jnp.minimum(i, n_steps - 1), 0))
    out = pl.BlockSpec((rows, d), lambda i: (i, 0))
    return pl.pallas_call(
        functools.partial(_interleave_norm_kernel, n_steps, ms, batch),
        grid=(n_steps + 1,),
        in_specs=[per_batch(b) for b in range(batch)]
        + [pl.BlockSpec((ms, d), lambda i: (batch * seq // ms, 0)), pl.BlockSpec((1, d), lambda i: (0, 0))],
        out_specs=[out, out],
        out_shape=[jax.ShapeDtypeStruct((m, d), F32)] * 2,
        scratch_shapes=[pltpu.VMEM((d // LANES, rows, LANES), F32)],
        compiler_params=_cparams(1),
        name="interleave_norm",
    )(*([x] * (batch + 1)), g.reshape(1, d))


def _deinterleave_norm_kernel(n_steps, n_sample, batch, x_ref, g_ref, y_ref, ys_ref, slab_ref):
    i = pl.program_id(0)
    d = x_ref.shape[1]

    @pl.when(i < n_steps)
    def _():
        y = _rms(x_ref[...], g_ref[...])
        for s in range(d // LANES):
            slab_ref[s] = y[:, s * LANES:(s + 1) * LANES]
        for b in range(batch):
            for s in range(d // LANES):
                y_ref[b, :, s * LANES:(s + 1) * LANES] = slab_ref[s, pl.ds(b, TB_T, stride=batch), :]

    @pl.when(i >= n_steps)
    def _():
        ys_ref[...] = _rms(x_ref[0:n_sample, :], g_ref[...])


def _deinterleave_norm(x, g, batch, seq):
    m, d = x.shape
    n_steps = seq // TB_T
    ms = m - batch * seq
    rows = batch * TB_T
    return pl.pallas_call(
        functools.partial(_deinterleave_norm_kernel, n_steps, ms, batch),
        grid=(n_steps + 1,),
        in_specs=[pl.BlockSpec((rows, d), lambda i: (i, 0)), pl.BlockSpec((1, d), lambda i: (0, 0))],
        out_specs=[pl.BlockSpec((batch, TB_T, d), lambda i: (0, jnp.minimum(i, n_steps - 1), 0)),
                   pl.BlockSpec((ms, d), lambda i: (0, 0))],
        out_shape=[jax.ShapeDtypeStruct((batch, seq, d), F32), jax.ShapeDtypeStruct((ms, d), F32)],
        scratch_shapes=[pltpu.VMEM((d // LANES, rows, LANES), F32)],
        compiler_params=_cparams(1),
        name="deinterleave_norm",
    )(x, g.reshape(1, d))


def _rope_tables(pos):
    inv = ROPE_BASE ** (-jnp.arange(RET_HALF, dtype=F32) / RET_HALF)
    ang = pos[:, None] * inv[None, :]
    return jnp.cos(ang), jnp.sin(ang)


def _ffn(x, norm_g, wg, wu, wd, layer):
    h = _rmsnorm(x, norm_g[layer], [BF16])[0]
    t = _matmul(h, [wg, wu], _epi_swiglu, BF16, layer=layer, tm=1664, name="ffn_gate_up")
    return _matmul(t, [wd], _epi_id, F32, res=x, layer=layer, name="ffn_down")


def kernel(x_prompt, x_sample, state_ret, state_hgrn, state_s5_re, state_s5_im, attn_norm_g, w_in, ret_gn_g, hg_lb, hg_gn_g, w_out, ssm_norm_g, s5_lam_re, s5_lam_im, s5_log_dt, s5_b_re, s5_b_im, s5_c_re, s5_c_im, s5_d, w_glu_a, w_glu_b, ffn_norm_g, w_ffn_gate, w_ffn_up, w_ffn_down, final_norm_g):
    one = lambda t: t.reshape(t.shape[1:])
    (state_ret, state_hgrn, state_s5_re, state_s5_im, attn_norm_g, w_in, ret_gn_g, hg_gn_g, w_out, ssm_norm_g,
     s5_lam_re, s5_lam_im, s5_log_dt, s5_b_re, s5_b_im, s5_c_re, s5_c_im, s5_d, w_glu_a, w_glu_b) = map(one, (
         state_ret, state_hgrn, state_s5_re, state_s5_im, attn_norm_g, w_in, ret_gn_g, hg_gn_g, w_out, ssm_norm_g,
         s5_lam_re, s5_lam_im, s5_log_dt, s5_b_re, s5_b_im, s5_c_re, s5_c_im, s5_d, w_glu_a, w_glu_b))
    cos_p, sin_p = _rope_tables(jnp.arange(SEQ, dtype=F32))
    cos_s, sin_s = _rope_tables(jnp.full((1,), float(PAST_LEN), F32))

    h, x = _rmsnorm_stack(x_prompt.reshape(N_PROMPT, D_MODEL), x_sample.reshape(DEC_BATCH, D_MODEL), attn_norm_g)
    proj = _matmul(h, [w_in], _epi_id, F32, tm=1664, tn=1024, name="w_in")
    mix = jnp.zeros((N_ROWS, D_MODEL), BF16)
    mix, ret_p = _ret_prompt(proj, mix, cos_p, sin_p, ret_gn_g, BATCH, SEQ)
    mix, hg_p = _hg_prompt(proj, mix, hg_lb, hg_gn_g, 0, BATCH, SEQ)
    mix, ret_s = _ret_sample(proj, mix, state_ret, cos_s, sin_s, ret_gn_g, N_PROMPT)
    mix, hg_s = _hg_sample(proj, mix, state_hgrn, hg_lb, hg_gn_g, 0, N_PROMPT)
    x = _matmul(mix, [w_out], _epi_id, F32, res=x, tm=1664, name="w_out")
    x = _ffn(x, ffn_norm_g, w_ffn_gate, w_ffn_up, w_ffn_down, 0)

    x, u = _interleave_norm(x, ssm_norm_g, BATCH, SEQ)
    p_r, p_i, bbr, bbi = _s5_prep(s5_lam_re, s5_lam_im, s5_log_dt, s5_b_re, s5_b_im)
    mats = _s5_tb_mats(bbr, bbi, s5_c_re, s5_c_im, p_r[0], p_i[0], BATCH)
    z = jnp.zeros((N_ROWS, D_MODEL), BF16)
    z, s5r_p, s5i_p = _s5_prompt_tb(u, z, mats, s5_d, BATCH, SEQ)
    n_st = S5_GROUPS * S5_STATE
    z, s5r_s, s5i_s = _s5_sample(u, z, mats, s5_d, state_s5_re.reshape(DEC_BATCH, n_st),
                                 state_s5_im.reshape(DEC_BATCH, n_st), N_PROMPT)
    x = _matmul(z, [w_glu_a, w_glu_b], _epi_glu, F32, res=x, name="glu")
    x = _ffn(x, ffn_norm_g, w_ffn_gate, w_ffn_up, w_ffn_down, 1)

    y_p, y_s = _deinterleave_norm(x, final_norm_g, BATCH, SEQ)
    st = lambda t, b: t.reshape(1, b, S5_GROUPS, S5_STATE)
    return (y_p, y_s.reshape(DEC_BATCH, 1, D_MODEL),
            ret_p[None], ret_s[None], hg_p[None], hg_s[None],
            st(s5r_p, BATCH), st(s5i_p, BATCH), st(s5r_s, DEC_BATCH), st(s5i_s, DEC_BATCH))
```

```python
import functools
import math

import numpy as np
import jax
import jax.numpy as jnp
from jax import lax
from jax.experimental import pallas as pl
from jax.experimental.pallas import tpu as pltpu

F32 = jnp.float32
BF16 = jnp.bfloat16

D_MODEL = 2048
BATCH = 4
SEQ = 2048
DEC_BATCH = 128
PAST_LEN = 16384
N_PROMPT = BATCH * SEQ
N_ROWS = N_PROMPT + DEC_BATCH
MIX_HALF = D_MODEL // 2
RET_HEADS = 4
RET_DK = MIX_HALF // RET_HEADS
RET_HALF = RET_DK // 2
RET_CHUNK = 128
RET_ROWS = 512
HG_HEADS = 8
HG_DK = MIX_HALF // HG_HEADS
HG_CHUNK = 64
HG_SUB = 16
HG_ROWS = 256
HG_NH = 8
HG_SAFE_SPAN = 40.0
S5_GROUP = 16
S5_GROUPS = D_MODEL // S5_GROUP
S5_STATE = 64
S5_BLK_GROUPS = 16
S5_BLK_CH = S5_BLK_GROUPS * S5_GROUP
S5_BLK_ST = S5_BLK_GROUPS * S5_STATE
S5_NBLK = S5_GROUPS // S5_BLK_GROUPS
S5_SUB = 256
S5_NSUB = 2
S5_ROWS = S5_SUB * S5_NSUB
D_FF = 5632
ROPE_BASE = 10000.0
EPS = 1e-6
SAMPLE_TOK = 16

VMEM_LIMIT_BYTES = 56 * 1024 * 1024


def _cparams(n_axes):
    return pltpu.CompilerParams(dimension_semantics=("arbitrary",) * n_axes,
                                vmem_limit_bytes=VMEM_LIMIT_BYTES)


def _silu(x):
    return x * jax.nn.sigmoid(x)


def _rmsnorm_kernel(x_ref, g_ref, *o_refs):
    x = x_ref[...]
    y = x * lax.rsqrt(jnp.mean(x * x, axis=-1, keepdims=True) + EPS) * g_ref[...]
    for o_ref in o_refs:
        o_ref[...] = y.astype(o_ref.dtype)


def _rmsnorm(x, g, dtypes, tm=320):
    m, d = x.shape
    return pl.pallas_call(
        _rmsnorm_kernel,
        grid=(m // tm,),
        in_specs=[pl.BlockSpec((tm, d), lambda i: (i, 0)),
                  pl.BlockSpec((1, d), lambda i: (0, 0))],
        out_specs=[pl.BlockSpec((tm, d), lambda i: (i, 0)) for _ in dtypes],
        out_shape=[jax.ShapeDtypeStruct((m, d), dt) for dt in dtypes],
        compiler_params=_cparams(1),
        name="rmsnorm",
    )(x, g.reshape(1, d))


ROW_TILE = 512


def _rms(x, g):
    return x * lax.rsqrt(jnp.mean(x * x, axis=-1, keepdims=True) + EPS) * g


def _rmsnorm_stack_kernel(n_prompt_tiles, n_sample, xp_ref, xs_ref, g_ref, h_ref, x_ref):
    i = pl.program_id(0)

    @pl.when(i < n_prompt_tiles)
    def _():
        x = xp_ref[...]
        x_ref[...] = x
        h_ref[...] = _rms(x, g_ref[...]).astype(h_ref.dtype)

    @pl.when(i >= n_prompt_tiles)
    def _():
        x = xs_ref[...]
        x_ref[0:n_sample, :] = x
        h_ref[0:n_sample, :] = _rms(x, g_ref[...]).astype(h_ref.dtype)


def _rmsnorm_stack(x_prompt, x_sample, g):
    (mp, d), ms = x_prompt.shape, x_sample.shape[0]
    tm = ROW_TILE
    npt = mp // tm
    m = mp + ms
    return pl.pallas_call(
        functools.partial(_rmsnorm_stack_kernel, npt, ms),
        grid=(npt + 1,),
        in_specs=[pl.BlockSpec((tm, d), lambda i: (jnp.minimum(i, npt - 1), 0)),
                  pl.BlockSpec((ms, d), lambda i: (0, 0)),
                  pl.BlockSpec((1, d), lambda i: (0, 0))],
        out_specs=[pl.BlockSpec((tm, d), lambda i: (i, 0))] * 2,
        out_shape=[jax.ShapeDtypeStruct((m, d), BF16), jax.ShapeDtypeStruct((m, d), F32)],
        compiler_params=_cparams(1),
        name="rmsnorm_stack",
    )(x_prompt, x_sample, g.reshape(1, d))


def _rmsnorm_split_kernel(n_prompt_tiles, n_sample, x_ref, g_ref, yp_ref, ys_ref):
    i = pl.program_id(0)

    @pl.when(i < n_prompt_tiles)
    def _():
        yp_ref[...] = _rms(x_ref[...], g_ref[...])

    @pl.when(i >= n_prompt_tiles)
    def _():
        ys_ref[...] = _rms(x_ref[0:n_sample, :], g_ref[...])


def _rmsnorm_split(x, g, n_prompt):
    m, d = x.shape
    tm = ROW_TILE
    npt = n_prompt // tm
    ms = m - n_prompt
    return pl.pallas_call(
        functools.partial(_rmsnorm_split_kernel, npt, ms),
        grid=(npt + 1,),
        in_specs=[pl.BlockSpec((tm, d), lambda i: (i, 0)),
                  pl.BlockSpec((1, d), lambda i: (0, 0))],
        out_specs=[pl.BlockSpec((tm, d), lambda i: (jnp.minimum(i, npt - 1), 0)),
                   pl.BlockSpec((ms, d), lambda i: (0, 0))],
        out_shape=[jax.ShapeDtypeStruct((n_prompt, d), F32), jax.ShapeDtypeStruct((ms, d), F32)],
        compiler_params=_cparams(1),
        name="rmsnorm_split",
    )(x, g.reshape(1, d))


def _mm_kernel(n_w, epilogue, has_res, a_ref, *refs):
    w_refs = refs[:n_w]
    res_ref = refs[n_w] if has_res else None
    o_ref = refs[n_w + has_res]
    wb_refs = refs[n_w + has_res + 1:]

    @pl.when(pl.program_id(1) == 0)
    def _():
        for w_ref, wb_ref in zip(w_refs, wb_refs):
            wb_ref[...] = w_ref[...].astype(BF16)

    a = a_ref[...]
    y = epilogue(*[jnp.dot(a, wb_ref[...], preferred_element_type=F32) for wb_ref in wb_refs])
    if has_res:
        y = res_ref[...] + y
    o_ref[...] = y.astype(o_ref.dtype)


def _matmul(a, ws, epilogue, out_dtype, res=None, layer=None, tm=640, tn=512, name="matmul"):
    m, k = a.shape
    n = ws[0].shape[-1]
    in_specs = [pl.BlockSpec((tm, k), lambda j, i: (i, 0))]
    if layer is None:
        in_specs += [pl.BlockSpec((k, tn), lambda j, i: (0, j)) for _ in ws]
    else:
        in_specs += [pl.BlockSpec((None, k, tn), lambda j, i: (layer, 0, j)) for _ in ws]
    args = [a, *ws]
    if res is not None:
        in_specs.append(pl.BlockSpec((tm, tn), lambda j, i: (i, j)))
        args.append(res)
    return pl.pallas_call(
        functools.partial(_mm_kernel, len(ws), epilogue, res is not None),
        grid=(n // tn, m // tm),
        in_specs=in_specs,
        out_specs=pl.BlockSpec((tm, tn), lambda j, i: (i, j)),
        out_shape=jax.ShapeDtypeStruct((m, n), out_dtype),
        scratch_shapes=[pltpu.VMEM((k, tn), BF16) for _ in ws],
        compiler_params=_cparams(2),
        name=name,
    )(*args)


def _epi_id(y):
    return y


def _epi_swiglu(g, u):
    return _silu(g) * u


def _epi_glu(a, b):
    return a * jax.nn.sigmoid(b)


def _rotary(x, cos, sin):
    x1 = x[:, :RET_HALF]
    x2 = x[:, RET_HALF:]
    return jnp.concatenate([x1 * cos - x2 * sin, x1 * sin + x2 * cos], axis=1)


def _group_norm_gate(o, gn, g):
    mu = jnp.mean(o, axis=-1, keepdims=True)
    d = o - mu
    var = jnp.mean(d * d, axis=-1, keepdims=True)
    return d * lax.rsqrt(var + EPS) * gn * _silu(g)


def _ret_prompt_kernel(q_ref, k_ref, v_ref, g_ref, cos_ref, sin_ref, intra_ref, qdec_ref, kdec_ref,
                       cdec_ref, gn_ref, mix_ref, o_ref, s_ref):
    del mix_ref

    @pl.when(pl.program_id(1) == 0)
    def _():
        s_ref[...] = jnp.zeros_like(s_ref)

    c = RET_CHUNK
    nch = RET_ROWS // c
    nh = RET_HEADS

    def split(ref, rotate):
        out = []
        for ci in range(nch):
            rows = pl.ds(ci * c, c)
            for k in range(nh):
                x = ref[rows, k * RET_DK:(k + 1) * RET_DK]
                out.append(_rotary(x, cos_ref[rows, :], sin_ref[rows, :]) if rotate else x)
        return jnp.stack(out)

    per_chunk = lambda t: jnp.concatenate([t] * nch, axis=0)
    q = split(q_ref, True)
    k = split(k_ref, True) * (RET_DK ** -0.5)
    v = split(v_ref, False).astype(BF16)
    att = jnp.einsum('bid,bjd->bij', q.astype(BF16), k.astype(BF16),
                     preferred_element_type=F32) * per_chunk(intra_ref[...])
    o = jnp.einsum('bij,bjv->biv', att.astype(BF16), v, preferred_element_type=F32)
    kv = jnp.einsum('bjd,bjv->bdv', (k * per_chunk(kdec_ref[...])).astype(BF16), v, preferred_element_type=F32)
    s = s_ref[0]
    cdec = cdec_ref[:, 0:1, :]
    starts = []
    for ci in range(nch):
        starts.append(s.astype(BF16))
        s = s * cdec + kv[ci * nh:(ci + 1) * nh]
    s_ref[0] = s
    o = o + jnp.einsum('bid,bdv->biv', (q * per_chunk(qdec_ref[...])).astype(BF16),
                       jnp.concatenate(starts, axis=0), preferred_element_type=F32)
    for ci in range(nch):
        rows = pl.ds(ci * c, c)
        for k in range(nh):
            cols = slice(k * RET_DK, (k + 1) * RET_DK)
            o_ref[rows, cols] = _group_norm_gate(o[ci * nh + k], gn_ref[:, cols], g_ref[rows, cols]).astype(o_ref.dtype)


def _ret_decay_tables(c):
    lg = np.log(1.0 - 2.0 ** (-5.0 - np.arange(RET_HEADS, dtype=np.float64)))
    idx = np.arange(c, dtype=np.float64)
    diff = idx[:, None] - idx[None, :]
    intra = np.where(diff >= 0, np.exp(np.maximum(diff, 0.0)[None] * lg[:, None, None]), 0.0)
    ones = np.ones((1, 1, RET_DK))
    qdec = np.exp((idx[None, :, None] + 1.0) * lg[:, None, None]) * ones
    kdec = np.exp((c - 1.0 - idx[None, :, None]) * lg[:, None, None]) * ones
    cdec = np.exp(c * lg)[:, None, None] * np.ones((1, 8, RET_DK))
    return [jnp.asarray(t, F32) for t in (intra, qdec, kdec, cdec)]


def _ret_prompt(proj, mix, cos, sin, gn, batch, seq):
    c = RET_CHUNK
    r = RET_ROWS
    nc = seq // r
    w = RET_HEADS * RET_DK
    tables = _ret_decay_tables(c)
    col = lambda j: pl.BlockSpec((r, w), lambda b, i: (b * nc + i, j))
    whole = lambda t: pl.BlockSpec(t.shape, lambda b, i: (0,) * t.ndim)
    return pl.pallas_call(
        _ret_prompt_kernel,
        grid=(batch, nc),
        in_specs=[col(0), col(1), col(2), col(3),
                  pl.BlockSpec((r, RET_HALF), lambda b, i: (i, 0)),
                  pl.BlockSpec((r, RET_HALF), lambda b, i: (i, 0)),
                  *[whole(t) for t in tables],
                  pl.BlockSpec((1, w), lambda b, i: (0, 0)),
                  pl.BlockSpec(memory_space=pl.ANY)],
        out_specs=[pl.BlockSpec((r, w), lambda b, i: (b * nc + i, 0)),
                   pl.BlockSpec((1, RET_HEADS, RET_DK, RET_DK), lambda b, i: (b, 0, 0, 0))],
        out_shape=[jax.ShapeDtypeStruct(mix.shape, mix.dtype),
                   jax.ShapeDtypeStruct((batch, RET_HEADS, RET_DK, RET_DK), F32)],
        input_output_aliases={11: 0},
        compiler_params=_cparams(2),
        name="ret_prompt",
    )(proj, proj, proj, proj, cos, sin, *tables, gn.reshape(1, MIX_HALF), mix)


def _columns(x, n):
    t = x.shape[0]
    parts = [jnp.concatenate([x[:, i:i + 128]] * (128 // t), axis=0).T for i in range(0, n, 128)]
    return parts[0] if len(parts) == 1 else jnp.concatenate(parts, axis=0)


def _ret_sample_kernel(q_ref, k_ref, v_ref, g_ref, cos_ref, sin_ref, cdec_ref, gn_ref, s_ref, mix_ref,
                       o_ref, so_ref):
    del mix_ref
    cos = cos_ref[...]
    sin = sin_ref[...]
    q = _rotary(q_ref[...], cos, sin)
    k = _rotary(k_ref[...], cos, sin) * (RET_DK ** -0.5)
    v = v_ref[...]
    kt = _columns(k, RET_DK)
    gamma = cdec_ref[0, 0:1, :]
    qb = q.astype(BF16)
    tok = lax.broadcasted_iota(jnp.int32, q.shape, 0)
    o = jnp.zeros(q.shape, F32)
    for t in range(SAMPLE_TOK):
        s_new = s_ref[t, 0] * gamma + kt[:, t:t + 1] * v[t:t + 1, :]
        so_ref[t, 0] = s_new
        o = jnp.where(tok == t, jnp.dot(qb, s_new.astype(BF16), preferred_element_type=F32), o)
    o_ref[...] = _group_norm_gate(o, gn_ref[...], g_ref[...]).astype(o_ref.dtype)


def _ret_sample(proj, mix, state, cos, sin, gn, row0):
    nb = state.shape[0]
    t = SAMPLE_TOK
    r0 = row0 // t
    _, _, _, cdec = _ret_decay_tables(1)
    col = lambda off: pl.BlockSpec((t, RET_DK), lambda b, h: (r0 + b, off + h))
    st = pl.BlockSpec((t, 1, RET_DK, RET_DK), lambda b, h: (b, h, 0, 0))
    return pl.pallas_call(
        _ret_sample_kernel,
        grid=(nb // t, RET_HEADS),
        in_specs=[col(0), col(RET_HEADS), col(2 * RET_HEADS), col(3 * RET_HEADS),
                  pl.BlockSpec((1, RET_HALF), lambda b, h: (0, 0)),
                  pl.BlockSpec((1, RET_HALF), lambda b, h: (0, 0)),
                  pl.BlockSpec((1, 8, RET_DK), lambda b, h: (h, 0, 0)),
                  pl.BlockSpec((1, RET_DK), lambda b, h: (0, h)),
                  st,
                  pl.BlockSpec(memory_space=pl.ANY)],
        out_specs=[pl.BlockSpec((t, RET_DK), lambda b, h: (r0 + b, h)), st],
        out_shape=[jax.ShapeDtypeStruct(mix.shape, mix.dtype),
                   jax.ShapeDtypeStruct(state.shape, F32)],
        input_output_aliases={9: 0},
        compiler_params=_cparams(2),
        name="ret_sample",
    )(proj, proj, proj, proj, cos, sin, cdec, gn.reshape(1, MIX_HALF), state, mix)


def _hg_lower_bound(lb_ref, layer):
    x = lb_ref[...]
    e = jnp.exp(x - jnp.max(x, axis=0, keepdims=True))
    return jnp.sum(e[:layer + 1], axis=0, keepdims=True) / jnp.sum(e, axis=0, keepdims=True)


def _hg_gates(gq, gf, lb):
    f = lb + (1.0 - lb) * jax.nn.sigmoid(gf)
    return _silu(gq), 1.0 - f, f


def _hg_out(o, gn, gg):
    return o * lax.rsqrt(jnp.mean(o * o, axis=-1, keepdims=True) + EPS) * gn * _silu(gg)


def _split3(x):
    hi = x.astype(BF16)
    r = x - hi.astype(F32)
    mid = r.astype(BF16)
    lo = (r - mid.astype(F32)).astype(BF16)
    return hi, mid, lo


def _hg_block(qq, kk, lf, v, st, tri):
    c = HG_CHUNK
    nch = qq.shape[0] // c
    nh = qq.shape[1] // HG_DK
    w = nh * HG_DK
    hi, mid, lo = _split3(lf)
    b3 = jnp.dot(tri, jnp.concatenate([hi, mid, lo], axis=1), preferred_element_type=F32)
    b = b3[:, :w] + b3[:, w:2 * w] + b3[:, 2 * w:]
    split = lambda t: jnp.stack([t[ci * c:(ci + 1) * c, k * HG_DK:(k + 1) * HG_DK]
                                 for ci in range(nch) for k in range(nh)])
    q3, k3, bc = split(qq), split(kk), split(b)
    vb = split(v.astype(BF16))
    nb = nch * nh
    ri = lax.broadcasted_iota(jnp.int32, (1, c, c), 1)
    cj = lax.broadcasted_iota(jnp.int32, (1, c, c), 2)
    mid = bc[:, c // 2 - 1:c // 2]
    off_mid = bc - mid

    def att_mid_referenced():
        q_m = (q3 * jnp.exp(off_mid)).astype(BF16)
        k_m = (k3 * jnp.exp(-off_mid)).astype(BF16)
        return jnp.where(ri >= cj, jnp.einsum('cid,cjd->cij', q_m, k_m, preferred_element_type=F32), 0.0)

    def att_any_decay():
        refs = [jnp.zeros_like(bc[:, 0:1])] + [bc[:, i0 - 1:i0] for i0 in range(HG_SUB, c, HG_SUB)]
        span = jnp.concatenate([jnp.broadcast_to(r, (nb, HG_SUB, HG_DK)) for r in refs], axis=1) - bc
        q_t = (q3 * jnp.exp(-span)).astype(BF16)
        row = lax.broadcasted_iota(jnp.int32, (1, c, HG_DK), 1)
        sub_row = lax.broadcasted_iota(jnp.int32, (1, HG_SUB, c), 1)
        sub_col = lax.broadcasted_iota(jnp.int32, (1, HG_SUB, c), 2)
        rows = []
        for i0 in range(0, c, HG_SUB):
            att_i = jnp.zeros((nb, HG_SUB, c), F32)
            if i0 > 0:
                r = refs[i0 // HG_SUB]
                k_t = jnp.where(row < i0, k3 * jnp.exp(jnp.minimum(r - bc, 0.0)), 0.0).astype(BF16)
                att_i = jnp.einsum('cid,cjd->cij', q_t[:, i0:i0 + HG_SUB], k_t, preferred_element_type=F32)
            q_i = q3[:, i0:i0 + HG_SUB]
            b_i = bc[:, i0:i0 + HG_SUB]
            for j in range(HG_SUB):
                jj = i0 + j
                p = q_i * (k3[:, jj:jj + 1] * jnp.exp(jnp.minimum(b_i - bc[:, jj:jj + 1], 0.0)))
                s_j = jnp.sum(p, axis=2, keepdims=True)
                att_i = jnp.where((sub_col == jj) & (sub_row >= j), s_j, att_i)
            rows.append(att_i)
        return jnp.concatenate(rows, axis=1)

    att = lax.cond(jnp.max(jnp.abs(off_mid)) < HG_SAFE_SPAN, att_mid_referenced, att_any_decay).astype(BF16)
    o_intra = jnp.einsum('cij,cjv->civ', att, vb, preferred_element_type=F32)
    b_last = bc[:, c - 1:c]
    qe = (q3 * jnp.exp(bc)).astype(BF16)
    khat = (k3 * jnp.exp(b_last - bc)).astype(BF16)
    dec = jnp.exp(b_last)
    upd = jnp.einsum('cjv,cjd->cvd', vb, khat, preferred_element_type=F32)
    starts = []
    for ci in range(nch):
        heads = slice(ci * nh, (ci + 1) * nh)
        starts.append(st.astype(BF16))
        st = st * dec[heads] + upd[heads]
    o = o_intra + jnp.einsum('cid,cvd->civ', qe, jnp.concatenate(starts, axis=0), preferred_element_type=F32)
    o = jnp.concatenate([jnp.concatenate([o[ci * nh + k] for k in range(nh)], axis=1) for ci in range(nch)], axis=0)
    return o, st


def _hg_prompt_kernel(layer, gq_ref, gf_ref, gi_ref, gg_ref, lb_ref, gn_ref, tri_ref, mix_ref,
                      o_ref, s_ref, st_ref):
    del mix_ref

    @pl.when(pl.program_id(2) == 0)
    def _():
        st_ref[...] = jnp.zeros_like(st_ref)

    qq, kk, f = _hg_gates(gq_ref[...], gf_ref[...], _hg_lower_bound(lb_ref, layer))
    o, st = _hg_block(qq, kk, jnp.log(f), gi_ref[...], st_ref[...], tri_ref[...])
    for k in range(HG_NH):
        cols = slice(k * HG_DK, (k + 1) * HG_DK)
        o_ref[:, cols] = _hg_out(o[:, cols], gn_ref[:, cols], gg_ref[:, cols]).astype(o_ref.dtype)
    st_ref[...] = st

    @pl.when(pl.program_id(2) == pl.num_programs(2) - 1)
    def _():
        for k in range(HG_NH):
            s_ref[0, k] = st[k].T


def _hg_prompt(proj, mix, lb_raw, gn, layer, batch, seq):
    nt = seq // HG_ROWS
    c0 = MIX_HALF * 4 // HG_DK
    nh = HG_NH
    w = nh * HG_DK
    c0 = c0 // nh
    col = lambda off: pl.BlockSpec((HG_ROWS, w), lambda b, h, i: (b * nt + i, c0 + off // nh + h))
    tri = jnp.asarray(np.kron(np.eye(HG_ROWS // HG_CHUNK), np.tril(np.ones((HG_CHUNK, HG_CHUNK)))), BF16)
    return pl.pallas_call(
        functools.partial(_hg_prompt_kernel, layer),
        grid=(batch, HG_HEADS // nh, nt),
        in_specs=[col(0), col(HG_HEADS), col(2 * HG_HEADS), col(3 * HG_HEADS),
                  pl.BlockSpec((lb_raw.shape[0], w), lambda b, h, i: (0, h)),
                  pl.BlockSpec((1, w), lambda b, h, i: (0, h)),
                  pl.BlockSpec((HG_ROWS, HG_ROWS), lambda b, h, i: (0, 0)),
                  pl.BlockSpec(memory_space=pl.ANY)],
        out_specs=[pl.BlockSpec((HG_ROWS, w), lambda b, h, i: (b * nt + i, HG_HEADS // nh + h)),
                   pl.BlockSpec((1, nh, HG_DK, HG_DK), lambda b, h, i: (b, h, 0, 0))],
        out_shape=[jax.ShapeDtypeStruct(mix.shape, mix.dtype),
                   jax.ShapeDtypeStruct((batch, HG_HEADS, HG_DK, HG_DK), F32)],
        scratch_shapes=[pltpu.VMEM((nh, HG_DK, HG_DK), F32)],
        input_output_aliases={7: 0},
        compiler_params=_cparams(3),
        name="hgrn_prompt",
    )(proj, proj, proj, proj, lb_raw, gn.reshape(1, MIX_HALF), tri, mix)


def _hg_sample_kernel(layer, gq_ref, gf_ref, gi_ref, gg_ref, lb_ref, gn_ref, s_ref, mix_ref,
                      o_ref, so_ref):
    del mix_ref
    lb = _hg_lower_bound(lb_ref, layer)
    qq, kk, f = _hg_gates(gq_ref[...], gf_ref[...], lb)
    v = gi_ref[...]
    kt = _columns(kk, HG_DK)
    ft = _columns(f, HG_DK)
    qb = qq.astype(BF16)
    tok = lax.broadcasted_iota(jnp.int32, qq.shape, 0)
    o = jnp.zeros(qq.shape, F32)
    for t in range(SAMPLE_TOK):
        s_new = s_ref[t, 0] * ft[:, t:t + 1] + kt[:, t:t + 1] * v[t:t + 1, :]
        so_ref[t, 0] = s_new
        o = jnp.where(tok == t, jnp.dot(qb, s_new.astype(BF16), preferred_element_type=F32), o)
    o_ref[...] = _hg_out(o, gn_ref[...], gg_ref[...]).astype(o_ref.dtype)


def _hg_sample(proj, mix, state, lb_raw, gn, layer, row0):
    nb = state.shape[0]
    t = SAMPLE_TOK
    r0 = row0 // t
    c0 = MIX_HALF * 4 // HG_DK
    col = lambda off: pl.BlockSpec((t, HG_DK), lambda b, h: (r0 + b, c0 + off + h))
    st = pl.BlockSpec((t, 1, HG_DK, HG_DK), lambda b, h: (b, h, 0, 0))
    return pl.pallas_call(
        functools.partial(_hg_sample_kernel, layer),
        grid=(nb // t, HG_HEADS),
        in_specs=[col(0), col(HG_HEADS), col(2 * HG_HEADS), col(3 * HG_HEADS),
                  pl.BlockSpec((lb_raw.shape[0], HG_DK), lambda b, h: (0, h)),
                  pl.BlockSpec((1, HG_DK), lambda b, h: (0, h)),
                  st,
                  pl.BlockSpec(memory_space=pl.ANY)],
        out_specs=[pl.BlockSpec((t, HG_DK), lambda b, h: (r0 + b, HG_HEADS + h)), st],
        out_shape=[jax.ShapeDtypeStruct(mix.shape, mix.dtype),
                   jax.ShapeDtypeStruct(state.shape, F32)],
        input_output_aliases={7: 0},
        compiler_params=_cparams(2),
        name="hgrn_sample",
    )(proj, proj, proj, proj, lb_raw, gn.reshape(1, MIX_HALF), state, mix)


S5_SEG = S5_SUB // 8
S5_POW_ROWS = 16
assert S5_SEG & (S5_SEG - 1) == 0


def _cmul(ar, ai, br, bi):
    return ar * br - ai * bi, ar * bi + ai * br


def _s5_prep_kernel(lr_ref, li_ref, ldt_ref, brt_ref, bit_ref, pr_ref, pi_ref, bbr_ref, bbi_ref):
    lr = lr_ref[...]
    li = li_ref[...]
    dt = jnp.exp(ldt_ref[...])
    mag = jnp.exp(lr * dt)
    ar = mag * jnp.cos(li * dt)
    ai = mag * jnp.sin(li * dt)
    den = lr * lr + li * li
    cr = ((ar - 1.0) * lr + ai * li) / den
    ci = (ai * lr - (ar - 1.0) * li) / den
    brt = brt_ref[...]
    bit = bit_ref[...]
    bbr_ref[...] = cr * brt - ci * bit
    bbi_ref[...] = cr * bit + ci * brt
    n = pr_ref.shape[1]
    pr_ref[0:8, :] = jnp.broadcast_to(ar, (8, n))
    pi_ref[0:8, :] = jnp.broadcast_to(ai, (8, n))
    seg_pow = (ar, ai)
    for _ in range(S5_SEG.bit_length() - 1):
        seg_pow = _cmul(*seg_pow, *seg_pow)
    p = seg_pow
    for m in range(8):
        pr_ref[8 + m:9 + m, :] = p[0]
        pi_ref[8 + m:9 + m, :] = p[1]
        p = _cmul(*p, *seg_pow)


def _s5_prep(lam_re, lam_im, log_dt, b_re, b_im):
    n = S5_GROUPS * S5_STATE
    flat = lambda t: t.reshape(1, n)
    b_t = lambda t: jnp.transpose(t, (2, 0, 1)).reshape(S5_GROUP, n)
    ldt = jnp.repeat(log_dt, S5_STATE).reshape(1, n)
    return pl.pallas_call(
        _s5_prep_kernel,
        out_shape=[jax.ShapeDtypeStruct((S5_POW_ROWS, n), F32)] * 2
        + [jax.ShapeDtypeStruct((S5_GROUP, n), F32)] * 2,
        compiler_params=pltpu.CompilerParams(vmem_limit_bytes=VMEM_LIMIT_BYTES),
        name="s5_prep",
    )(flat(lam_re), flat(lam_im), ldt, b_t(b_re), b_t(b_im))


def _s5_block_diag(bbr, bbi, c_re, c_im):
    eye = jnp.eye(S5_BLK_GROUPS, dtype=F32)
    def b_blk(t):
        t = t.reshape(S5_GROUP, S5_NBLK, S5_BLK_GROUPS, S5_STATE)
        t = jnp.einsum('cbgp,hg->bhcgp', t, eye)
        return t.reshape(S5_NBLK, S5_BLK_CH, S5_BLK_ST).astype(BF16)
    def c_blk(t):
        t = t.reshape(S5_NBLK, S5_BLK_GROUPS, S5_GROUP, S5_STATE)
        t = jnp.einsum('bgcp,hg->bhpgc', t, eye)
        return t.reshape(S5_NBLK, S5_BLK_ST, S5_BLK_CH).astype(BF16)
    return (jnp.concatenate([b_blk(bbr), b_blk(bbi)], axis=2),
            jnp.concatenate([c_blk(c_re), c_blk(-c_im)], axis=1))


def _s5_readout(u, h_bf16, c_ref, d_ref):
    y = jnp.dot(h_bf16, c_ref[0], preferred_element_type=F32) + d_ref[...] * u
    return jax.nn.gelu(y)


def _s5_row_perm(rows):
    seg = rows // 8
    p = np.zeros((rows, rows), np.float32)
    t, s = np.meshgrid(np.arange(seg), np.arange(8), indexing="ij")
    p[(t * 8 + s).ravel(), (s * seg + t).ravel()] = 1.0
    return jnp.asarray(p, BF16), jnp.asarray(p.T, BF16)


def _s5_prompt_kernel(u_ref, p_ref, pt_ref, b_ref, c_ref, pr_ref, pi_ref, d_ref, z_in_ref,
                      z_ref, hr_out, hi_out, *scratch):
    del z_in_ref
    x_refs = scratch[:S5_NSUB]
    hb_refs = scratch[S5_NSUB:2 * S5_NSUB]
    us_ref, cr_carry, ci_carry = scratch[2 * S5_NSUB:]
    tb = pl.program_id(2)
    n = S5_BLK_ST
    ch = S5_BLK_CH

    @pl.when(tb == 0)
    def _():
        cr_carry[...] = jnp.zeros_like(cr_carry)
        ci_carry[...] = jnp.zeros_like(ci_carry)

    for q in range(S5_NSUB):
        rows = slice(q * S5_SUB, (q + 1) * S5_SUB)
        up = jnp.dot(p_ref[...], jnp.concatenate(_split3(u_ref[rows, :]), axis=1),
                     preferred_element_type=F32)
        us_ref[rows, :] = (up[:, :ch] + up[:, ch:2 * ch]) + up[:, 2 * ch:]
        x_refs[q][...] = jnp.dot(up[:, :ch].astype(BF16), b_ref[0], preferred_element_type=F32)

    a_r = pr_ref[0:8, :]
    a_i = pi_ref[0:8, :]
    seg0 = 8
    row = lax.broadcasted_iota(jnp.int32, (8, n), 0)
    h0r = cr_carry[...]
    h0i = ci_carry[...]
    for q in range(S5_NSUB):
        x_ref = x_refs[q]
        def advance(t, h, x_ref=x_ref):
            rows = pl.ds(pl.multiple_of(t * 8, 8), 8)
            dr, di = _cmul(a_r, a_i, *h)
            return dr + x_ref[rows, 0:n], di + x_ref[rows, n:2 * n]

        zero = jnp.zeros((8, n), F32)
        er, ei = lax.fori_loop(0, S5_SEG, advance, (zero, zero), unroll=4)
        for s in range(3):
            m = 1 << s
            g_r = pr_ref[seg0 + m - 1:seg0 + m, :]
            g_i = pi_ref[seg0 + m - 1:seg0 + m, :]
            sr = jnp.where(row >= m, pltpu.roll(er, m, 0), 0.0)
            si = jnp.where(row >= m, pltpu.roll(ei, m, 0), 0.0)
            dr, di = _cmul(g_r, g_i, sr, si)
            er, ei = er + dr, ei + di
        dr, di = _cmul(pr_ref[seg0:seg0 + 8, :], pi_ref[seg0:seg0 + 8, :], h0r, h0i)
        er, ei = er + dr, ei + di
        cm_r = jnp.where(row >= 1, pltpu.roll(er, 1, 0), h0r)
        cm_i = jnp.where(row >= 1, pltpu.roll(ei, 1, 0), h0i)
        h0r = er[7:8, :]
        h0i = ei[7:8, :]
        def emit(t2, h, hb_ref=hb_refs[q]):
            h1 = advance(2 * t2, h)
            h2 = advance(2 * t2 + 1, h1)
            rows = pl.ds(pl.multiple_of(t2 * 16, 16), 16)
            hb_ref[rows, 0:n] = jnp.concatenate([h1[0], h2[0]], axis=0).astype(BF16)
            hb_ref[rows, n:2 * n] = jnp.concatenate([h1[1], h2[1]], axis=0).astype(BF16)
            return h2

        lax.fori_loop(0, S5_SEG // 2, emit, (cm_r, cm_i), unroll=2)
        rows = slice(q * S5_SUB, (q + 1) * S5_SUB)
        zp = _s5_readout(us_ref[rows, :], hb_refs[q][...], c_ref, d_ref).astype(BF16)
        z_ref[rows, :] = jnp.dot(pt_ref[...], zp, preferred_element_type=F32).astype(z_ref.dtype)
    cr_carry[...] = h0r
    ci_carry[...] = h0i

    @pl.when(tb == pl.num_programs(2) - 1)
    def _():
        hr_out[0, 0] = h0r
        hi_out[0, 0] = h0i


def _s5_prompt(u, z, mats, pows, d, batch, seq):
    b_m, c_m = mats
    p_r, p_i = pows
    perm, perm_t = _s5_row_perm(S5_SUB)
    nt = seq // S5_ROWS
    blk3 = lambda r, c: pl.BlockSpec((1, r, c), lambda b, j, i: (j, 0, 0))
    whole = pl.BlockSpec((S5_SUB, S5_SUB), lambda b, j, i: (0, 0))
    st_spec = pl.BlockSpec((1, 1, 1, S5_BLK_ST), lambda b, j, i: (b, j, 0, 0))
    st_shape = jax.ShapeDtypeStruct((batch, S5_NBLK, 1, S5_BLK_ST), F32)
    return pl.pallas_call(
        _s5_prompt_kernel,
        grid=(batch, S5_NBLK, nt),
        in_specs=[pl.BlockSpec((S5_ROWS, S5_BLK_CH), lambda b, j, i: (b * nt + i, j)),
                  whole, whole,
                  blk3(S5_BLK_CH, 2 * S5_BLK_ST), blk3(2 * S5_BLK_ST, S5_BLK_CH),
                  pl.BlockSpec((S5_POW_ROWS, S5_BLK_ST), lambda b, j, i: (0, j)),
                  pl.BlockSpec((S5_POW_ROWS, S5_BLK_ST), lambda b, j, i: (0, j)),
                  pl.BlockSpec((1, S5_BLK_CH), lambda b, j, i: (0, j)),
                  pl.BlockSpec(memory_space=pl.ANY)],
        out_specs=[pl.BlockSpec((S5_ROWS, S5_BLK_CH), lambda b, j, i: (b * nt + i, j)), st_spec, st_spec],
        out_shape=[jax.ShapeDtypeStruct(z.shape, z.dtype), st_shape, st_shape],
        input_output_aliases={8: 0},
        scratch_shapes=[pltpu.VMEM((S5_SUB, 2 * S5_BLK_ST), F32) for _ in range(S5_NSUB)]
        + [pltpu.VMEM((S5_SUB, 2 * S5_BLK_ST), BF16) for _ in range(S5_NSUB)]
        + [pltpu.VMEM((S5_ROWS, S5_BLK_CH), F32),
           pltpu.VMEM((1, S5_BLK_ST), F32), pltpu.VMEM((1, S5_BLK_ST), F32)],
        compiler_params=_cparams(3),
        name="s5_prompt",
    )(u, perm, perm_t, b_m, c_m, p_r, p_i, d.reshape(1, D_MODEL), z)


S5_T = 256
S5_HALF_ST = S5_BLK_ST // 2


def _s5_tb_mats(bbr, bbi, c_re, c_im, a_r, a_i, batch):
    hg = S5_BLK_GROUPS // 2
    eye = jnp.eye(hg, dtype=F32)
    def b_blk(t):
        t = t.reshape(S5_GROUP, S5_NBLK, 2, hg, S5_STATE)
        return jnp.einsum('cbhgp,kg->bhgckp', t, eye)
    def c_blk(t):
        t = t.reshape(S5_NBLK, 2, hg, S5_GROUP, S5_STATE)
        return jnp.einsum('bhgcp,kg->bkphgc', t, eye)
    b_m = jnp.stack([b_blk(bbr), b_blk(bbi)], axis=4)
    c_m = jnp.stack([c_blk(c_re), c_blk(-c_im)], axis=1)
    tile_a = lambda a: jnp.broadcast_to(a.reshape(S5_NBLK, 2, 1, S5_HALF_ST),
                                        (S5_NBLK, 2, batch, S5_HALF_ST)).reshape(S5_NBLK, 2 * batch, S5_HALF_ST)
    return (b_m.reshape(S5_NBLK, S5_BLK_CH, S5_BLK_ST).astype(BF16),
            c_m.reshape(S5_NBLK, S5_BLK_ST, S5_BLK_CH).astype(BF16), tile_a(a_r), tile_a(a_i))


S5_NB = 2


def _s5_tb_kernel(u_ref, b_ref, c_ref, ar_ref, ai_ref, d_ref, z_in_ref, z_ref, hr_out, hi_out,
                  x_ref, hb_ref, hr_ref, hi_ref):
    del z_in_ref
    i = pl.program_id(1)
    n = S5_HALF_ST
    ch = S5_BLK_CH
    t_steps = u_ref.shape[0] // 4
    half_rows = 4 * t_steps

    @pl.when(i == 0)
    def _():
        hr_ref[...] = jnp.zeros_like(hr_ref)
        hi_ref[...] = jnp.zeros_like(hi_ref)

    first4 = lax.broadcasted_iota(jnp.int32, (1, 8, ch), 1) < 4
    half0 = lax.broadcasted_iota(jnp.int32, (1, 8, ch), 2) < ch // 2
    for k in range(S5_NB):
        u3 = u_ref[:, k * ch:(k + 1) * ch].reshape(t_steps // 2, 8, ch)
        r3 = pltpu.roll(u3, 4, 1)
        even = jnp.where(first4, jnp.where(half0, u3, 0.0), jnp.where(half0, 0.0, r3))
        odd = jnp.where(first4, jnp.where(half0, r3, 0.0), jnp.where(half0, 0.0, u3))
        lhs = jnp.stack([even, odd], axis=1).reshape(8 * t_steps, ch).astype(BF16)
        for r0 in (0, half_rows):
            x_ref[k, r0:r0 + half_rows, :] = jnp.dot(lhs[r0:r0 + half_rows], b_ref[k],
                                                     preferred_element_type=F32)
    a = [(ar_ref[k], ai_ref[k]) for k in range(S5_NB)]

    def advance(t, hs):
        rows = pl.ds(pl.multiple_of(t * 8, 8), 8)
        out = []
        for k in range(S5_NB):
            dr, di = _cmul(*a[k], *hs[k])
            out.append((dr + x_ref[k, rows, 0:n], di + x_ref[k, rows, n:2 * n]))
        return tuple(out)

    def emit(t2, hs):
        h1 = advance(2 * t2, hs)
        h2 = advance(2 * t2 + 1, h1)
        rows = pl.ds(pl.multiple_of(t2 * 16, 16), 16)
        for k in range(S5_NB):
            hb_ref[k, rows, 0:n] = jnp.concatenate([h1[k][0], h2[k][0]], axis=0).astype(BF16)
            hb_ref[k, rows, n:2 * n] = jnp.concatenate([h1[k][1], h2[k][1]], axis=0).astype(BF16)
        return h2

    hs = lax.fori_loop(0, t_steps // 2, emit, tuple((hr_ref[k], hi_ref[k]) for k in range(S5_NB)), unroll=2)
    for k in range(S5_NB):
        hr_ref[k] = hs[k][0]
        hi_ref[k] = hs[k][1]

    pick = lambda yv: jnp.where(half0, yv, pltpu.roll(yv, 4, 1))
    for k in range(S5_NB):
        y2 = jnp.concatenate([jnp.dot(hb_ref[k, r0:r0 + half_rows, :], c_ref[k], preferred_element_type=F32)
                              for r0 in (0, half_rows)], axis=0)
        y4 = y2.reshape(t_steps // 2, 2, 8, ch)
        y = jnp.where(first4, pick(y4[:, 0]), pltpu.roll(pick(y4[:, 1]), 4, 1)).reshape(4 * t_steps, ch)
        cols = slice(k * ch, (k + 1) * ch)
        z_ref[:, cols] = jax.nn.gelu(y + d_ref[:, cols] * u_ref[:, cols]).astype(z_ref.dtype)

    @pl.when(i == pl.num_programs(1) - 1)
    def _():
        for k in range(S5_NB):
            for h in range(2):
                cols = slice((2 * k + h) * n, (2 * k + h + 1) * n)
                hr_out[:, cols] = hs[k][0][4 * h:4 * h + 4, :]
                hi_out[:, cols] = hs[k][1][4 * h:4 * h + 4, :]


def _s5_prompt_tb(u, z, mats, d, batch, seq):
    assert batch == 4
    b_m, c_m, a_r, a_i = mats
    rows = batch * S5_T
    nt = seq // S5_T
    nb = S5_NB
    blk3 = lambda r, c: pl.BlockSpec((nb, r, c), lambda j, i: (j, 0, 0))
    st_spec = pl.BlockSpec((batch, nb * S5_BLK_ST), lambda j, i: (0, j))
    st_shape = jax.ShapeDtypeStruct((batch, S5_GROUPS * S5_STATE), F32)
    return pl.pallas_call(
        _s5_tb_kernel,
        grid=(S5_NBLK // nb, nt),
        in_specs=[pl.BlockSpec((rows, nb * S5_BLK_CH), lambda j, i: (i, j)),
                  blk3(S5_BLK_CH, S5_BLK_ST), blk3(S5_BLK_ST, S5_BLK_CH),
                  blk3(2 * batch, S5_HALF_ST), blk3(2 * batch, S5_HALF_ST),
                  pl.BlockSpec((1, nb * S5_BLK_CH), lambda j, i: (0, j)),
                  pl.BlockSpec(memory_space=pl.ANY)],
        out_specs=[pl.BlockSpec((rows, nb * S5_BLK_CH), lambda j, i: (i, j)), st_spec, st_spec],
        out_shape=[jax.ShapeDtypeStruct(z.shape, z.dtype), st_shape, st_shape],
        scratch_shapes=[pltpu.VMEM((nb, 2 * rows, S5_BLK_ST), F32), pltpu.VMEM((nb, 2 * rows, S5_BLK_ST), BF16),
                        pltpu.VMEM((nb, 2 * batch, S5_HALF_ST), F32), pltpu.VMEM((nb, 2 * batch, S5_HALF_ST), F32)],
        input_output_aliases={6: 0},
        compiler_params=_cparams(2),
        name="s5_prompt",
    )(u, b_m, c_m, a_r, a_i, d.reshape(1, D_MODEL), z)


def _s5_sample_kernel(u_ref, b_ref, c_ref, ar_ref, ai_ref, d_ref, h0r_ref, h0i_ref,
                      z_in_ref, z_ref, hr_out, hi_out):
    del z_in_ref
    n = S5_HALF_ST
    u = u_ref[...]
    nb = u.shape[0]
    half0 = lax.broadcasted_iota(jnp.int32, u.shape, 1) < S5_BLK_CH // 2
    lhs = jnp.concatenate([jnp.where(half0, u, 0.0), jnp.where(half0, 0.0, u)], axis=0).astype(BF16)
    x = jnp.dot(lhs, b_ref[0], preferred_element_type=F32)
    halves = []
    for h in range(2):
        rows = slice(h * nb, (h + 1) * nb)
        cols = slice(h * n, (h + 1) * n)
        row_a = slice(h * (ar_ref.shape[1] // 2), h * (ar_ref.shape[1] // 2) + 1)
        dr, di = _cmul(ar_ref[0, row_a, :], ai_ref[0, row_a, :], h0r_ref[:, cols], h0i_ref[:, cols])
        hr = x[rows, 0:n] + dr
        hi = x[rows, n:2 * n] + di
        hr_out[:, cols] = hr
        hi_out[:, cols] = hi
        halves.append(jnp.concatenate([hr, hi], axis=1))
    y2 = jnp.dot(jnp.concatenate(halves, axis=0).astype(BF16), c_ref[0], preferred_element_type=F32)
    y = jnp.where(half0, y2[0:nb], y2[nb:2 * nb])
    z_ref[...] = jax.nn.gelu(y + d_ref[...] * u).astype(z_ref.dtype)


def _s5_sample(u, z, mats, d, h0r, h0i, row0):
    b_m, c_m, a_r, a_i = mats
    nb = h0r.shape[0]
    r0 = row0 // nb
    blk3 = lambda r, c: pl.BlockSpec((1, r, c), lambda j: (j, 0, 0))
    st_spec = pl.BlockSpec((nb, S5_BLK_ST), lambda j: (0, j))
    st_shape = jax.ShapeDtypeStruct(h0r.shape, F32)
    return pl.pallas_call(
        _s5_sample_kernel,
        grid=(S5_NBLK,),
        in_specs=[pl.BlockSpec((nb, S5_BLK_CH), lambda j: (r0, j)),
                  blk3(S5_BLK_CH, S5_BLK_ST), blk3(S5_BLK_ST, S5_BLK_CH),
                  blk3(a_r.shape[1], S5_HALF_ST), blk3(a_r.shape[1], S5_HALF_ST),
                  pl.BlockSpec((1, S5_BLK_CH), lambda j: (0, j)),
                  st_spec, st_spec,
                  pl.BlockSpec(memory_space=pl.ANY)],
        out_specs=[pl.BlockSpec((nb, S5_BLK_CH), lambda j: (r0, j)), st_spec, st_spec],
        out_shape=[jax.ShapeDtypeStruct(z.shape, z.dtype), st_shape, st_shape],
        input_output_aliases={8: 0},
        compiler_params=_cparams(1),
        name="s5_sample",
    )(u, b_m, c_m, a_r, a_i, d.reshape(1, D_MODEL), h0r, h0i, z)


TB_T = 64
LANES = 128


def _interleave_norm_kernel(n_steps, n_sample, batch, *refs):
    x_refs = refs[:batch]
    xs_ref, g_ref, xt_ref, ut_ref, slab_ref = refs[batch:]
    i = pl.program_id(0)
    d = xt_ref.shape[1]

    @pl.when(i < n_steps)
    def _():
        for dst_ref, norm in ((xt_ref, False), (ut_ref, True)):
            for b, x_ref in enumerate(x_refs):
                v = x_ref[...]
                if norm:
                    v = _rms(v, g_ref[...])
                for s in range(d // LANES):
                    slab_ref[s, pl.ds(b, TB_T, stride=batch), :] = v[:, s * LANES:(s + 1) * LANES]
            for s in range(d // LANES):
                dst_ref[:, s * LANES:(s + 1) * LANES] = slab_ref[s]

    @pl.when(i >= n_steps)
    def _():
        v = xs_ref[...]
        xt_ref[0:n_sample, :] = v
        ut_ref[0:n_sample, :] = _rms(v, g_ref[...])


def _interleave_norm(x, g, batch, seq):
    m, d = x.shape
    n_steps = seq // TB_T
    ms = m - batch * seq
    rows = batch * TB_T
    per_batch = lambda b: pl.BlockSpec((TB_T, d), lambda i: (b * n_steps + jnp.minimum(i, n_steps - 1), 0))
    out = pl.BlockSpec((rows, d), lambda i: (i, 0))
    return pl.pallas_call(
        functools.partial(_interleave_norm_kernel, n_steps, ms, batch),
        grid=(n_steps + 1,),
        in_specs=[per_batch(b) for b in range(batch)]
        + [pl.BlockSpec((ms, d), lambda i: (batch * seq // ms, 0)), pl.BlockSpec((1, d), lambda i: (0, 0))],
        out_specs=[out, out],
        out_shape=[jax.ShapeDtypeStruct((m, d), F32)] * 2,
        scratch_shapes=[pltpu.VMEM((d // LANES, rows, LANES), F32)],
        compiler_params=_cparams(1),
        name="interleave_norm",
    )(*([x] * (batch + 1)), g.reshape(1, d))


def _deinterleave_norm_kernel(n_steps, n_sample, batch, x_ref, g_ref, y_ref, ys_ref, slab_ref):
    i = pl.program_id(0)
    d = x_ref.shape[1]

    @pl.when(i < n_steps)
    def _():
        y = _rms(x_ref[...], g_ref[...])
        for s in range(d // LANES):
            slab_ref[s] = y[:, s * LANES:(s + 1) * LANES]
        for b in range(batch):
            for s in range(d // LANES):
                y_ref[b, :, s * LANES:(s + 1) * LANES] = slab_ref[s, pl.ds(b, TB_T, stride=batch), :]

    @pl.when(i >= n_steps)
    def _():
        ys_ref[...] = _rms(x_ref[0:n_sample, :], g_ref[...])


def _deinterleave_norm(x, g, batch, seq):
    m, d = x.shape
    n_steps = seq // TB_T
    ms = m - batch * seq
    rows = batch * TB_T
    return pl.pallas_call(
        functools.partial(_deinterleave_norm_kernel, n_steps, ms, batch),
        grid=(n_steps + 1,),
        in_specs=[pl.BlockSpec((rows, d), lambda i: (i, 0)), pl.BlockSpec((1, d), lambda i: (0, 0))],
        out_specs=[pl.BlockSpec((batch, TB_T, d), lambda i: (0, jnp.minimum(i, n_steps - 1), 0)),
                   pl.BlockSpec((ms, d), lambda i: (0, 0))],
        out_shape=[jax.ShapeDtypeStruct((batch, seq, d), F32), jax.ShapeDtypeStruct((ms, d), F32)],
        scratch_shapes=[pltpu.VMEM((d // LANES, rows, LANES), F32)],
        compiler_params=_cparams(1),
        name="deinterleave_norm",
    )(x, g.reshape(1, d))


def _rope_tables(pos):
    inv = ROPE_BASE ** (-jnp.arange(RET_HALF, dtype=F32) / RET_HALF)
    ang = pos[:, None] * inv[None, :]
    return jnp.cos(ang), jnp.sin(ang)


def _ffn(x, norm_g, wg, wu, wd, layer):
    h = _rmsnorm(x, norm_g[layer], [BF16])[0]
    t = _matmul(h, [wg, wu], _epi_swiglu, BF16, layer=layer, tm=1664, name="ffn_gate_up")
    return _matmul(t, [wd], _epi_id, F32, res=x, layer=layer, name="ffn_down")


def kernel(x_prompt, x_sample, state_ret, state_hgrn, state_s5_re, state_s5_im, attn_norm_g, w_in, ret_gn_g, hg_lb, hg_gn_g, w_out, ssm_norm_g, s5_lam_re, s5_lam_im, s5_log_dt, s5_b_re, s5_b_im, s5_c_re, s5_c_im, s5_d, w_glu_a, w_glu_b, ffn_norm_g, w_ffn_gate, w_ffn_up, w_ffn_down, final_norm_g):
    one = lambda t: t.reshape(t.shape[1:])
    (state_ret, state_hgrn, state_s5_re, state_s5_im, attn_norm_g, w_in, ret_gn_g, hg_gn_g, w_out, ssm_norm_g,
     s5_lam_re, s5_lam_im, s5_log_dt, s5_b_re, s5_b_im, s5_c_re, s5_c_im, s5_d, w_glu_a, w_glu_b) = map(one, (
         state_ret, state_hgrn, state_s5_re, state_s5_im, attn_norm_g, w_in, ret_gn_g, hg_gn_g, w_out, ssm_norm_g,
         s5_lam_re, s5_lam_im, s5_log_dt, s5_b_re, s5_b_im, s5_c_re, s5_c_im, s5_d, w_glu_a, w_glu_b))
    cos_p, sin_p = _rope_tables(jnp.arange(SEQ, dtype=F32))
    cos_s, sin_s = _rope_tables(jnp.full((1,), float(PAST_LEN), F32))

    h, x = _rmsnorm_stack(x_prompt.reshape(N_PROMPT, D_MODEL), x_sample.reshape(DEC_BATCH, D_MODEL), attn_norm_g)
    proj = _matmul(h, [w_in], _epi_id, F32, tm=1664, tn=1024, name="w_in")
    mix = jnp.zeros((N_ROWS, D_MODEL), BF16)
    mix, ret_p = _ret_prompt(proj, mix, cos_p, sin_p, ret_gn_g, BATCH, SEQ)
    mix, hg_p = _hg_prompt(proj, mix, hg_lb, hg_gn_g, 0, BATCH, SEQ)
    mix, ret_s = _ret_sample(proj, mix, state_ret, cos_s, sin_s, ret_gn_g, N_PROMPT)
    mix, hg_s = _hg_sample(proj, mix, state_hgrn, hg_lb, hg_gn_g, 0, N_PROMPT)
    x = _matmul(mix, [w_out], _epi_id, F32, res=x, tm=1664, name="w_out")
    x = _ffn(x, ffn_norm_g, w_ffn_gate, w_ffn_up, w_ffn_down, 0)

    x, u = _interleave_norm(x, ssm_norm_g, BATCH, SEQ)
    p_r, p_i, bbr, bbi = _s5_prep(s5_lam_re, s5_lam_im, s5_log_dt, s5_b_re, s5_b_im)
    mats = _s5_tb_mats(bbr, bbi, s5_c_re, s5_c_im, p_r[0], p_i[0], BATCH)
    z = jnp.zeros((N_ROWS, D_MODEL), BF16)
    z, s5r_p, s5i_p = _s5_prompt_tb(u, z, mats, s5_d, BATCH, SEQ)
    n_st = S5_GROUPS * S5_STATE
    z, s5r_s, s5i_s = _s5_sample(u, z, mats, s5_d, state_s5_re.reshape(DEC_BATCH, n_st),
                                 state_s5_im.reshape(DEC_BATCH, n_st), N_PROMPT)
    x = _matmul(z, [w_glu_a, w_glu_b], _epi_glu, F32, res=x, name="glu")
    x = _ffn(x, ffn_norm_g, w_ffn_gate, w_ffn_up, w_ffn_down, 1)

    y_p, y_s = _deinterleave_norm(x, final_norm_g, BATCH, SEQ)
    st = lambda t, b: t.reshape(1, b, S5_GROUPS, S5_STATE)
    return (y_p, y_s.reshape(DEC_BATCH, 1, D_MODEL),
            ret_p[None], ret_s[None], hg_p[None], hg_s[None],
            st(s5r_p, BATCH), st(s5i_p, BATCH), st(s5r_s, DEC_BATCH), st(s5i_s, DEC_BATCH))
```

```python
import functools

import numpy as np
import jax
import jax.numpy as jnp
from jax import lax
from jax.experimental import pallas as pl
from jax.experimental.pallas import tpu as pltpu

F32 = jnp.float32
BF16 = jnp.bfloat16

D_MODEL = 2048
BATCH = 4
SEQ = 2048
DEC_BATCH = 128
PAST_LEN = 16384
N_PROMPT = BATCH * SEQ
N_ROWS = N_PROMPT + DEC_BATCH
MIX_HALF = D_MODEL // 2
RET_HEADS = 4
RET_DK = MIX_HALF // RET_HEADS
RET_HALF = RET_DK // 2
RET_CHUNK = 128
RET_ROWS = 512
HG_HEADS = 8
HG_DK = MIX_HALF // HG_HEADS
HG_CHUNK = 64
HG_SUB = 16
HG_ROWS = 256
HG_NH = 8
HG_SAFE_SPAN = 40.0
S5_GROUP = 16
S5_GROUPS = D_MODEL // S5_GROUP
S5_STATE = 64
S5_BLK_GROUPS = 16
S5_BLK_CH = S5_BLK_GROUPS * S5_GROUP
S5_BLK_ST = S5_BLK_GROUPS * S5_STATE
S5_NBLK = S5_GROUPS // S5_BLK_GROUPS
ROPE_BASE = 10000.0
EPS = 1e-6
SAMPLE_TOK = 16

VMEM_LIMIT_BYTES = 56 * 1024 * 1024

MM_TILES = {"w_in": (1664, 1024), "w_out": (1664, 512), "ffn_gate_up": (1664, 512),
            "ffn_down": (640, 512), "glu": (640, 512)}


def _cparams(n_axes):
    return pltpu.CompilerParams(dimension_semantics=("arbitrary",) * n_axes,
                                vmem_limit_bytes=VMEM_LIMIT_BYTES)


def _silu(x):
    return x * jax.nn.sigmoid(x)


def _rmsnorm_kernel(x_ref, g_ref, *o_refs):
    x = x_ref[...]
    y = x * lax.rsqrt(jnp.mean(x * x, axis=-1, keepdims=True) + EPS) * g_ref[...]
    for o_ref in o_refs:
        o_ref[...] = y.astype(o_ref.dtype)


def _rmsnorm(x, g, dtypes, tm=640):
    m, d = x.shape
    return pl.pallas_call(
        _rmsnorm_kernel,
        grid=(m // tm,),
        in_specs=[pl.BlockSpec((tm, d), lambda i: (i, 0)),
                  pl.BlockSpec((1, d), lambda i: (0, 0))],
        out_specs=[pl.BlockSpec((tm, d), lambda i: (i, 0)) for _ in dtypes],
        out_shape=[jax.ShapeDtypeStruct((m, d), dt) for dt in dtypes],
        compiler_params=_cparams(1),
        name="rmsnorm",
    )(x, g.reshape(1, d))


ROW_TILE = 1024


def _rms(x, g):
    return x * lax.rsqrt(jnp.mean(x * x, axis=-1, keepdims=True) + EPS) * g


def _rmsnorm_stack_kernel(n_prompt_tiles, n_sample, xp_ref, xs_ref, g_ref, h_ref, x_ref):
    i = pl.program_id(0)

    @pl.when(i < n_prompt_tiles)
    def _():
        x = xp_ref[...]
        x_ref[...] = x
        h_ref[...] = _rms(x, g_ref[...]).astype(h_ref.dtype)

    @pl.when(i >= n_prompt_tiles)
    def _():
        x = xs_ref[...]
        x_ref[0:n_sample, :] = x
        h_ref[0:n_sample, :] = _rms(x, g_ref[...]).astype(h_ref.dtype)


def _rmsnorm_stack(x_prompt, x_sample, g):
    (mp, d), ms = x_prompt.shape, x_sample.shape[0]
    tm = ROW_TILE
    npt = mp // tm
    m = mp + ms
    return pl.pallas_call(
        functools.partial(_rmsnorm_stack_kernel, npt, ms),
        grid=(npt + 1,),
        in_specs=[pl.BlockSpec((tm, d), lambda i: (jnp.minimum(i, npt - 1), 0)),
                  pl.BlockSpec((ms, d), lambda i: (0, 0)),
                  pl.BlockSpec((1, d), lambda i: (0, 0))],
        out_specs=[pl.BlockSpec((tm, d), lambda i: (i, 0))] * 2,
        out_shape=[jax.ShapeDtypeStruct((m, d), BF16), jax.ShapeDtypeStruct((m, d), F32)],
        compiler_params=_cparams(1),
        name="rmsnorm_stack",
    )(x_prompt, x_sample, g.reshape(1, d))


def _mm_kernel(n_w, epilogue, has_res, a_ref, *refs):
    w_refs = refs[:n_w]
    res_ref = refs[n_w] if has_res else None
    o_ref = refs[n_w + has_res]
    wb_refs = refs[n_w + has_res + 1:]

    @pl.when(pl.program_id(1) == 0)
    def _():
        for w_ref, wb_ref in zip(w_refs, wb_refs):
            wb_ref[...] = w_ref[...].astype(BF16)

    a = a_ref[...]
    y = epilogue(*[jnp.dot(a, wb_ref[...], preferred_element_type=F32) for wb_ref in wb_refs])
    if has_res:
        y = res_ref[...] + y
    o_ref[...] = y.astype(o_ref.dtype)


def _matmul(a, ws, epilogue, out_dtype, name, res=None, layer=None):
    tm, tn = MM_TILES[name]
    m, k = a.shape
    n = ws[0].shape[-1]
    in_specs = [pl.BlockSpec((tm, k), lambda j, i: (i, 0))]
    if layer is None:
        in_specs += [pl.BlockSpec((k, tn), lambda j, i: (0, j)) for _ in ws]
    else:
        in_specs += [pl.BlockSpec((None, k, tn), lambda j, i: (layer, 0, j)) for _ in ws]
    args = [a, *ws]
    if res is not None:
        in_specs.append(pl.BlockSpec((tm, tn), lambda j, i: (i, j)))
        args.append(res)
    return pl.pallas_call(
        functools.partial(_mm_kernel, len(ws), epilogue, res is not None),
        grid=(n // tn, m // tm),
        in_specs=in_specs,
        out_specs=pl.BlockSpec((tm, tn), lambda j, i: (i, j)),
        out_shape=jax.ShapeDtypeStruct((m, n), out_dtype),
        scratch_shapes=[pltpu.VMEM((k, tn), BF16) for _ in ws],
        compiler_params=_cparams(2),
        name=name,
    )(*args)


def _epi_id(y):
    return y


def _epi_swiglu(g, u):
    return _silu(g) * u


def _epi_glu(a, b):
    return a * jax.nn.sigmoid(b)


def _rotary(x, cos, sin):
    x1 = x[:, :RET_HALF]
    x2 = x[:, RET_HALF:]
    return jnp.concatenate([x1 * cos - x2 * sin, x1 * sin + x2 * cos], axis=1)


def _group_norm_gate(o, gn, g):
    mu = jnp.mean(o, axis=-1, keepdims=True)
    d = o - mu
    var = jnp.mean(d * d, axis=-1, keepdims=True)
    return d * lax.rsqrt(var + EPS) * gn * _silu(g)


def _ret_prompt_kernel(q_ref, k_ref, v_ref, g_ref, cos_ref, sin_ref, intra_ref, qdec_ref, kdec_ref,
                       cdec_ref, gn_ref, mix_ref, o_ref, s_ref):
    del mix_ref

    @pl.when(pl.program_id(1) == 0)
    def _():
        s_ref[...] = jnp.zeros_like(s_ref)

    c = RET_CHUNK
    nch = RET_ROWS // c
    nh = RET_HEADS

    def split(ref, rotate):
        out = []
        for ci in range(nch):
            rows = pl.ds(ci * c, c)
            for k in range(nh):
                x = ref[rows, k * RET_DK:(k + 1) * RET_DK]
                out.append(_rotary(x, cos_ref[rows, :], sin_ref[rows, :]) if rotate else x)
        return jnp.stack(out)

    per_chunk = lambda t: jnp.concatenate([t] * nch, axis=0)
    q = split(q_ref, True)
    k = split(k_ref, True) * (RET_DK ** -0.5)
    v = split(v_ref, False).astype(BF16)
    att = jnp.einsum('bid,bjd->bij', q.astype(BF16), k.astype(BF16),
                     preferred_element_type=F32) * per_chunk(intra_ref[...])
    o = jnp.einsum('bij,bjv->biv', att.astype(BF16), v, preferred_element_type=F32)
    kv = jnp.einsum('bjd,bjv->bdv', (k * per_chunk(kdec_ref[...])).astype(BF16), v, preferred_element_type=F32)
    s = s_ref[0]
    cdec = cdec_ref[:, 0:1, :]
    starts = []
    for ci in range(nch):
        starts.append(s.astype(BF16))
        s = s * cdec + kv[ci * nh:(ci + 1) * nh]
    s_ref[0] = s
    o = o + jnp.einsum('bid,bdv->biv', (q * per_chunk(qdec_ref[...])).astype(BF16),
                       jnp.concatenate(starts, axis=0), preferred_element_type=F32)
    for ci in range(nch):
        rows = pl.ds(ci * c, c)
        for k in range(nh):
            cols = slice(k * RET_DK, (k + 1) * RET_DK)
            o_ref[rows, cols] = _group_norm_gate(o[ci * nh + k], gn_ref[:, cols], g_ref[rows, cols]).astype(o_ref.dtype)


def _ret_decay_tables(c):
    lg = np.log(1.0 - 2.0 ** (-5.0 - np.arange(RET_HEADS, dtype=np.float64)))
    idx = np.arange(c, dtype=np.float64)
    diff = idx[:, None] - idx[None, :]
    intra = np.where(diff >= 0, np.exp(np.maximum(diff, 0.0)[None] * lg[:, None, None]), 0.0)
    ones = np.ones((1, 1, RET_DK))
    qdec = np.exp((idx[None, :, None] + 1.0) * lg[:, None, None]) * ones
    kdec = np.exp((c - 1.0 - idx[None, :, None]) * lg[:, None, None]) * ones
    cdec = np.exp(c * lg)[:, None, None] * np.ones((1, 8, RET_DK))
    return [jnp.asarray(t, F32) for t in (intra, qdec, kdec, cdec)]


def _ret_prompt(proj, mix, cos, sin, gn, batch, seq):
    c = RET_CHUNK
    r = RET_ROWS
    nc = seq // r
    w = RET_HEADS * RET_DK
    tables = _ret_decay_tables(c)
    col = lambda j: pl.BlockSpec((r, w), lambda b, i: (b * nc + i, j))
    whole = lambda t: pl.BlockSpec(t.shape, lambda b, i: (0,) * t.ndim)
    return pl.pallas_call(
        _ret_prompt_kernel,
        grid=(batch, nc),
        in_specs=[col(0), col(1), col(2), col(3),
                  pl.BlockSpec((r, RET_HALF), lambda b, i: (i, 0)),
                  pl.BlockSpec((r, RET_HALF), lambda b, i: (i, 0)),
                  *[whole(t) for t in tables],
                  pl.BlockSpec((1, w), lambda b, i: (0, 0)),
                  pl.BlockSpec(memory_space=pl.ANY)],
        out_specs=[pl.BlockSpec((r, w), lambda b, i: (b * nc + i, 0)),
                   pl.BlockSpec((1, RET_HEADS, RET_DK, RET_DK), lambda b, i: (b, 0, 0, 0))],
        out_shape=[jax.ShapeDtypeStruct(mix.shape, mix.dtype),
                   jax.ShapeDtypeStruct((batch, RET_HEADS, RET_DK, RET_DK), F32)],
        input_output_aliases={11: 0},
        compiler_params=_cparams(2),
        name="ret_prompt",
    )(proj, proj, proj, proj, cos, sin, *tables, gn.reshape(1, MIX_HALF), mix)


def _columns(x, n):
    t = x.shape[0]
    parts = [jnp.concatenate([x[:, i:i + 128]] * (128 // t), axis=0).T for i in range(0, n, 128)]
    return parts[0] if len(parts) == 1 else jnp.concatenate(parts, axis=0)


def _ret_sample_kernel(q_ref, k_ref, v_ref, g_ref, cos_ref, sin_ref, cdec_ref, gn_ref, s_ref, mix_ref,
                       o_ref, so_ref):
    del mix_ref
    cos = cos_ref[...]
    sin = sin_ref[...]
    q = _rotary(q_ref[...], cos, sin)
    k = _rotary(k_ref[...], cos, sin) * (RET_DK ** -0.5)
    v = v_ref[...]
    kt = _columns(k, RET_DK)
    gamma = cdec_ref[0, 0:1, :]
    qb = q.astype(BF16)
    tok = lax.broadcasted_iota(jnp.int32, q.shape, 0)
    o = jnp.zeros(q.shape, F32)
    for t in range(SAMPLE_TOK):
        s_new = s_ref[t, 0] * gamma + kt[:, t:t + 1] * v[t:t + 1, :]
        so_ref[t, 0] = s_new
        o = jnp.where(tok == t, jnp.dot(qb, s_new.astype(BF16), preferred_element_type=F32), o)
    o_ref[...] = _group_norm_gate(o, gn_ref[...], g_ref[...]).astype(o_ref.dtype)


def _ret_sample(proj, mix, state, cos, sin, gn, row0):
    nb = state.shape[0]
    t = SAMPLE_TOK
    r0 = row0 // t
    _, _, _, cdec = _ret_decay_tables(1)
    col = lambda off: pl.BlockSpec((t, RET_DK), lambda b, h: (r0 + b, off + h))
    st = pl.BlockSpec((t, 1, RET_DK, RET_DK), lambda b, h: (b, h, 0, 0))
    return pl.pallas_call(
        _ret_sample_kernel,
        grid=(nb // t, RET_HEADS),
        in_specs=[col(0), col(RET_HEADS), col(2 * RET_HEADS), col(3 * RET_HEADS),
                  pl.BlockSpec((1, RET_HALF), lambda b, h: (0, 0)),
                  pl.BlockSpec((1, RET_HALF), lambda b, h: (0, 0)),
                  pl.BlockSpec((1, 8, RET_DK), lambda b, h: (h, 0, 0)),
                  pl.BlockSpec((1, RET_DK), lambda b, h: (0, h)),
                  st,
                  pl.BlockSpec(memory_space=pl.ANY)],
        out_specs=[pl.BlockSpec((t, RET_DK), lambda b, h: (r0 + b, h)), st],
        out_shape=[jax.ShapeDtypeStruct(mix.shape, mix.dtype),
                   jax.ShapeDtypeStruct(state.shape, F32)],
        input_output_aliases={9: 0},
        compiler_params=_cparams(2),
        name="ret_sample",
    )(proj, proj, proj, proj, cos, sin, cdec, gn.reshape(1, MIX_HALF), state, mix)


def _hg_lower_bound(lb_ref, layer):
    x = lb_ref[...]
    e = jnp.exp(x - jnp.max(x, axis=0, keepdims=True))
    return jnp.sum(e[:layer + 1], axis=0, keepdims=True) / jnp.sum(e, axis=0, keepdims=True)


def _hg_gates(gq, gf, lb):
    f = lb + (1.0 - lb) * jax.nn.sigmoid(gf)
    return _silu(gq), 1.0 - f, f


def _hg_out(o, gn, gg):
    return o * lax.rsqrt(jnp.mean(o * o, axis=-1, keepdims=True) + EPS) * gn * _silu(gg)


def _split3(x):
    hi = x.astype(BF16)
    r = x - hi.astype(F32)
    mid = r.astype(BF16)
    lo = (r - mid.astype(F32)).astype(BF16)
    return hi, mid, lo


def _hg_block(qq, kk, lf, v, st, tri):
    c = HG_CHUNK
    nch = qq.shape[0] // c
    nh = qq.shape[1] // HG_DK
    w = nh * HG_DK
    hi, mid, lo = _split3(lf)
    b3 = jnp.dot(tri, jnp.concatenate([hi, mid, lo], axis=1), preferred_element_type=F32)
    b = b3[:, :w] + b3[:, w:2 * w] + b3[:, 2 * w:]
    split = lambda t: jnp.stack([t[ci * c:(ci + 1) * c, k * HG_DK:(k + 1) * HG_DK]
                                 for ci in range(nch) for k in range(nh)])
    q3, k3, bc = split(qq), split(kk), split(b)
    vb = split(v.astype(BF16))
    nb = nch * nh
    ri = lax.broadcasted_iota(jnp.int32, (1, c, c), 1)
    cj = lax.broadcasted_iota(jnp.int32, (1, c, c), 2)
    mid = bc[:, c // 2 - 1:c // 2]
    off_mid = bc - mid

    def att_mid_referenced():
        q_m = (q3 * jnp.exp(off_mid)).astype(BF16)
        k_m = (k3 * jnp.exp(-off_mid)).astype(BF16)
        return jnp.where(ri >= cj, jnp.einsum('cid,cjd->cij', q_m, k_m, preferred_element_type=F32), 0.0)

    def att_any_decay():
        refs = [jnp.zeros_like(bc[:, 0:1])] + [bc[:, i0 - 1:i0] for i0 in range(HG_SUB, c, HG_SUB)]
        span = jnp.concatenate([jnp.broadcast_to(r, (nb, HG_SUB, HG_DK)) for r in refs], axis=1) - bc
        q_t = (q3 * jnp.exp(-span)).astype(BF16)
        row = lax.broadcasted_iota(jnp.int32, (1, c, HG_DK), 1)
        sub_row = lax.broadcasted_iota(jnp.int32, (1, HG_SUB, c), 1)
        sub_col = lax.broadcasted_iota(jnp.int32, (1, HG_SUB, c), 2)
        rows = []
        for i0 in range(0, c, HG_SUB):
            att_i = jnp.zeros((nb, HG_SUB, c), F32)
            if i0 > 0:
                r = refs[i0 // HG_SUB]
                k_t = jnp.where(row < i0, k3 * jnp.exp(jnp.minimum(r - bc, 0.0)), 0.0).astype(BF16)
                att_i = jnp.einsum('cid,cjd->cij', q_t[:, i0:i0 + HG_SUB], k_t, preferred_element_type=F32)
            q_i = q3[:, i0:i0 + HG_SUB]
            b_i = bc[:, i0:i0 + HG_SUB]
            for j in range(HG_SUB):
                jj = i0 + j
                p = q_i * (k3[:, jj:jj + 1] * jnp.exp(jnp.minimum(b_i - bc[:, jj:jj + 1], 0.0)))
                s_j = jnp.sum(p, axis=2, keepdims=True)
                att_i = jnp.where((sub_col == jj) & (sub_row >= j), s_j, att_i)
            rows.append(att_i)
        return jnp.concatenate(rows, axis=1)

    att = lax.cond(jnp.max(jnp.abs(off_mid)) < HG_SAFE_SPAN, att_mid_referenced, att_any_decay).astype(BF16)
    o_intra = jnp.einsum('cij,cjv->civ', att, vb, preferred_element_type=F32)
    b_last = bc[:, c - 1:c]
    qe = (q3 * jnp.exp(bc)).astype(BF16)
    khat = (k3 * jnp.exp(b_last - bc)).astype(BF16)
    dec = jnp.exp(b_last)
    upd = jnp.einsum('cjv,cjd->cvd', vb, khat, preferred_element_type=F32)
    starts = []
    for ci in range(nch):
        heads = slice(ci * nh, (ci + 1) * nh)
        starts.append(st.astype(BF16))
        st = st * dec[heads] + upd[heads]
    o = o_intra + jnp.einsum('cid,cvd->civ', qe, jnp.concatenate(starts, axis=0), preferred_element_type=F32)
    o = jnp.concatenate([jnp.concatenate([o[ci * nh + k] for k in range(nh)], axis=1) for ci in range(nch)], axis=0)
    return o, st


def _hg_prompt_kernel(layer, gq_ref, gf_ref, gi_ref, gg_ref, lb_ref, gn_ref, tri_ref, mix_ref,
                      o_ref, s_ref, st_ref):
    del mix_ref

    @pl.when(pl.program_id(2) == 0)
    def _():
        st_ref[...] = jnp.zeros_like(st_ref)

    qq, kk, f = _hg_gates(gq_ref[...], gf_ref[...], _hg_lower_bound(lb_ref, layer))
    o, st = _hg_block(qq, kk, jnp.log(f), gi_ref[...], st_ref[...], tri_ref[...])
    for k in range(HG_NH):
        cols = slice(k * HG_DK, (k + 1) * HG_DK)
        o_ref[:, cols] = _hg_out(o[:, cols], gn_ref[:, cols], gg_ref[:, cols]).astype(o_ref.dtype)
    st_ref[...] = st

    @pl.when(pl.program_id(2) == pl.num_programs(2) - 1)
    def _():
        for k in range(HG_NH):
            s_ref[0, k] = st[k].T


def _hg_prompt(proj, mix, lb_raw, gn, layer, batch, seq):
    nt = seq // HG_ROWS
    c0 = MIX_HALF * 4 // HG_DK
    nh = HG_NH
    w = nh * HG_DK
    c0 = c0 // nh
    col = lambda off: pl.BlockSpec((HG_ROWS, w), lambda b, h, i: (b * nt + i, c0 + off // nh + h))
    tri = jnp.asarray(np.kron(np.eye(HG_ROWS // HG_CHUNK), np.tril(np.ones((HG_CHUNK, HG_CHUNK)))), BF16)
    return pl.pallas_call(
        functools.partial(_hg_prompt_kernel, layer),
        grid=(batch, HG_HEADS // nh, nt),
        in_specs=[col(0), col(HG_HEADS), col(2 * HG_HEADS), col(3 * HG_HEADS),
                  pl.BlockSpec((lb_raw.shape[0], w), lambda b, h, i: (0, h)),
                  pl.BlockSpec((1, w), lambda b, h, i: (0, h)),
                  pl.BlockSpec((HG_ROWS, HG_ROWS), lambda b, h, i: (0, 0)),
                  pl.BlockSpec(memory_space=pl.ANY)],
        out_specs=[pl.BlockSpec((HG_ROWS, w), lambda b, h, i: (b * nt + i, HG_HEADS // nh + h)),
                   pl.BlockSpec((1, nh, HG_DK, HG_DK), lambda b, h, i: (b, h, 0, 0))],
        out_shape=[jax.ShapeDtypeStruct(mix.shape, mix.dtype),
                   jax.ShapeDtypeStruct((batch, HG_HEADS, HG_DK, HG_DK), F32)],
        scratch_shapes=[pltpu.VMEM((nh, HG_DK, HG_DK), F32)],
        input_output_aliases={7: 0},
        compiler_params=_cparams(3),
        name="hgrn_prompt",
    )(proj, proj, proj, proj, lb_raw, gn.reshape(1, MIX_HALF), tri, mix)


HG_SAMPLE_NH = 4


def _hg_sample_kernel(layer, gq_ref, gf_ref, gi_ref, gg_ref, lb_ref, gn_ref, s_ref, mix_ref,
                      o_ref, so_ref):
    del mix_ref
    qq_all, kk_all, f_all = _hg_gates(gq_ref[...], gf_ref[...], _hg_lower_bound(lb_ref, layer))
    tok = lax.broadcasted_iota(jnp.int32, (SAMPLE_TOK, HG_DK), 0)
    for k in range(HG_SAMPLE_NH):
        cols = slice(k * HG_DK, (k + 1) * HG_DK)
        v = gi_ref[:, cols]
        kt = _columns(kk_all[:, cols], HG_DK)
        ft = _columns(f_all[:, cols], HG_DK)
        qb = qq_all[:, cols].astype(BF16)
        o = jnp.zeros((SAMPLE_TOK, HG_DK), F32)
        for t in range(SAMPLE_TOK):
            s_new = s_ref[t, k] * ft[:, t:t + 1] + kt[:, t:t + 1] * v[t:t + 1, :]
            so_ref[t, k] = s_new
            o = jnp.where(tok == t, jnp.dot(qb, s_new.astype(BF16), preferred_element_type=F32), o)
        o_ref[:, cols] = _hg_out(o, gn_ref[:, cols], gg_ref[:, cols]).astype(o_ref.dtype)


def _hg_sample(proj, mix, state, lb_raw, gn, layer, row0):
    nb = state.shape[0]
    t = SAMPLE_TOK
    nh = HG_SAMPLE_NH
    w = nh * HG_DK
    r0 = row0 // t
    c0 = MIX_HALF * 4 // w
    col = lambda off: pl.BlockSpec((t, w), lambda b, h: (r0 + b, c0 + off // nh + h))
    st = pl.BlockSpec((t, nh, HG_DK, HG_DK), lambda b, h: (b, h, 0, 0))
    return pl.pallas_call(
        functools.partial(_hg_sample_kernel, layer),
        grid=(nb // t, HG_HEADS // nh),
        in_specs=[col(0), col(HG_HEADS), col(2 * HG_HEADS), col(3 * HG_HEADS),
                  pl.BlockSpec((lb_raw.shape[0], w), lambda b, h: (0, h)),
                  pl.BlockSpec((1, w), lambda b, h: (0, h)),
                  st,
                  pl.BlockSpec(memory_space=pl.ANY)],
        out_specs=[pl.BlockSpec((t, w), lambda b, h: (r0 + b, HG_HEADS // nh + h)), st],
        out_shape=[jax.ShapeDtypeStruct(mix.shape, mix.dtype),
                   jax.ShapeDtypeStruct(state.shape, F32)],
        input_output_aliases={7: 0},
        compiler_params=_cparams(2),
        name="hgrn_sample",
    )(proj, proj, proj, proj, lb_raw, gn.reshape(1, MIX_HALF), state, mix)


def _cmul(ar, ai, br, bi):
    return ar * br - ai * bi, ar * bi + ai * br


def _s5_prep_kernel(lr_ref, li_ref, ldt_ref, brt_ref, bit_ref, ar_ref, ai_ref, bbr_ref, bbi_ref):
    lr = lr_ref[...]
    li = li_ref[...]
    dt = jnp.exp(ldt_ref[...])
    mag = jnp.exp(lr * dt)
    ar = mag * jnp.cos(li * dt)
    ai = mag * jnp.sin(li * dt)
    den = lr * lr + li * li
    cr = ((ar - 1.0) * lr + ai * li) / den
    ci = (ai * lr - (ar - 1.0) * li) / den
    brt = brt_ref[...]
    bit = bit_ref[...]
    bbr_ref[...] = cr * brt - ci * bit
    bbi_ref[...] = cr * bit + ci * brt
    ar_ref[...] = ar
    ai_ref[...] = ai


def _s5_prep(lam_re, lam_im, log_dt, b_re, b_im):
    n = S5_GROUPS * S5_STATE
    flat = lambda t: t.reshape(1, n)
    b_t = lambda t: jnp.transpose(t, (2, 0, 1)).reshape(S5_GROUP, n)
    ldt = jnp.repeat(log_dt, S5_STATE).reshape(1, n)
    return pl.pallas_call(
        _s5_prep_kernel,
        out_shape=[jax.ShapeDtypeStruct((1, n), F32)] * 2 + [jax.ShapeDtypeStruct((S5_GROUP, n), F32)] * 2,
        compiler_params=pltpu.CompilerParams(vmem_limit_bytes=VMEM_LIMIT_BYTES),
        name="s5_prep",
    )(flat(lam_re), flat(lam_im), ldt, b_t(b_re), b_t(b_im))


S5_T = 256
S5_HALF_ST = S5_BLK_ST // 2


def _s5_tb_mats(bbr, bbi, c_re, c_im, a_r, a_i, batch):
    hg = S5_BLK_GROUPS // 2
    eye = jnp.eye(hg, dtype=F32)
    def b_blk(t):
        t = t.reshape(S5_GROUP, S5_NBLK, 2, hg, S5_STATE)
        return jnp.einsum('cbhgp,kg->bhgckp', t, eye)
    def c_blk(t):
        t = t.reshape(S5_NBLK, 2, hg, S5_GROUP, S5_STATE)
        return jnp.einsum('bhgcp,kg->bkphgc', t, eye)
    b_m = jnp.stack([b_blk(bbr), b_blk(bbi)], axis=4)
    c_m = jnp.stack([c_blk(c_re), c_blk(-c_im)], axis=1)
    tile_a = lambda a: jnp.broadcast_to(a.reshape(S5_NBLK, 2, 1, S5_HALF_ST),
                                        (S5_NBLK, 2, batch, S5_HALF_ST)).reshape(S5_NBLK, 2 * batch, S5_HALF_ST)
    return (b_m.reshape(S5_NBLK, S5_BLK_CH, S5_BLK_ST).astype(BF16),
            c_m.reshape(S5_NBLK, S5_BLK_ST, S5_BLK_CH).astype(BF16), tile_a(a_r), tile_a(a_i))


S5_NB = 2


def _s5_tb_kernel(u_ref, b_ref, c_ref, ar_ref, ai_ref, d_ref, z_in_ref, z_ref, hr_out, hi_out,
                  x_ref, hb_ref, hr_ref, hi_ref):
    del z_in_ref
    i = pl.program_id(1)
    n = S5_HALF_ST
    ch = S5_BLK_CH
    t_steps = u_ref.shape[0] // 4
    half_rows = 4 * t_steps

    @pl.when(i == 0)
    def _():
        hr_ref[...] = jnp.zeros_like(hr_ref)
        hi_ref[...] = jnp.zeros_like(hi_ref)

    first4 = lax.broadcasted_iota(jnp.int32, (1, 8, ch), 1) < 4
    half0 = lax.broadcasted_iota(jnp.int32, (1, 8, ch), 2) < ch // 2
    for k in range(S5_NB):
        u3 = u_ref[:, k * ch:(k + 1) * ch].reshape(t_steps // 2, 8, ch)
        r3 = pltpu.roll(u3, 4, 1)
        even = jnp.where(first4, jnp.where(half0, u3, 0.0), jnp.where(half0, 0.0, r3))
        odd = jnp.where(first4, jnp.where(half0, r3, 0.0), jnp.where(half0, 0.0, u3))
        lhs = jnp.stack([even, odd], axis=1).reshape(8 * t_steps, ch).astype(BF16)
        for r0 in (0, half_rows):
            x_ref[k, r0:r0 + half_rows, :] = jnp.dot(lhs[r0:r0 + half_rows], b_ref[k],
                                                     preferred_element_type=F32)
    a = [(ar_ref[k], ai_ref[k]) for k in range(S5_NB)]

    def advance(t, hs):
        rows = pl.ds(pl.multiple_of(t * 8, 8), 8)
        out = []
        for k in range(S5_NB):
            dr, di = _cmul(*a[k], *hs[k])
            out.append((dr + x_ref[k, rows, 0:n], di + x_ref[k, rows, n:2 * n]))
        return tuple(out)

    def emit(t2, hs):
        h1 = advance(2 * t2, hs)
        h2 = advance(2 * t2 + 1, h1)
        rows = pl.ds(pl.multiple_of(t2 * 16, 16), 16)
        for k in range(S5_NB):
            hb_ref[k, rows, 0:n] = jnp.concatenate([h1[k][0], h2[k][0]], axis=0).astype(BF16)
            hb_ref[k, rows, n:2 * n] = jnp.concatenate([h1[k][1], h2[k][1]], axis=0).astype(BF16)
        return h2

    hs = lax.fori_loop(0, t_steps // 2, emit, tuple((hr_ref[k], hi_ref[k]) for k in range(S5_NB)), unroll=2)
    for k in range(S5_NB):
        hr_ref[k] = hs[k][0]
        hi_ref[k] = hs[k][1]

    pick = lambda yv: jnp.where(half0, yv, pltpu.roll(yv, 4, 1))
    for k in range(S5_NB):
        y2 = jnp.concatenate([jnp.dot(hb_ref[k, r0:r0 + half_rows, :], c_ref[k], preferred_element_type=F32)
                              for r0 in (0, half_rows)], axis=0)
        y4 = y2.reshape(t_steps // 2, 2, 8, ch)
        y = jnp.where(first4, pick(y4[:, 0]), pltpu.roll(pick(y4[:, 1]), 4, 1)).reshape(4 * t_steps, ch)
        cols = slice(k * ch, (k + 1) * ch)
        z_ref[:, cols] = jax.nn.gelu(y + d_ref[:, cols] * u_ref[:, cols]).astype(z_ref.dtype)

    @pl.when(i == pl.num_programs(1) - 1)
    def _():
        for k in range(S5_NB):
            for h in range(2):
                cols = slice((2 * k + h) * n, (2 * k + h + 1) * n)
                hr_out[:, cols] = hs[k][0][4 * h:4 * h + 4, :]
                hi_out[:, cols] = hs[k][1][4 * h:4 * h + 4, :]


def _s5_prompt_tb(u, z, mats, d, batch, seq):
    assert batch == 4
    b_m, c_m, a_r, a_i = mats
    rows = batch * S5_T
    nt = seq // S5_T
    nb = S5_NB
    blk3 = lambda r, c: pl.BlockSpec((nb, r, c), lambda j, i: (j, 0, 0))
    st_spec = pl.BlockSpec((batch, nb * S5_BLK_ST), lambda j, i: (0, j))
    st_shape = jax.ShapeDtypeStruct((batch, S5_GROUPS * S5_STATE), F32)
    return pl.pallas_call(
        _s5_tb_kernel,
        grid=(S5_NBLK // nb, nt),
        in_specs=[pl.BlockSpec((rows, nb * S5_BLK_CH), lambda j, i: (i, j)),
                  blk3(S5_BLK_CH, S5_BLK_ST), blk3(S5_BLK_ST, S5_BLK_CH),
                  blk3(2 * batch, S5_HALF_ST), blk3(2 * batch, S5_HALF_ST),
                  pl.BlockSpec((1, nb * S5_BLK_CH), lambda j, i: (0, j)),
                  pl.BlockSpec(memory_space=pl.ANY)],
        out_specs=[pl.BlockSpec((rows, nb * S5_BLK_CH), lambda j, i: (i, j)), st_spec, st_spec],
        out_shape=[jax.ShapeDtypeStruct(z.shape, z.dtype), st_shape, st_shape],
        scratch_shapes=[pltpu.VMEM((nb, 2 * rows, S5_BLK_ST), F32), pltpu.VMEM((nb, 2 * rows, S5_BLK_ST), BF16),
                        pltpu.VMEM((nb, 2 * batch, S5_HALF_ST), F32), pltpu.VMEM((nb, 2 * batch, S5_HALF_ST), F32)],
        input_output_aliases={6: 0},
        compiler_params=_cparams(2),
        name="s5_prompt",
    )(u, b_m, c_m, a_r, a_i, d.reshape(1, D_MODEL), z)


def _s5_sample_kernel(u_ref, b_ref, c_ref, ar_ref, ai_ref, d_ref, h0r_ref, h0i_ref,
                      z_in_ref, z_ref, hr_out, hi_out):
    del z_in_ref
    n = S5_HALF_ST
    u = u_ref[...]
    nb = u.shape[0]
    half0 = lax.broadcasted_iota(jnp.int32, u.shape, 1) < S5_BLK_CH // 2
    lhs = jnp.concatenate([jnp.where(half0, u, 0.0), jnp.where(half0, 0.0, u)], axis=0).astype(BF16)
    x = jnp.dot(lhs, b_ref[0], preferred_element_type=F32)
    halves = []
    for h in range(2):
        rows = slice(h * nb, (h + 1) * nb)
        cols = slice(h * n, (h + 1) * n)
        row_a = slice(h * (ar_ref.shape[1] // 2), h * (ar_ref.shape[1] // 2) + 1)
        dr, di = _cmul(ar_ref[0, row_a, :], ai_ref[0, row_a, :], h0r_ref[:, cols], h0i_ref[:, cols])
        hr = x[rows, 0:n] + dr
        hi = x[rows, n:2 * n] + di
        hr_out[:, cols] = hr
        hi_out[:, cols] = hi
        halves.append(jnp.concatenate([hr, hi], axis=1))
    y2 = jnp.dot(jnp.concatenate(halves, axis=0).astype(BF16), c_ref[0], preferred_element_type=F32)
    y = jnp.where(half0, y2[0:nb], y2[nb:2 * nb])
    z_ref[...] = jax.nn.gelu(y + d_ref[...] * u).astype(z_ref.dtype)


def _s5_sample(u, z, mats, d, h0r, h0i, row0):
    b_m, c_m, a_r, a_i = mats
    nb = h0r.shape[0]
    r0 = row0 // nb
    blk3 = lambda r, c: pl.BlockSpec((1, r, c), lambda j: (j, 0, 0))
    st_spec = pl.BlockSpec((nb, S5_BLK_ST), lambda j: (0, j))
    st_shape = jax.ShapeDtypeStruct(h0r.shape, F32)
    return pl.pallas_call(
        _s5_sample_kernel,
        grid=(S5_NBLK,),
        in_specs=[pl.BlockSpec((nb, S5_BLK_CH), lambda j: (r0, j)),
                  blk3(S5_BLK_CH, S5_BLK_ST), blk3(S5_BLK_ST, S5_BLK_CH),
                  blk3(a_r.shape[1], S5_HALF_ST), blk3(a_r.shape[1], S5_HALF_ST),
                  pl.BlockSpec((1, S5_BLK_CH), lambda j: (0, j)),
                  st_spec, st_spec,
                  pl.BlockSpec(memory_space=pl.ANY)],
        out_specs=[pl.BlockSpec((nb, S5_BLK_CH), lambda j: (r0, j)), st_spec, st_spec],
        out_shape=[jax.ShapeDtypeStruct(z.shape, z.dtype), st_shape, st_shape],
        input_output_aliases={8: 0},
        compiler_params=_cparams(1),
        name="s5_sample",
    )(u, b_m, c_m, a_r, a_i, d.reshape(1, D_MODEL), h0r, h0i, z)


TB_T = 128
LANES = 128


def _interleave_norm_kernel(n_steps, n_sample, batch, *refs):
    x_refs = refs[:batch]
    xs_ref, g_ref, xt_ref, ut_ref, slab_ref = refs[batch:]
    i = pl.program_id(0)
    d = xt_ref.shape[1]

    @pl.when(i < n_steps)
    def _():
        for dst_ref, norm in ((xt_ref, False), (ut_ref, True)):
            for b, x_ref in enumerate(x_refs):
                v = x_ref[...]
                if norm:
                    v = _rms(v, g_ref[...])
                for s in range(d // LANES):
                    slab_ref[s, pl.ds(b, TB_T, stride=batch), :] = v[:, s * LANES:(s + 1) * LANES]
            for s in range(d // LANES):
                dst_ref[:, s * LANES:(s + 1) * LANES] = slab_ref[s]

    @pl.when(i >= n_steps)
    def _():
        v = xs_ref[...]
        xt_ref[0:n_sample, :] = v
        ut_ref[0:n_sample, :] = _rms(v, g_ref[...])


def _interleave_norm(x, g, batch, seq):
    m, d = x.shape
    n_steps = seq // TB_T
    ms = m - batch * seq
    rows = batch * TB_T
    per_batch = lambda b: pl.BlockSpec((TB_T, d), lambda i: (b * n_steps + jnp.minimum(i, n_steps - 1), 0))
    out = pl.BlockSpec((rows, d), lambda i: (i, 0))
    return pl.pallas_call(
        functools.partial(_interleave_norm_kernel, n_steps, ms, batch),
        grid=(n_steps + 1,),
        in_specs=[per_batch(b) for b in range(batch)]
        + [pl.BlockSpec((ms, d), lambda i: (batch * seq // ms, 0)), pl.BlockSpec((1, d), lambda i: (0, 0))],
        out_specs=[out, out],
        out_shape=[jax.ShapeDtypeStruct((m, d), F32)] * 2,
        scratch_shapes=[pltpu.VMEM((d // LANES, rows, LANES), F32)],
        compiler_params=_cparams(1),
        name="interleave_norm",
    )(*([x] * (batch + 1)), g.reshape(1, d))


def _deinterleave_norm_kernel(n_steps, n_sample, batch, x_ref, g_ref, y_ref, ys_ref, slab_ref):
    i = pl.program_id(0)
    d = x_ref.shape[1]

    @pl.when(i < n_steps)
    def _():
        y = _rms(x_ref[...], g_ref[...])
        for s in range(d // LANES):
            slab_ref[s] = y[:, s * LANES:(s + 1) * LANES]
        for b in range(batch):
            for s in range(d // LANES):
                y_ref[b, :, s * LANES:(s + 1) * LANES] = slab_ref[s, pl.ds(b, TB_T, stride=batch), :]

    @pl.when(i >= n_steps)
    def _():
        ys_ref[...] = _rms(x_ref[0:n_sample, :], g_ref[...])


def _deinterleave_norm(x, g, batch, seq):
    m, d = x.shape
    n_steps = seq // TB_T
    ms = m - batch * seq
    rows = batch * TB_T
    return pl.pallas_call(
        functools.partial(_deinterleave_norm_kernel, n_steps, ms, batch),
        grid=(n_steps + 1,),
        in_specs=[pl.BlockSpec((rows, d), lambda i: (i, 0)), pl.BlockSpec((1, d), lambda i: (0, 0))],
        out_specs=[pl.BlockSpec((batch, TB_T, d), lambda i: (0, jnp.minimum(i, n_steps - 1), 0)),
                   pl.BlockSpec((ms, d), lambda i: (0, 0))],
        out_shape=[jax.ShapeDtypeStruct((batch, seq, d), F32), jax.ShapeDtypeStruct((ms, d), F32)],
        scratch_shapes=[pltpu.VMEM((d // LANES, rows, LANES), F32)],
        compiler_params=_cparams(1),
        name="deinterleave_norm",
    )(x, g.reshape(1, d))


def _rope_tables(pos):
    inv = ROPE_BASE ** (-jnp.arange(RET_HALF, dtype=F32) / RET_HALF)
    ang = pos[:, None] * inv[None, :]
    return jnp.cos(ang), jnp.sin(ang)


def _ffn(x, norm_g, wg, wu, wd, layer):
    h = _rmsnorm(x, norm_g[layer], [BF16])[0]
    t = _matmul(h, [wg, wu], _epi_swiglu, BF16, "ffn_gate_up", layer=layer)
    return _matmul(t, [wd], _epi_id, F32, "ffn_down", res=x, layer=layer)


def kernel(x_prompt, x_sample, state_ret, state_hgrn, state_s5_re, state_s5_im, attn_norm_g, w_in, ret_gn_g, hg_lb, hg_gn_g, w_out, ssm_norm_g, s5_lam_re, s5_lam_im, s5_log_dt, s5_b_re, s5_b_im, s5_c_re, s5_c_im, s5_d, w_glu_a, w_glu_b, ffn_norm_g, w_ffn_gate, w_ffn_up, w_ffn_down, final_norm_g):
    one = lambda t: t.reshape(t.shape[1:])
    (state_ret, state_hgrn, state_s5_re, state_s5_im, attn_norm_g, w_in, ret_gn_g, hg_gn_g, w_out, ssm_norm_g,
     s5_lam_re, s5_lam_im, s5_log_dt, s5_b_re, s5_b_im, s5_c_re, s5_c_im, s5_d, w_glu_a, w_glu_b) = map(one, (
         state_ret, state_hgrn, state_s5_re, state_s5_im, attn_norm_g, w_in, ret_gn_g, hg_gn_g, w_out, ssm_norm_g,
         s5_lam_re, s5_lam_im, s5_log_dt, s5_b_re, s5_b_im, s5_c_re, s5_c_im, s5_d, w_glu_a, w_glu_b))
    cos_p, sin_p = _rope_tables(jnp.arange(SEQ, dtype=F32))
    cos_s, sin_s = _rope_tables(jnp.full((1,), float(PAST_LEN), F32))

    h, x = _rmsnorm_stack(x_prompt.reshape(N_PROMPT, D_MODEL), x_sample.reshape(DEC_BATCH, D_MODEL), attn_norm_g)
    proj = _matmul(h, [w_in], _epi_id, F32, "w_in")
    mix = jnp.zeros((N_ROWS, D_MODEL), BF16)
    mix, ret_p = _ret_prompt(proj, mix, cos_p, sin_p, ret_gn_g, BATCH, SEQ)
    mix, hg_p = _hg_prompt(proj, mix, hg_lb, hg_gn_g, 0, BATCH, SEQ)
    mix, ret_s = _ret_sample(proj, mix, state_ret, cos_s, sin_s, ret_gn_g, N_PROMPT)
    mix, hg_s = _hg_sample(proj, mix, state_hgrn, hg_lb, hg_gn_g, 0, N_PROMPT)
    x = _matmul(mix, [w_out], _epi_id, F32, "w_out", res=x)
    x = _ffn(x, ffn_norm_g, w_ffn_gate, w_ffn_up, w_ffn_down, 0)

    x, u = _interleave_norm(x, ssm_norm_g, BATCH, SEQ)
    p_r, p_i, bbr, bbi = _s5_prep(s5_lam_re, s5_lam_im, s5_log_dt, s5_b_re, s5_b_im)
    mats = _s5_tb_mats(bbr, bbi, s5_c_re, s5_c_im, p_r[0], p_i[0], BATCH)
    z = jnp.zeros((N_ROWS, D_MODEL), BF16)
    z, s5r_p, s5i_p = _s5_prompt_tb(u, z, mats, s5_d, BATCH, SEQ)
    n_st = S5_GROUPS * S5_STATE
    z, s5r_s, s5i_s = _s5_sample(u, z, mats, s5_d, state_s5_re.reshape(DEC_BATCH, n_st),
                                 state_s5_im.reshape(DEC_BATCH, n_st), N_PROMPT)
    x = _matmul(z, [w_glu_a, w_glu_b], _epi_glu, F32, "glu", res=x)
    x = _ffn(x, ffn_norm_g, w_ffn_gate, w_ffn_up, w_ffn_down, 1)

    y_p, y_s = _deinterleave_norm(x, final_norm_g, BATCH, SEQ)
    st = lambda t, b: t.reshape(1, b, S5_GROUPS, S5_STATE)
    return (y_p, y_s.reshape(DEC_BATCH, 1, D_MODEL),
            ret_p[None], ret_s[None], hg_p[None], hg_s[None],
            st(s5r_p, BATCH), st(s5i_p, BATCH), st(s5r_s, DEC_BATCH), st(s5i_s, DEC_BATCH))
```

```python
import functools

import numpy as np
import jax
import jax.numpy as jnp
from jax import lax
from jax.experimental import pallas as pl
from jax.experimental.pallas import tpu as pltpu

F32 = jnp.float32
BF16 = jnp.bfloat16

D_MODEL = 2048
BATCH = 4
SEQ = 2048
DEC_BATCH = 128
PAST_LEN = 16384
N_PROMPT = BATCH * SEQ
N_ROWS = N_PROMPT + DEC_BATCH
MIX_HALF = D_MODEL // 2
RET_HEADS = 4
RET_DK = MIX_HALF // RET_HEADS
RET_HALF = RET_DK // 2
RET_CHUNK = 128
RET_ROWS = 512
HG_HEADS = 8
HG_DK = MIX_HALF // HG_HEADS
HG_CHUNK = 64
HG_SUB = 16
HG_ROWS = 256
HG_NH = 8
HG_SAFE_SPAN = 40.0
S5_GROUP = 16
S5_GROUPS = D_MODEL // S5_GROUP
S5_STATE = 64
S5_BLK_GROUPS = 16
S5_BLK_CH = S5_BLK_GROUPS * S5_GROUP
S5_BLK_ST = S5_BLK_GROUPS * S5_STATE
S5_NBLK = S5_GROUPS // S5_BLK_GROUPS
ROPE_BASE = 10000.0
EPS = 1e-6
SAMPLE_TOK = 16

VMEM_LIMIT_BYTES = 56 * 1024 * 1024

MM_TILES = {"w_in": (1664, 1024), "w_out": (1664, 512), "ffn_gate_up": (1664, 512),
            "ffn_down": (640, 512), "glu": (640, 512)}
MM_SUB_TILES = 2


def _cparams(n_axes):
    return pltpu.CompilerParams(dimension_semantics=("arbitrary",) * n_axes,
                                vmem_limit_bytes=VMEM_LIMIT_BYTES)


def _silu(x):
    return x * jax.nn.sigmoid(x)


def _rmsnorm_kernel(x_ref, g_ref, *o_refs):
    x = x_ref[...]
    y = x * lax.rsqrt(jnp.mean(x * x, axis=-1, keepdims=True) + EPS) * g_ref[...]
    for o_ref in o_refs:
        o_ref[...] = y.astype(o_ref.dtype)


def _rmsnorm(x, g, dtypes, tm=640):
    m, d = x.shape
    return pl.pallas_call(
        _rmsnorm_kernel,
        grid=(m // tm,),
        in_specs=[pl.BlockSpec((tm, d), lambda i: (i, 0)),
                  pl.BlockSpec((1, d), lambda i: (0, 0))],
        out_specs=[pl.BlockSpec((tm, d), lambda i: (i, 0)) for _ in dtypes],
        out_shape=[jax.ShapeDtypeStruct((m, d), dt) for dt in dtypes],
        compiler_params=_cparams(1),
        name="rmsnorm",
    )(x, g.reshape(1, d))


ROW_TILE = 1024


def _rms(x, g):
    return x * lax.rsqrt(jnp.mean(x * x, axis=-1, keepdims=True) + EPS) * g


def _rmsnorm_stack_kernel(n_prompt_tiles, n_sample, xp_ref, xs_ref, g_ref, h_ref, x_ref):
    i = pl.program_id(0)

    @pl.when(i < n_prompt_tiles)
    def _():
        x = xp_ref[...]
        x_ref[...] = x
        h_ref[...] = _rms(x, g_ref[...]).astype(h_ref.dtype)

    @pl.when(i >= n_prompt_tiles)
    def _():
        x = xs_ref[...]
        x_ref[0:n_sample, :] = x
        h_ref[0:n_sample, :] = _rms(x, g_ref[...]).astype(h_ref.dtype)


def _rmsnorm_stack(x_prompt, x_sample, g):
    (mp, d), ms = x_prompt.shape, x_sample.shape[0]
    tm = ROW_TILE
    npt = mp // tm
    m = mp + ms
    return pl.pallas_call(
        functools.partial(_rmsnorm_stack_kernel, npt, ms),
        grid=(npt + 1,),
        in_specs=[pl.BlockSpec((tm, d), lambda i: (jnp.minimum(i, npt - 1), 0)),
                  pl.BlockSpec((ms, d), lambda i: (0, 0)),
                  pl.BlockSpec((1, d), lambda i: (0, 0))],
        out_specs=[pl.BlockSpec((tm, d), lambda i: (i, 0))] * 2,
        out_shape=[jax.ShapeDtypeStruct((m, d), BF16), jax.ShapeDtypeStruct((m, d), F32)],
        compiler_params=_cparams(1),
        name="rmsnorm_stack",
    )(x_prompt, x_sample, g.reshape(1, d))


def _mm_kernel(n_w, epilogue, has_res, a_ref, *refs):
    w_refs = refs[:n_w]
    res_ref = refs[n_w] if has_res else None
    o_ref = refs[n_w + has_res]
    wb_refs = refs[n_w + has_res + 1:]

    @pl.when(pl.program_id(1) == 0)
    def _():
        for w_ref, wb_ref in zip(w_refs, wb_refs):
            wb_ref[...] = w_ref[...].astype(BF16)

    sub = a_ref.shape[0] // MM_SUB_TILES
    for r in range(MM_SUB_TILES):
        rows = pl.ds(r * sub, sub)
        a = a_ref[rows, :]
        y = epilogue(*[jnp.dot(a, wb_ref[...], preferred_element_type=F32) for wb_ref in wb_refs])
        if has_res:
            y = res_ref[rows, :] + y
        o_ref[rows, :] = y.astype(o_ref.dtype)


def _matmul(a, ws, epilogue, out_dtype, name, res=None, layer=None):
    tm, tn = MM_TILES[name]
    m, k = a.shape
    n = ws[0].shape[-1]
    in_specs = [pl.BlockSpec((tm, k), lambda j, i: (i, 0))]
    if layer is None:
        in_specs += [pl.BlockSpec((k, tn), lambda j, i: (0, j)) for _ in ws]
    else:
        in_specs += [pl.BlockSpec((None, k, tn), lambda j, i: (layer, 0, j)) for _ in ws]
    args = [a, *ws]
    if res is not None:
        in_specs.append(pl.BlockSpec((tm, tn), lambda j, i: (i, j)))
        args.append(res)
    return pl.pallas_call(
        functools.partial(_mm_kernel, len(ws), epilogue, res is not None),
        grid=(n // tn, m // tm),
        in_specs=in_specs,
        out_specs=pl.BlockSpec((tm, tn), lambda j, i: (i, j)),
        out_shape=jax.ShapeDtypeStruct((m, n), out_dtype),
        scratch_shapes=[pltpu.VMEM((k, tn), BF16) for _ in ws],
        compiler_params=_cparams(2),
        name=name,
    )(*args)


def _epi_id(y):
    return y


def _epi_swiglu(g, u):
    return _silu(g) * u


def _epi_glu(a, b):
    return a * jax.nn.sigmoid(b)


def _rotary(x, cos, sin):
    x1 = x[:, :RET_HALF]
    x2 = x[:, RET_HALF:]
    return jnp.concatenate([x1 * cos - x2 * sin, x1 * sin + x2 * cos], axis=1)


def _group_norm_gate(o, gn, g):
    mu = jnp.mean(o, axis=-1, keepdims=True)
    d = o - mu
    var = jnp.mean(d * d, axis=-1, keepdims=True)
    return d * lax.rsqrt(var + EPS) * gn * _silu(g)


def _ret_prompt_kernel(q_ref, k_ref, v_ref, g_ref, cos_ref, sin_ref, intra_ref, qdec_ref, kdec_ref,
                       cdec_ref, gn_ref, mix_ref, o_ref, s_ref):
    del mix_ref

    @pl.when(pl.program_id(1) == 0)
    def _():
        s_ref[...] = jnp.zeros_like(s_ref)

    c = RET_CHUNK
    nch = RET_ROWS // c
    nh = RET_HEADS

    def split(ref, rotate):
        out = []
        for ci in range(nch):
            rows = pl.ds(ci * c, c)
            for k in range(nh):
                x = ref[rows, k * RET_DK:(k + 1) * RET_DK]
                out.append(_rotary(x, cos_ref[rows, :], sin_ref[rows, :]) if rotate else x)
        return jnp.stack(out)

    per_chunk = lambda t: jnp.concatenate([t] * nch, axis=0)
    q = split(q_ref, True)
    k = split(k_ref, True) * (RET_DK ** -0.5)
    v = split(v_ref, False).astype(BF16)
    att = jnp.einsum('bid,bjd->bij', q.astype(BF16), k.astype(BF16),
                     preferred_element_type=F32) * per_chunk(intra_ref[...])
    o = jnp.einsum('bij,bjv->biv', att.astype(BF16), v, preferred_element_type=F32)
    kv = jnp.einsum('bjd,bjv->bdv', (k * per_chunk(kdec_ref[...])).astype(BF16), v, preferred_element_type=F32)
    s = s_ref[0]
    cdec = cdec_ref[:, 0:1, :]
    starts = []
    for ci in range(nch):
        starts.append(s.astype(BF16))
        s = s * cdec + kv[ci * nh:(ci + 1) * nh]
    s_ref[0] = s
    o = o + jnp.einsum('bid,bdv->biv', (q * per_chunk(qdec_ref[...])).astype(BF16),
                       jnp.concatenate(starts, axis=0), preferred_element_type=F32)
    for ci in range(nch):
        rows = pl.ds(ci * c, c)
        for k in range(nh):
            cols = slice(k * RET_DK, (k + 1) * RET_DK)
            o_ref[rows, cols] = _group_norm_gate(o[ci * nh + k], gn_ref[:, cols], g_ref[rows, cols]).astype(o_ref.dtype)


def _ret_decay_tables(c):
    lg = np.log(1.0 - 2.0 ** (-5.0 - np.arange(RET_HEADS, dtype=np.float64)))
    idx = np.arange(c, dtype=np.float64)
    diff = idx[:, None] - idx[None, :]
    intra = np.where(diff >= 0, np.exp(np.maximum(diff, 0.0)[None] * lg[:, None, None]), 0.0)
    ones = np.ones((1, 1, RET_DK))
    qdec = np.exp((idx[None, :, None] + 1.0) * lg[:, None, None]) * ones
    kdec = np.exp((c - 1.0 - idx[None, :, None]) * lg[:, None, None]) * ones
    cdec = np.exp(c * lg)[:, None, None] * np.ones((1, 8, RET_DK))
    return [jnp.asarray(t, F32) for t in (intra, qdec, kdec, cdec)]


def _ret_prompt(proj, mix, cos, sin, gn, batch, seq):
    c = RET_CHUNK
    r = RET_ROWS
    nc = seq // r
    w = RET_HEADS * RET_DK
    tables = _ret_decay_tables(c)
    col = lambda j: pl.BlockSpec((r, w), lambda b, i: (b * nc + i, j))
    whole = lambda t: pl.BlockSpec(t.shape, lambda b, i: (0,) * t.ndim)
    return pl.pallas_call(
        _ret_prompt_kernel,
        grid=(batch, nc),
        in_specs=[col(0), col(1), col(2), col(3),
                  pl.BlockSpec((r, RET_HALF), lambda b, i: (i, 0)),
                  pl.BlockSpec((r, RET_HALF), lambda b, i: (i, 0)),
                  *[whole(t) for t in tables],
                  pl.BlockSpec((1, w), lambda b, i: (0, 0)),
                  pl.BlockSpec(memory_space=pl.ANY)],
        out_specs=[pl.BlockSpec((r, w), lambda b, i: (b * nc + i, 0)),
                   pl.BlockSpec((1, RET_HEADS, RET_DK, RET_DK), lambda b, i: (b, 0, 0, 0))],
        out_shape=[jax.ShapeDtypeStruct(mix.shape, mix.dtype),
                   jax.ShapeDtypeStruct((batch, RET_HEADS, RET_DK, RET_DK), F32)],
        input_output_aliases={11: 0},
        compiler_params=_cparams(2),
        name="ret_prompt",
    )(proj, proj, proj, proj, cos, sin, *tables, gn.reshape(1, MIX_HALF), mix)


def _columns(x, n):
    t = x.shape[0]
    parts = [jnp.concatenate([x[:, i:i + 128]] * (128 // t), axis=0).T for i in range(0, n, 128)]
    return parts[0] if len(parts) == 1 else jnp.concatenate(parts, axis=0)


def _ret_sample_kernel(q_ref, k_ref, v_ref, g_ref, cos_ref, sin_ref, cdec_ref, gn_ref, s_ref, mix_ref,
                       o_ref, so_ref):
    del mix_ref
    cos = cos_ref[...]
    sin = sin_ref[...]
    q = _rotary(q_ref[...], cos, sin)
    k = _rotary(k_ref[...], cos, sin) * (RET_DK ** -0.5)
    v = v_ref[...]
    kt = _columns(k, RET_DK)
    gamma = cdec_ref[0, 0:1, :]
    qb = q.astype(BF16)
    tok = lax.broadcasted_iota(jnp.int32, q.shape, 0)
    o = jnp.zeros(q.shape, F32)
    for t in range(SAMPLE_TOK):
        s_new = s_ref[t, 0] * gamma + kt[:, t:t + 1] * v[t:t + 1, :]
        so_ref[t, 0] = s_new
        o = jnp.where(tok == t, jnp.dot(qb, s_new.astype(BF16), preferred_element_type=F32), o)
    o_ref[...] = _group_norm_gate(o, gn_ref[...], g_ref[...]).astype(o_ref.dtype)


def _ret_sample(proj, mix, state, cos, sin, gn, row0):
    nb = state.shape[0]
    t = SAMPLE_TOK
    r0 = row0 // t
    _, _, _, cdec = _ret_decay_tables(1)
    col = lambda off: pl.BlockSpec((t, RET_DK), lambda b, h: (r0 + b, off + h))
    st = pl.BlockSpec((t, 1, RET_DK, RET_DK), lambda b, h: (b, h, 0, 0))
    return pl.pallas_call(
        _ret_sample_kernel,
        grid=(nb // t, RET_HEADS),
        in_specs=[col(0), col(RET_HEADS), col(2 * RET_HEADS), col(3 * RET_HEADS),
                  pl.BlockSpec((1, RET_HALF), lambda b, h: (0, 0)),
                  pl.BlockSpec((1, RET_HALF), lambda b, h: (0, 0)),
                  pl.BlockSpec((1, 8, RET_DK), lambda b, h: (h, 0, 0)),
                  pl.BlockSpec((1, RET_DK), lambda b, h: (0, h)),
                  st,
                  pl.BlockSpec(memory_space=pl.ANY)],
        out_specs=[pl.BlockSpec((t, RET_DK), lambda b, h: (r0 + b, h)), st],
        out_shape=[jax.ShapeDtypeStruct(mix.shape, mix.dtype),
                   jax.ShapeDtypeStruct(state.shape, F32)],
        input_output_aliases={9: 0},
        compiler_params=_cparams(2),
        name="ret_sample",
    )(proj, proj, proj, proj, cos, sin, cdec, gn.reshape(1, MIX_HALF), state, mix)


def _hg_lower_bound(lb_ref, layer):
    x = lb_ref[...]
    e = jnp.exp(x - jnp.max(x, axis=0, keepdims=True))
    return jnp.sum(e[:layer + 1], axis=0, keepdims=True) / jnp.sum(e, axis=0, keepdims=True)


def _hg_gates(gq, gf, lb):
    f = lb + (1.0 - lb) * jax.nn.sigmoid(gf)
    return _silu(gq), 1.0 - f, f


def _hg_out(o, gn, gg):
    return o * lax.rsqrt(jnp.mean(o * o, axis=-1, keepdims=True) + EPS) * gn * _silu(gg)


def _split3(x):
    hi = x.astype(BF16)
    r = x - hi.astype(F32)
    mid = r.astype(BF16)
    lo = (r - mid.astype(F32)).astype(BF16)
    return hi, mid, lo


def _hg_block(qq, kk, lf, v, st, tri):
    c = HG_CHUNK
    nch = qq.shape[0] // c
    nh = qq.shape[1] // HG_DK
    w = nh * HG_DK
    hi, mid, lo = _split3(lf)
    b3 = jnp.dot(tri, jnp.concatenate([hi, mid, lo], axis=1), preferred_element_type=F32)
    b = b3[:, :w] + b3[:, w:2 * w] + b3[:, 2 * w:]
    split = lambda t: jnp.stack([t[ci * c:(ci + 1) * c, k * HG_DK:(k + 1) * HG_DK]
                                 for ci in range(nch) for k in range(nh)])
    q3, k3, bc = split(qq), split(kk), split(b)
    vb = split(v.astype(BF16))
    nb = nch * nh
    ri = lax.broadcasted_iota(jnp.int32, (1, c, c), 1)
    cj = lax.broadcasted_iota(jnp.int32, (1, c, c), 2)
    mid = bc[:, c // 2 - 1:c // 2]
    off_mid = bc - mid

    def att_mid_referenced():
        q_m = (q3 * jnp.exp(off_mid)).astype(BF16)
        k_m = (k3 * jnp.exp(-off_mid)).astype(BF16)
        return jnp.where(ri >= cj, jnp.einsum('cid,cjd->cij', q_m, k_m, preferred_element_type=F32), 0.0)

    def att_any_decay():
        refs = [jnp.zeros_like(bc[:, 0:1])] + [bc[:, i0 - 1:i0] for i0 in range(HG_SUB, c, HG_SUB)]
        span = jnp.concatenate([jnp.broadcast_to(r, (nb, HG_SUB, HG_DK)) for r in refs], axis=1) - bc
        q_t = (q3 * jnp.exp(-span)).astype(BF16)
        row = lax.broadcasted_iota(jnp.int32, (1, c, HG_DK), 1)
        sub_row = lax.broadcasted_iota(jnp.int32, (1, HG_SUB, c), 1)
        sub_col = lax.broadcasted_iota(jnp.int32, (1, HG_SUB, c), 2)
        rows = []
        for i0 in range(0, c, HG_SUB):
            att_i = jnp.zeros((nb, HG_SUB, c), F32)
            if i0 > 0:
                r = refs[i0 // HG_SUB]
                k_t = jnp.where(row < i0, k3 * jnp.exp(jnp.minimum(r - bc, 0.0)), 0.0).astype(BF16)
                att_i = jnp.einsum('cid,cjd->cij', q_t[:, i0:i0 + HG_SUB], k_t, preferred_element_type=F32)
            q_i = q3[:, i0:i0 + HG_SUB]
            b_i = bc[:, i0:i0 + HG_SUB]
            for j in range(HG_SUB):
                jj = i0 + j
                p = q_i * (k3[:, jj:jj + 1] * jnp.exp(jnp.minimum(b_i - bc[:, jj:jj + 1], 0.0)))
                s_j = jnp.sum(p, axis=2, keepdims=True)
                att_i = jnp.where((sub_col == jj) & (sub_row >= j), s_j, att_i)
            rows.append(att_i)
        return jnp.concatenate(rows, axis=1)

    att = lax.cond(jnp.max(jnp.abs(off_mid)) < HG_SAFE_SPAN, att_mid_referenced, att_any_decay).astype(BF16)
    o_intra = jnp.einsum('cij,cjv->civ', att, vb, preferred_element_type=F32)
    b_last = bc[:, c - 1:c]
    qe = (q3 * jnp.exp(bc)).astype(BF16)
    khat = (k3 * jnp.exp(b_last - bc)).astype(BF16)
    dec = jnp.exp(b_last)
    upd = jnp.einsum('cjv,cjd->cvd', vb, khat, preferred_element_type=F32)
    starts = []
    for ci in range(nch):
        heads = slice(ci * nh, (ci + 1) * nh)
        starts.append(st.astype(BF16))
        st = st * dec[heads] + upd[heads]
    o = o_intra + jnp.einsum('cid,cvd->civ', qe, jnp.concatenate(starts, axis=0), preferred_element_type=F32)
    o = jnp.concatenate([jnp.concatenate([o[ci * nh + k] for k in range(nh)], axis=1) for ci in range(nch)], axis=0)
    return o, st


def _hg_prompt_kernel(layer, gq_ref, gf_ref, gi_ref, gg_ref, lb_ref, gn_ref, tri_ref, mix_ref,
                      o_ref, s_ref, st_ref):
    del mix_ref

    @pl.when(pl.program_id(2) == 0)
    def _():
        st_ref[...] = jnp.zeros_like(st_ref)

    qq, kk, f = _hg_gates(gq_ref[...], gf_ref[...], _hg_lower_bound(lb_ref, layer))
    o, st = _hg_block(qq, kk, jnp.log(f), gi_ref[...], st_ref[...], tri_ref[...])
    for k in range(HG_NH):
        cols = slice(k * HG_DK, (k + 1) * HG_DK)
        o_ref[:, cols] = _hg_out(o[:, cols], gn_ref[:, cols], gg_ref[:, cols]).astype(o_ref.dtype)
    st_ref[...] = st

    @pl.when(pl.program_id(2) == pl.num_programs(2) - 1)
    def _():
        for k in range(HG_NH):
            s_ref[0, k] = st[k].T


def _hg_prompt(proj, mix, lb_raw, gn, layer, batch, seq):
    nt = seq // HG_ROWS
    c0 = MIX_HALF * 4 // HG_DK
    nh = HG_NH
    w = nh * HG_DK
    c0 = c0 // nh
    col = lambda off: pl.BlockSpec((HG_ROWS, w), lambda b, h, i: (b * nt + i, c0 + off // nh + h))
    tri = jnp.asarray(np.kron(np.eye(HG_ROWS // HG_CHUNK), np.tril(np.ones((HG_CHUNK, HG_CHUNK)))), BF16)
    return pl.pallas_call(
        functools.partial(_hg_prompt_kernel, layer),
        grid=(batch, HG_HEADS // nh, nt),
        in_specs=[col(0), col(HG_HEADS), col(2 * HG_HEADS), col(3 * HG_HEADS),
                  pl.BlockSpec((lb_raw.shape[0], w), lambda b, h, i: (0, h)),
                  pl.BlockSpec((1, w), lambda b, h, i: (0, h)),
                  pl.BlockSpec((HG_ROWS, HG_ROWS), lambda b, h, i: (0, 0)),
                  pl.BlockSpec(memory_space=pl.ANY)],
        out_specs=[pl.BlockSpec((HG_ROWS, w), lambda b, h, i: (b * nt + i, HG_HEADS // nh + h)),
                   pl.BlockSpec((1, nh, HG_DK, HG_DK), lambda b, h, i: (b, h, 0, 0))],
        out_shape=[jax.ShapeDtypeStruct(mix.shape, mix.dtype),
                   jax.ShapeDtypeStruct((batch, HG_HEADS, HG_DK, HG_DK), F32)],
        scratch_shapes=[pltpu.VMEM((nh, HG_DK, HG_DK), F32)],
        input_output_aliases={7: 0},
        compiler_params=_cparams(3),
        name="hgrn_prompt",
    )(proj, proj, proj, proj, lb_raw, gn.reshape(1, MIX_HALF), tri, mix)


HG_SAMPLE_NH = 4


def _hg_sample_kernel(layer, gq_ref, gf_ref, gi_ref, gg_ref, lb_ref, gn_ref, s_ref, mix_ref,
                      o_ref, so_ref):
    del mix_ref
    qq_all, kk_all, f_all = _hg_gates(gq_ref[...], gf_ref[...], _hg_lower_bound(lb_ref, layer))
    tok = lax.broadcasted_iota(jnp.int32, (SAMPLE_TOK, HG_DK), 0)
    for k in range(HG_SAMPLE_NH):
        cols = slice(k * HG_DK, (k + 1) * HG_DK)
        v = gi_ref[:, cols]
        kt = _columns(kk_all[:, cols], HG_DK)
        ft = _columns(f_all[:, cols], HG_DK)
        qb = qq_all[:, cols].astype(BF16)
        o = jnp.zeros((SAMPLE_TOK, HG_DK), F32)
        for t in range(SAMPLE_TOK):
            s_new = s_ref[t, k] * ft[:, t:t + 1] + kt[:, t:t + 1] * v[t:t + 1, :]
            so_ref[t, k] = s_new
            o = jnp.where(tok == t, jnp.dot(qb, s_new.astype(BF16), preferred_element_type=F32), o)
        o_ref[:, cols] = _hg_out(o, gn_ref[:, cols], gg_ref[:, cols]).astype(o_ref.dtype)


def _hg_sample(proj, mix, state, lb_raw, gn, layer, row0):
    nb = state.shape[0]
    t = SAMPLE_TOK
    nh = HG_SAMPLE_NH
    w = nh * HG_DK
    r0 = row0 // t
    c0 = MIX_HALF * 4 // w
    col = lambda off: pl.BlockSpec((t, w), lambda b, h: (r0 + b, c0 + off // nh + h))
    st = pl.BlockSpec((t, nh, HG_DK, HG_DK), lambda b, h: (b, h, 0, 0))
    return pl.pallas_call(
        functools.partial(_hg_sample_kernel, layer),
        grid=(nb // t, HG_HEADS // nh),
        in_specs=[col(0), col(HG_HEADS), col(2 * HG_HEADS), col(3 * HG_HEADS),
                  pl.BlockSpec((lb_raw.shape[0], w), lambda b, h: (0, h)),
                  pl.BlockSpec((1, w), lambda b, h: (0, h)),
                  st,
                  pl.BlockSpec(memory_space=pl.ANY)],
        out_specs=[pl.BlockSpec((t, w), lambda b, h: (r0 + b, HG_HEADS // nh + h)), st],
        out_shape=[jax.ShapeDtypeStruct(mix.shape, mix.dtype),
                   jax.ShapeDtypeStruct(state.shape, F32)],
        input_output_aliases={7: 0},
        compiler_params=_cparams(2),
        name="hgrn_sample",
    )(proj, proj, proj, proj, lb_raw, gn.reshape(1, MIX_HALF), state, mix)


def _cmul(ar, ai, br, bi):
    return ar * br - ai * bi, ar * bi + ai * br


def _s5_prep_kernel(lr_ref, li_ref, ldt_ref, brt_ref, bit_ref, ar_ref, ai_ref, bbr_ref, bbi_ref):
    lr = lr_ref[...]
    li = li_ref[...]
    dt = jnp.exp(ldt_ref[...])
    mag = jnp.exp(lr * dt)
    ar = mag * jnp.cos(li * dt)
    ai = mag * jnp.sin(li * dt)
    den = lr * lr + li * li
    cr = ((ar - 1.0) * lr + ai * li) / den
    ci = (ai * lr - (ar - 1.0) * li) / den
    brt = brt_ref[...]
    bit = bit_ref[...]
    bbr_ref[...] = cr * brt - ci * bit
    bbi_ref[...] = cr * bit + ci * brt
    ar_ref[...] = ar
    ai_ref[...] = ai


def _s5_prep(lam_re, lam_im, log_dt, b_re, b_im):
    n = S5_GROUPS * S5_STATE
    flat = lambda t: t.reshape(1, n)
    b_t = lambda t: jnp.transpose(t, (2, 0, 1)).reshape(S5_GROUP, n)
    ldt = jnp.repeat(log_dt, S5_STATE).reshape(1, n)
    return pl.pallas_call(
        _s5_prep_kernel,
        out_shape=[jax.ShapeDtypeStruct((1, n), F32)] * 2 + [jax.ShapeDtypeStruct((S5_GROUP, n), F32)] * 2,
        compiler_params=pltpu.CompilerParams(vmem_limit_bytes=VMEM_LIMIT_BYTES),
        name="s5_prep",
    )(flat(lam_re), flat(lam_im), ldt, b_t(b_re), b_t(b_im))


S5_T = 256
S5_HALF_ST = S5_BLK_ST // 2


def _s5_tb_mats(bbr, bbi, c_re, c_im, a_r, a_i, batch):
    hg = S5_BLK_GROUPS // 2
    eye = jnp.eye(hg, dtype=F32)
    def b_blk(t):
        t = t.reshape(S5_GROUP, S5_NBLK, 2, hg, S5_STATE)
        return jnp.einsum('cbhgp,kg->bhgckp', t, eye)
    def c_blk(t):
        t = t.reshape(S5_NBLK, 2, hg, S5_GROUP, S5_STATE)
        return jnp.einsum('bhgcp,kg->bkphgc', t, eye)
    b_m = jnp.stack([b_blk(bbr), b_blk(bbi)], axis=4)
    c_m = jnp.stack([c_blk(c_re), c_blk(-c_im)], axis=1)
    tile_a = lambda a: jnp.broadcast_to(a.reshape(S5_NBLK, 2, 1, S5_HALF_ST),
                                        (S5_NBLK, 2, batch, S5_HALF_ST)).reshape(S5_NBLK, 2 * batch, S5_HALF_ST)
    return (b_m.reshape(S5_NBLK, S5_BLK_CH, S5_BLK_ST).astype(BF16),
            c_m.reshape(S5_NBLK, S5_BLK_ST, S5_BLK_CH).astype(BF16), tile_a(a_r), tile_a(a_i))


S5_NB = 2


def _s5_tb_kernel(u_ref, b_ref, c_ref, ar_ref, ai_ref, d_ref, z_in_ref, z_ref, hr_out, hi_out,
                  x_ref, hb_ref, hr_ref, hi_ref):
    del z_in_ref
    i = pl.program_id(1)
    n = S5_HALF_ST
    ch = S5_BLK_CH
    t_steps = u_ref.shape[0] // 4
    half_rows = 4 * t_steps

    @pl.when(i == 0)
    def _():
        hr_ref[...] = jnp.zeros_like(hr_ref)
        hi_ref[...] = jnp.zeros_like(hi_ref)

    first4 = lax.broadcasted_iota(jnp.int32, (1, 8, ch), 1) < 4
    half0 = lax.broadcasted_iota(jnp.int32, (1, 8, ch), 2) < ch // 2
    for k in range(S5_NB):
        u3 = u_ref[:, k * ch:(k + 1) * ch].reshape(t_steps // 2, 8, ch)
        r3 = pltpu.roll(u3, 4, 1)
        even = jnp.where(first4, jnp.where(half0, u3, 0.0), jnp.where(half0, 0.0, r3))
        odd = jnp.where(first4, jnp.where(half0, r3, 0.0), jnp.where(half0, 0.0, u3))
        lhs = jnp.stack([even, odd], axis=1).reshape(8 * t_steps, ch).astype(BF16)
        for r0 in (0, half_rows):
            x_ref[k, r0:r0 + half_rows, :] = jnp.dot(lhs[r0:r0 + half_rows], b_ref[k],
                                                     preferred_element_type=F32)
    a = [(ar_ref[k], ai_ref[k]) for k in range(S5_NB)]

    def advance(t, hs):
        rows = pl.ds(pl.multiple_of(t * 8, 8), 8)
        out = []
        for k in range(S5_NB):
            dr, di = _cmul(*a[k], *hs[k])
            out.append((dr + x_ref[k, rows, 0:n], di + x_ref[k, rows, n:2 * n]))
        return tuple(out)

    def emit(t2, hs):
        h1 = advance(2 * t2, hs)
        h2 = advance(2 * t2 + 1, h1)
        rows = pl.ds(pl.multiple_of(t2 * 16, 16), 16)
        for k in range(S5_NB):
            hb_ref[k, rows, 0:n] = jnp.concatenate([h1[k][0], h2[k][0]], axis=0).astype(BF16)
            hb_ref[k, rows, n:2 * n] = jnp.concatenate([h1[k][1], h2[k][1]], axis=0).astype(BF16)
        return h2

    hs = lax.fori_loop(0, t_steps // 2, emit, tuple((hr_ref[k], hi_ref[k]) for k in range(S5_NB)), unroll=2)
    for k in range(S5_NB):
        hr_ref[k] = hs[k][0]
        hi_ref[k] = hs[k][1]

    pick = lambda yv: jnp.where(half0, yv, pltpu.roll(yv, 4, 1))
    for k in range(S5_NB):
        y2 = jnp.concatenate([jnp.dot(hb_ref[k, r0:r0 + half_rows, :], c_ref[k], preferred_element_type=F32)
                              for r0 in (0, half_rows)], axis=0)
        y4 = y2.reshape(t_steps // 2, 2, 8, ch)
        y = jnp.where(first4, pick(y4[:, 0]), pltpu.roll(pick(y4[:, 1]), 4, 1)).reshape(4 * t_steps, ch)
        cols = slice(k * ch, (k + 1) * ch)
        z_ref[:, cols] = jax.nn.gelu(y + d_ref[:, cols] * u_ref[:, cols]).astype(z_ref.dtype)

    @pl.when(i == pl.num_programs(1) - 1)
    def _():
        for k in range(S5_NB):
            for h in range(2):
                cols = slice((2 * k + h) * n, (2 * k + h + 1) * n)
                hr_out[:, cols] = hs[k][0][4 * h:4 * h + 4, :]
                hi_out[:, cols] = hs[k][1][4 * h:4 * h + 4, :]


def _s5_prompt_tb(u, z, mats, d, batch, seq):
    assert batch == 4
    b_m, c_m, a_r, a_i = mats
    rows = batch * S5_T
    nt = seq // S5_T
    nb = S5_NB
    blk3 = lambda r, c: pl.BlockSpec((nb, r, c), lambda j, i: (j, 0, 0))
    st_spec = pl.BlockSpec((batch, nb * S5_BLK_ST), lambda j, i: (0, j))
    st_shape = jax.ShapeDtypeStruct((batch, S5_GROUPS * S5_STATE), F32)
    return pl.pallas_call(
        _s5_tb_kernel,
        grid=(S5_NBLK // nb, nt),
        in_specs=[pl.BlockSpec((rows, nb * S5_BLK_CH), lambda j, i: (i, j)),
                  blk3(S5_BLK_CH, S5_BLK_ST), blk3(S5_BLK_ST, S5_BLK_CH),
                  blk3(2 * batch, S5_HALF_ST), blk3(2 * batch, S5_HALF_ST),
                  pl.BlockSpec((1, nb * S5_BLK_CH), lambda j, i: (0, j)),
                  pl.BlockSpec(memory_space=pl.ANY)],
        out_specs=[pl.BlockSpec((rows, nb * S5_BLK_CH), lambda j, i: (i, j)), st_spec, st_spec],
        out_shape=[jax.ShapeDtypeStruct(z.shape, z.dtype), st_shape, st_shape],
        scratch_shapes=[pltpu.VMEM((nb, 2 * rows, S5_BLK_ST), F32), pltpu.VMEM((nb, 2 * rows, S5_BLK_ST), BF16),
                        pltpu.VMEM((nb, 2 * batch, S5_HALF_ST), F32), pltpu.VMEM((nb, 2 * batch, S5_HALF_ST), F32)],
        input_output_aliases={6: 0},
        compiler_params=_cparams(2),
        name="s5_prompt",
    )(u, b_m, c_m, a_r, a_i, d.reshape(1, D_MODEL), z)


def _s5_sample_kernel(u_ref, b_ref, c_ref, ar_ref, ai_ref, d_ref, h0r_ref, h0i_ref,
                      z_in_ref, z_ref, hr_out, hi_out):
    del z_in_ref
    n = S5_HALF_ST
    u = u_ref[...]
    nb = u.shape[0]
    half0 = lax.broadcasted_iota(jnp.int32, u.shape, 1) < S5_BLK_CH // 2
    lhs = jnp.concatenate([jnp.where(half0, u, 0.0), jnp.where(half0, 0.0, u)], axis=0).astype(BF16)
    x = jnp.dot(lhs, b_ref[0], preferred_element_type=F32)
    halves = []
    for h in range(2):
        rows = slice(h * nb, (h + 1) * nb)
        cols = slice(h * n, (h + 1) * n)
        row_a = slice(h * (ar_ref.shape[1] // 2), h * (ar_ref.shape[1] // 2) + 1)
        dr, di = _cmul(ar_ref[0, row_a, :], ai_ref[0, row_a, :], h0r_ref[:, cols], h0i_ref[:, cols])
        hr = x[rows, 0:n] + dr
        hi = x[rows, n:2 * n] + di
        hr_out[:, cols] = hr
        hi_out[:, cols] = hi
        halves.append(jnp.concatenate([hr, hi], axis=1))
    y2 = jnp.dot(jnp.concatenate(halves, axis=0).astype(BF16), c_ref[0], preferred_element_type=F32)
    y = jnp.where(half0, y2[0:nb], y2[nb:2 * nb])
    z_ref[...] = jax.nn.gelu(y + d_ref[...] * u).astype(z_ref.dtype)


def _s5_sample(u, z, mats, d, h0r, h0i, row0):
    b_m, c_m, a_r, a_i = mats
    nb = h0r.shape[0]
    r0 = row0 // nb
    blk3 = lambda r, c: pl.BlockSpec((1, r, c), lambda j: (j, 0, 0))
    st_spec = pl.BlockSpec((nb, S5_BLK_ST), lambda j: (0, j))
    st_shape = jax.ShapeDtypeStruct(h0r.shape, F32)
    return pl.pallas_call(
        _s5_sample_kernel,
        grid=(S5_NBLK,),
        in_specs=[pl.BlockSpec((nb, S5_BLK_CH), lambda j: (r0, j)),
                  blk3(S5_BLK_CH, S5_BLK_ST), blk3(S5_BLK_ST, S5_BLK_CH),
                  blk3(a_r.shape[1], S5_HALF_ST), blk3(a_r.shape[1], S5_HALF_ST),
                  pl.BlockSpec((1, S5_BLK_CH), lambda j: (0, j)),
                  st_spec, st_spec,
                  pl.BlockSpec(memory_space=pl.ANY)],
        out_specs=[pl.BlockSpec((nb, S5_BLK_CH), lambda j: (r0, j)), st_spec, st_spec],
        out_shape=[jax.ShapeDtypeStruct(z.shape, z.dtype), st_shape, st_shape],
        input_output_aliases={8: 0},
        compiler_params=_cparams(1),
        name="s5_sample",
    )(u, b_m, c_m, a_r, a_i, d.reshape(1, D_MODEL), h0r, h0i, z)


TB_T = 128
LANES = 128


def _interleave_norm_kernel(n_steps, n_sample, batch, *refs):
    x_refs = refs[:batch]
    xs_ref, g_ref, xt_ref, ut_ref, slab_ref = refs[batch:]
    i = pl.program_id(0)
    d = xt_ref.shape[1]

    @pl.when(i < n_steps)
    def _():
        for dst_ref, norm in ((xt_ref, False), (ut_ref, True)):
            for b, x_ref in enumerate(x_refs):
                v = x_ref[...]
                if norm:
                    v = _rms(v, g_ref[...])
                for s in range(d // LANES):
                    slab_ref[s, pl.ds(b, TB_T, stride=batch), :] = v[:, s * LANES:(s + 1) * LANES]
            for s in range(d // LANES):
                dst_ref[:, s * LANES:(s + 1) * LANES] = slab_ref[s]

    @pl.when(i >= n_steps)
    def _():
        v = xs_ref[...]
        xt_ref[0:n_sample, :] = v
        ut_ref[0:n_sample, :] = _rms(v, g_ref[...])


def _interleave_norm(x, g, batch, seq):
    m, d = x.shape
    n_steps = seq // TB_T
    ms = m - batch * seq
    rows = batch * TB_T
    per_batch = lambda b: pl.BlockSpec((TB_T, d), lambda i: (b * n_steps + jnp.minimum(i, n_steps - 1), 0))
    out = pl.BlockSpec((rows, d), lambda i: (i, 0))
    return pl.pallas_call(
        functools.partial(_interleave_norm_kernel, n_steps, ms, batch),
        grid=(n_steps + 1,),
        in_specs=[per_batch(b) for b in range(batch)]
        + [pl.BlockSpec((ms, d), lambda i: (batch * seq // ms, 0)), pl.BlockSpec((1, d), lambda i: (0, 0))],
        out_specs=[out, out],
        out_shape=[jax.ShapeDtypeStruct((m, d), F32)] * 2,
        scratch_shapes=[pltpu.VMEM((d // LANES, rows, LANES), F32)],
        compiler_params=_cparams(1),
        name="interleave_norm",
    )(*([x] * (batch + 1)), g.reshape(1, d))


def _deinterleave_norm_kernel(n_steps, n_sample, batch, x_ref, g_ref, y_ref, ys_ref, slab_ref):
    i = pl.program_id(0)
    d = x_ref.shape[1]

    @pl.when(i < n_steps)
    def _():
        y = _rms(x_ref[...], g_ref[...])
        for s in range(d // LANES):
            slab_ref[s] = y[:, s * LANES:(s + 1) * LANES]
        for b in range(batch):
            for s in range(d // LANES):
                y_ref[b, :, s * LANES:(s + 1) * LANES] = slab_ref[s, pl.ds(b, TB_T, stride=batch), :]

    @pl.when(i >= n_steps)
    def _():
        ys_ref[...] = _rms(x_ref[0:n_sample, :], g_ref[...])


def _deinterleave_norm(x, g, batch, seq):
    m, d = x.shape
    n_steps = seq // TB_T
    ms = m - batch * seq
    rows = batch * TB_T
    return pl.pallas_call(
        functools.partial(_deinterleave_norm_kernel, n_steps, ms, batch),
        grid=(n_steps + 1,),
        in_specs=[pl.BlockSpec((rows, d), lambda i: (i, 0)), pl.BlockSpec((1, d), lambda i: (0, 0))],
        out_specs=[pl.BlockSpec((batch, TB_T, d), lambda i: (0, jnp.minimum(i, n_steps - 1), 0)),
                   pl.BlockSpec((ms, d), lambda i: (0, 0))],
        out_shape=[jax.ShapeDtypeStruct((batch, seq, d), F32), jax.ShapeDtypeStruct((ms, d), F32)],
        scratch_shapes=[pltpu.VMEM((d // LANES, rows, LANES), F32)],
        compiler_params=_cparams(1),
        name="deinterleave_norm",
    )(x, g.reshape(1, d))


def _rope_tables(pos):
    inv = ROPE_BASE ** (-jnp.arange(RET_HALF, dtype=F32) / RET_HALF)
    ang = pos[:, None] * inv[None, :]
    return jnp.cos(ang), jnp.sin(ang)


def _ffn(x, norm_g, wg, wu, wd, layer):
    h = _rmsnorm(x, norm_g[layer], [BF16])[0]
    t = _matmul(h, [wg, wu], _epi_swiglu, BF16, "ffn_gate_up", layer=layer)
    return _matmul(t, [wd], _epi_id, F32, "ffn_down", res=x, layer=layer)


def kernel(x_prompt, x_sample, state_ret, state_hgrn, state_s5_re, state_s5_im, attn_norm_g, w_in, ret_gn_g, hg_lb, hg_gn_g, w_out, ssm_norm_g, s5_lam_re, s5_lam_im, s5_log_dt, s5_b_re, s5_b_im, s5_c_re, s5_c_im, s5_d, w_glu_a, w_glu_b, ffn_norm_g, w_ffn_gate, w_ffn_up, w_ffn_down, final_norm_g):
    one = lambda t: t.reshape(t.shape[1:])
    (state_ret, state_hgrn, state_s5_re, state_s5_im, attn_norm_g, w_in, ret_gn_g, hg_gn_g, w_out, ssm_norm_g,
     s5_lam_re, s5_lam_im, s5_log_dt, s5_b_re, s5_b_im, s5_c_re, s5_c_im, s5_d, w_glu_a, w_glu_b) = map(one, (
         state_ret, state_hgrn, state_s5_re, state_s5_im, attn_norm_g, w_in, ret_gn_g, hg_gn_g, w_out, ssm_norm_g,
         s5_lam_re, s5_lam_im, s5_log_dt, s5_b_re, s5_b_im, s5_c_re, s5_c_im, s5_d, w_glu_a, w_glu_b))
    cos_p, sin_p = _rope_tables(jnp.arange(SEQ, dtype=F32))
    cos_s, sin_s = _rope_tables(jnp.full((1,), float(PAST_LEN), F32))

    h, x = _rmsnorm_stack(x_prompt.reshape(N_PROMPT, D_MODEL), x_sample.reshape(DEC_BATCH, D_MODEL), attn_norm_g)
    proj = _matmul(h, [w_in], _epi_id, F32, "w_in")
    mix = jnp.zeros((N_ROWS, D_MODEL), BF16)
    mix, ret_p = _ret_prompt(proj, mix, cos_p, sin_p, ret_gn_g, BATCH, SEQ)
    mix, hg_p = _hg_prompt(proj, mix, hg_lb, hg_gn_g, 0, BATCH, SEQ)
    mix, ret_s = _ret_sample(proj, mix, state_ret, cos_s, sin_s, ret_gn_g, N_PROMPT)
    mix, hg_s = _hg_sample(proj, mix, state_hgrn, hg_lb, hg_gn_g, 0, N_PROMPT)
    x = _matmul(mix, [w_out], _epi_id, F32, "w_out", res=x)
    x = _ffn(x, ffn_norm_g, w_ffn_gate, w_ffn_up, w_ffn_down, 0)

    x, u = _interleave_norm(x, ssm_norm_g, BATCH, SEQ)
    p_r, p_i, bbr, bbi = _s5_prep(s5_lam_re, s5_lam_im, s5_log_dt, s5_b_re, s5_b_im)
    mats = _s5_tb_mats(bbr, bbi, s5_c_re, s5_c_im, p_r[0], p_i[0], BATCH)
    z = jnp.zeros((N_ROWS, D_MODEL), BF16)
    z, s5r_p, s5i_p = _s5_prompt_tb(u, z, mats, s5_d, BATCH, SEQ)
    n_st = S5_GROUPS * S5_STATE
    z, s5r_s, s5i_s = _s5_sample(u, z, mats, s5_d, state_s5_re.reshape(DEC_BATCH, n_st),
                                 state_s5_im.reshape(DEC_BATCH, n_st), N_PROMPT)
    x = _matmul(z, [w_glu_a, w_glu_b], _epi_glu, F32, "glu", res=x)
    x = _ffn(x, ffn_norm_g, w_ffn_gate, w_ffn_up, w_ffn_down, 1)

    y_p, y_s = _deinterleave_norm(x, final_norm_g, BATCH, SEQ)
    st = lambda t, b: t.reshape(1, b, S5_GROUPS, S5_STATE)
    return (y_p, y_s.reshape(DEC_BATCH, 1, D_MODEL),
            ret_p[None], ret_s[None], hg_p[None], hg_s[None],
            st(s5r_p, BATCH), st(s5i_p, BATCH), st(s5r_s, DEC_BATCH), st(s5i_s, DEC_BATCH))
```

```python
import functools

import numpy as np
import jax
import jax.numpy as jnp
from jax import lax
from jax.experimental import pallas as pl
from jax.experimental.pallas import tpu as pltpu

F32 = jnp.float32
BF16 = jnp.bfloat16

D_MODEL = 2048
BATCH = 4
SEQ = 2048
DEC_BATCH = 128
PAST_LEN = 16384
N_PROMPT = BATCH * SEQ
N_ROWS = N_PROMPT + DEC_BATCH
MIX_HALF = D_MODEL // 2
RET_HEADS = 4
RET_DK = MIX_HALF // RET_HEADS
RET_HALF = RET_DK // 2
RET_CHUNK = 128
RET_ROWS = 512
HG_HEADS = 8
HG_DK = MIX_HALF // HG_HEADS
HG_CHUNK = 64
HG_SUB = 16
HG_ROWS = 256
HG_NH = 8
HG_SAFE_SPAN = 40.0
S5_GROUP = 16
S5_GROUPS = D_MODEL // S5_GROUP
S5_STATE = 64
S5_BLK_GROUPS = 16
S5_BLK_CH = S5_BLK_GROUPS * S5_GROUP
S5_BLK_ST = S5_BLK_GROUPS * S5_STATE
S5_NBLK = S5_GROUPS // S5_BLK_GROUPS
ROPE_BASE = 10000.0
EPS = 1e-6
SAMPLE_TOK = 16

VMEM_LIMIT_BYTES = 56 * 1024 * 1024

MM_TILES = {"w_in": (1664, 1024), "w_out": (1664, 512), "ffn_gate_up": (1664, 512),
            "ffn_down": (640, 512), "glu": (640, 512)}
MM_SUB_TILES = 2


def _cparams(n_axes):
    return pltpu.CompilerParams(dimension_semantics=("arbitrary",) * n_axes,
                                vmem_limit_bytes=VMEM_LIMIT_BYTES)


def _silu(x):
    return x * jax.nn.sigmoid(x)


def _rmsnorm_kernel(x_ref, g_ref, *o_refs):
    x = x_ref[...]
    y = x * lax.rsqrt(jnp.mean(x * x, axis=-1, keepdims=True) + EPS) * g_ref[...]
    for o_ref in o_refs:
        o_ref[...] = y.astype(o_ref.dtype)


def _rmsnorm(x, g, dtypes, tm=640):
    m, d = x.shape
    return pl.pallas_call(
        _rmsnorm_kernel,
        grid=(m // tm,),
        in_specs=[pl.BlockSpec((tm, d), lambda i: (i, 0)),
                  pl.BlockSpec((1, d), lambda i: (0, 0))],
        out_specs=[pl.BlockSpec((tm, d), lambda i: (i, 0)) for _ in dtypes],
        out_shape=[jax.ShapeDtypeStruct((m, d), dt) for dt in dtypes],
        compiler_params=_cparams(1),
        name="rmsnorm",
    )(x, g.reshape(1, d))


ROW_TILE = 1024


def _rms(x, g):
    return x * lax.rsqrt(jnp.mean(x * x, axis=-1, keepdims=True) + EPS) * g


def _rmsnorm_stack_kernel(n_prompt_tiles, n_sample, xp_ref, xs_ref, g_ref, h_ref, x_ref):
    i = pl.program_id(0)

    @pl.when(i < n_prompt_tiles)
    def _():
        x = xp_ref[...]
        x_ref[...] = x
        h_ref[...] = _rms(x, g_ref[...]).astype(h_ref.dtype)

    @pl.when(i >= n_prompt_tiles)
    def _():
        x = xs_ref[...]
        x_ref[0:n_sample, :] = x
        h_ref[0:n_sample, :] = _rms(x, g_ref[...]).astype(h_ref.dtype)


def _rmsnorm_stack(x_prompt, x_sample, g):
    (mp, d), ms = x_prompt.shape, x_sample.shape[0]
    tm = ROW_TILE
    npt = mp // tm
    m = mp + ms
    return pl.pallas_call(
        functools.partial(_rmsnorm_stack_kernel, npt, ms),
        grid=(npt + 1,),
        in_specs=[pl.BlockSpec((tm, d), lambda i: (jnp.minimum(i, npt - 1), 0)),
                  pl.BlockSpec((ms, d), lambda i: (0, 0)),
                  pl.BlockSpec((1, d), lambda i: (0, 0))],
        out_specs=[pl.BlockSpec((tm, d), lambda i: (i, 0))] * 2,
        out_shape=[jax.ShapeDtypeStruct((m, d), BF16), jax.ShapeDtypeStruct((m, d), F32)],
        compiler_params=_cparams(1),
        name="rmsnorm_stack",
    )(x_prompt, x_sample, g.reshape(1, d))


def _mm_kernel(n_w, epilogue, has_res, a_ref, *refs):
    w_refs = refs[:n_w]
    res_ref = refs[n_w] if has_res else None
    o_ref = refs[n_w + has_res]
    wb_refs = refs[n_w + has_res + 1:]

    @pl.when(pl.program_id(1) == 0)
    def _():
        for w_ref, wb_ref in zip(w_refs, wb_refs):
            wb_ref[...] = w_ref[...].astype(BF16)

    sub = a_ref.shape[0] // MM_SUB_TILES
    for r in range(MM_SUB_TILES):
        rows = pl.ds(r * sub, sub)
        a = a_ref[rows, :]
        y = epilogue(*[jnp.dot(a, wb_ref[...], preferred_element_type=F32) for wb_ref in wb_refs])
        if has_res:
            y = res_ref[rows, :] + y
        o_ref[rows, :] = y.astype(o_ref.dtype)


def _matmul(a, ws, epilogue, out_dtype, name, res=None, layer=None):
    tm, tn = MM_TILES[name]
    m, k = a.shape
    n = ws[0].shape[-1]
    in_specs = [pl.BlockSpec((tm, k), lambda j, i: (i, 0))]
    if layer is None:
        in_specs += [pl.BlockSpec((k, tn), lambda j, i: (0, j)) for _ in ws]
    else:
        in_specs += [pl.BlockSpec((None, k, tn), lambda j, i: (layer, 0, j)) for _ in ws]
    args = [a, *ws]
    if res is not None:
        in_specs.append(pl.BlockSpec((tm, tn), lambda j, i: (i, j)))
        args.append(res)
    return pl.pallas_call(
        functools.partial(_mm_kernel, len(ws), epilogue, res is not None),
        grid=(n // tn, m // tm),
        in_specs=in_specs,
        out_specs=pl.BlockSpec((tm, tn), lambda j, i: (i, j)),
        out_shape=jax.ShapeDtypeStruct((m, n), out_dtype),
        scratch_shapes=[pltpu.VMEM((k, tn), BF16) for _ in ws],
        compiler_params=_cparams(2),
        name=name,
    )(*args)


def _epi_id(y):
    return y


def _epi_swiglu(g, u):
    return _silu(g) * u


def _epi_glu(a, b):
    return a * jax.nn.sigmoid(b)


def _rotary(x, cos, sin):
    x1 = x[:, :RET_HALF]
    x2 = x[:, RET_HALF:]
    return jnp.concatenate([x1 * cos - x2 * sin, x1 * sin + x2 * cos], axis=1)


def _group_norm_gate(o, gn, g):
    mu = jnp.mean(o, axis=-1, keepdims=True)
    d = o - mu
    var = jnp.mean(d * d, axis=-1, keepdims=True)
    return d * lax.rsqrt(var + EPS) * gn * _silu(g)


def _ret_prompt_kernel(q_ref, k_ref, v_ref, g_ref, cos_ref, sin_ref, intra_ref, qdec_ref, kdec_ref,
                       cdec_ref, gn_ref, mix_ref, o_ref, s_ref):
    del mix_ref

    @pl.when(pl.program_id(1) == 0)
    def _():
        s_ref[...] = jnp.zeros_like(s_ref)

    c = RET_CHUNK
    nch = RET_ROWS // c
    nh = RET_HEADS

    def split(ref, rotate):
        out = []
        for ci in range(nch):
            rows = pl.ds(ci * c, c)
            for k in range(nh):
                x = ref[rows, k * RET_DK:(k + 1) * RET_DK]
                out.append(_rotary(x, cos_ref[rows, :], sin_ref[rows, :]) if rotate else x)
        return jnp.stack(out)

    per_chunk = lambda t: jnp.concatenate([t] * nch, axis=0)
    q = split(q_ref, True)
    k = split(k_ref, True) * (RET_DK ** -0.5)
    v = split(v_ref, False).astype(BF16)
    att = jnp.einsum('bid,bjd->bij', q.astype(BF16), k.astype(BF16),
                     preferred_element_type=F32) * per_chunk(intra_ref[...])
    o = jnp.einsum('bij,bjv->biv', att.astype(BF16), v, preferred_element_type=F32)
    kv = jnp.einsum('bjd,bjv->bdv', (k * per_chunk(kdec_ref[...])).astype(BF16), v, preferred_element_type=F32)
    s = s_ref[0]
    cdec = cdec_ref[:, 0:1, :]
    starts = []
    for ci in range(nch):
        starts.append(s.astype(BF16))
        s = s * cdec + kv[ci * nh:(ci + 1) * nh]
    s_ref[0] = s
    o = o + jnp.einsum('bid,bdv->biv', (q * per_chunk(qdec_ref[...])).astype(BF16),
                       jnp.concatenate(starts, axis=0), preferred_element_type=F32)
    for ci in range(nch):
        rows = pl.ds(ci * c, c)
        for k in range(nh):
            cols = slice(k * RET_DK, (k + 1) * RET_DK)
            o_ref[rows, cols] = _group_norm_gate(o[ci * nh + k], gn_ref[:, cols], g_ref[rows, cols]).astype(o_ref.dtype)


def _ret_decay_tables(c):
    lg = np.log(1.0 - 2.0 ** (-5.0 - np.arange(RET_HEADS, dtype=np.float64)))
    idx = np.arange(c, dtype=np.float64)
    diff = idx[:, None] - idx[None, :]
    intra = np.where(diff >= 0, np.exp(np.maximum(diff, 0.0)[None] * lg[:, None, None]), 0.0)
    ones = np.ones((1, 1, RET_DK))
    qdec = np.exp((idx[None, :, None] + 1.0) * lg[:, None, None]) * ones
    kdec = np.exp((c - 1.0 - idx[None, :, None]) * lg[:, None, None]) * ones
    cdec = np.exp(c * lg)[:, None, None] * np.ones((1, 8, RET_DK))
    return [jnp.asarray(t, F32) for t in (intra, qdec, kdec, cdec)]


def _ret_prompt(proj, mix, cos, sin, gn, batch, seq):
    c = RET_CHUNK
    r = RET_ROWS
    nc = seq // r
    w = RET_HEADS * RET_DK
    tables = _ret_decay_tables(c)
    col = lambda j: pl.BlockSpec((r, w), lambda b, i: (b * nc + i, j))
    whole = lambda t: pl.BlockSpec(t.shape, lambda b, i: (0,) * t.ndim)
    return pl.pallas_call(
        _ret_prompt_kernel,
        grid=(batch, nc),
        in_specs=[col(0), col(1), col(2), col(3),
                  pl.BlockSpec((r, RET_HALF), lambda b, i: (i, 0)),
                  pl.BlockSpec((r, RET_HALF), lambda b, i: (i, 0)),
                  *[whole(t) for t in tables],
                  pl.BlockSpec((1, w), lambda b, i: (0, 0)),
                  pl.BlockSpec(memory_space=pl.ANY)],
        out_specs=[pl.BlockSpec((r, w), lambda b, i: (b * nc + i, 0)),
                   pl.BlockSpec((1, RET_HEADS, RET_DK, RET_DK), lambda b, i: (b, 0, 0, 0))],
        out_shape=[jax.ShapeDtypeStruct(mix.shape, mix.dtype),
                   jax.ShapeDtypeStruct((batch, RET_HEADS, RET_DK, RET_DK), F32)],
        input_output_aliases={11: 0},
        compiler_params=_cparams(2),
        name="ret_prompt",
    )(proj, proj, proj, proj, cos, sin, *tables, gn.reshape(1, MIX_HALF), mix)


def _columns(x, n):
    t = x.shape[0]
    parts = [jnp.concatenate([x[:, i:i + 128]] * (128 // t), axis=0).T for i in range(0, n, 128)]
    return parts[0] if len(parts) == 1 else jnp.concatenate(parts, axis=0)


def _ret_sample_kernel(q_ref, k_ref, v_ref, g_ref, cos_ref, sin_ref, cdec_ref, gn_ref, s_ref, mix_ref,
                       o_ref, so_ref):
    del mix_ref
    cos = cos_ref[...]
    sin = sin_ref[...]
    q = _rotary(q_ref[...], cos, sin)
    k = _rotary(k_ref[...], cos, sin) * (RET_DK ** -0.5)
    v = v_ref[...]
    kt = _columns(k, RET_DK)
    gamma = cdec_ref[0, 0:1, :]
    qb = q.astype(BF16)
    tok = lax.broadcasted_iota(jnp.int32, q.shape, 0)
    o = jnp.zeros(q.shape, F32)
    for t in range(SAMPLE_TOK):
        s_new = s_ref[t, 0] * gamma + kt[:, t:t + 1] * v[t:t + 1, :]
        so_ref[t, 0] = s_new
        o = jnp.where(tok == t, jnp.dot(qb, s_new.astype(BF16), preferred_element_type=F32), o)
    o_ref[...] = _group_norm_gate(o, gn_ref[...], g_ref[...]).astype(o_ref.dtype)


def _ret_sample(proj, mix, state, cos, sin, gn, row0):
    nb = state.shape[0]
    t = SAMPLE_TOK
    r0 = row0 // t
    _, _, _, cdec = _ret_decay_tables(1)
    col = lambda off: pl.BlockSpec((t, RET_DK), lambda b, h: (r0 + b, off + h))
    st = pl.BlockSpec((t, 1, RET_DK, RET_DK), lambda b, h: (b, h, 0, 0))
    return pl.pallas_call(
        _ret_sample_kernel,
        grid=(nb // t, RET_HEADS),
        in_specs=[col(0), col(RET_HEADS), col(2 * RET_HEADS), col(3 * RET_HEADS),
                  pl.BlockSpec((1, RET_HALF), lambda b, h: (0, 0)),
                  pl.BlockSpec((1, RET_HALF), lambda b, h: (0, 0)),
                  pl.BlockSpec((1, 8, RET_DK), lambda b, h: (h, 0, 0)),
                  pl.BlockSpec((1, RET_DK), lambda b, h: (0, h)),
                  st,
                  pl.BlockSpec(memory_space=pl.ANY)],
        out_specs=[pl.BlockSpec((t, RET_DK), lambda b, h: (r0 + b, h)), st],
        out_shape=[jax.ShapeDtypeStruct(mix.shape, mix.dtype),
                   jax.ShapeDtypeStruct(state.shape, F32)],
        input_output_aliases={9: 0},
        compiler_params=_cparams(2),
        name="ret_sample",
    )(proj, proj, proj, proj, cos, sin, cdec, gn.reshape(1, MIX_HALF), state, mix)


def _hg_lower_bound(lb_ref, layer):
    x = lb_ref[...]
    e = jnp.exp(x - jnp.max(x, axis=0, keepdims=True))
    return jnp.sum(e[:layer + 1], axis=0, keepdims=True) / jnp.sum(e, axis=0, keepdims=True)


def _hg_gates(gq, gf, lb):
    f = lb + (1.0 - lb) * jax.nn.sigmoid(gf)
    return _silu(gq), 1.0 - f, f


def _hg_out(o, gn, gg):
    return o * lax.rsqrt(jnp.mean(o * o, axis=-1, keepdims=True) + EPS) * gn * _silu(gg)


def _split3(x):
    hi = x.astype(BF16)
    r = x - hi.astype(F32)
    mid = r.astype(BF16)
    lo = (r - mid.astype(F32)).astype(BF16)
    return hi, mid, lo


def _hg_block(qq, kk, lf, v, st, tri):
    c = HG_CHUNK
    nch = qq.shape[0] // c
    nh = qq.shape[1] // HG_DK
    w = nh * HG_DK
    hi, mid, lo = _split3(lf)
    b3 = jnp.dot(tri, jnp.concatenate([hi, mid, lo], axis=1), preferred_element_type=F32)
    b = b3[:, :w] + b3[:, w:2 * w] + b3[:, 2 * w:]
    split = lambda t: jnp.stack([t[ci * c:(ci + 1) * c, k * HG_DK:(k + 1) * HG_DK]
                                 for ci in range(nch) for k in range(nh)])
    q3, k3, bc = split(qq), split(kk), split(b)
    vb = split(v.astype(BF16))
    nb = nch * nh
    ri = lax.broadcasted_iota(jnp.int32, (1, c, c), 1)
    cj = lax.broadcasted_iota(jnp.int32, (1, c, c), 2)
    mid = bc[:, c // 2 - 1:c // 2]
    off_mid = bc - mid

    def att_mid_referenced():
        q_m = (q3 * jnp.exp(off_mid)).astype(BF16)
        k_m = (k3 * jnp.exp(-off_mid)).astype(BF16)
        return jnp.where(ri >= cj, jnp.einsum('cid,cjd->cij', q_m, k_m, preferred_element_type=F32), 0.0)

    def att_any_decay():
        refs = [jnp.zeros_like(bc[:, 0:1])] + [bc[:, i0 - 1:i0] for i0 in range(HG_SUB, c, HG_SUB)]
        span = jnp.concatenate([jnp.broadcast_to(r, (nb, HG_SUB, HG_DK)) for r in refs], axis=1) - bc
        q_t = (q3 * jnp.exp(-span)).astype(BF16)
        row = lax.broadcasted_iota(jnp.int32, (1, c, HG_DK), 1)
        sub_row = lax.broadcasted_iota(jnp.int32, (1, HG_SUB, c), 1)
        sub_col = lax.broadcasted_iota(jnp.int32, (1, HG_SUB, c), 2)
        rows = []
        for i0 in range(0, c, HG_SUB):
            att_i = jnp.zeros((nb, HG_SUB, c), F32)
            if i0 > 0:
                r = refs[i0 // HG_SUB]
                k_t = jnp.where(row < i0, k3 * jnp.exp(jnp.minimum(r - bc, 0.0)), 0.0).astype(BF16)
                att_i = jnp.einsum('cid,cjd->cij', q_t[:, i0:i0 + HG_SUB], k_t, preferred_element_type=F32)
            q_i = q3[:, i0:i0 + HG_SUB]
            b_i = bc[:, i0:i0 + HG_SUB]
            for j in range(HG_SUB):
                jj = i0 + j
                p = q_i * (k3[:, jj:jj + 1] * jnp.exp(jnp.minimum(b_i - bc[:, jj:jj + 1], 0.0)))
                s_j = jnp.sum(p, axis=2, keepdims=True)
                att_i = jnp.where((sub_col == jj) & (sub_row >= j), s_j, att_i)
            rows.append(att_i)
        return jnp.concatenate(rows, axis=1)

    att = lax.cond(jnp.max(jnp.abs(off_mid)) < HG_SAFE_SPAN, att_mid_referenced, att_any_decay).astype(BF16)
    o_intra = jnp.einsum('cij,cjv->civ', att, vb, preferred_element_type=F32)
    b_last = bc[:, c - 1:c]
    qe = (q3 * jnp.exp(bc)).astype(BF16)
    khat = (k3 * jnp.exp(b_last - bc)).astype(BF16)
    dec = jnp.exp(b_last)
    upd = jnp.einsum('cjv,cjd->cvd', vb, khat, preferred_element_type=F32)
    starts = []
    for ci in range(nch):
        heads = slice(ci * nh, (ci + 1) * nh)
        starts.append(st.astype(BF16))
        st = st * dec[heads] + upd[heads]
    o = o_intra + jnp.einsum('cid,cvd->civ', qe, jnp.concatenate(starts, axis=0), preferred_element_type=F32)
    o = jnp.concatenate([jnp.concatenate([o[ci * nh + k] for k in range(nh)], axis=1) for ci in range(nch)], axis=0)
    return o, st


def _hg_prompt_kernel(layer, gq_ref, gf_ref, gi_ref, gg_ref, lb_ref, gn_ref, tri_ref, mix_ref,
                      o_ref, s_ref, st_ref):
    del mix_ref

    @pl.when(pl.program_id(2) == 0)
    def _():
        st_ref[...] = jnp.zeros_like(st_ref)

    qq, kk, f = _hg_gates(gq_ref[...], gf_ref[...], _hg_lower_bound(lb_ref, layer))
    o, st = _hg_block(qq, kk, jnp.log(f), gi_ref[...], st_ref[...], tri_ref[...])
    for k in range(HG_NH):
        cols = slice(k * HG_DK, (k + 1) * HG_DK)
        o_ref[:, cols] = _hg_out(o[:, cols], gn_ref[:, cols], gg_ref[:, cols]).astype(o_ref.dtype)
    st_ref[...] = st

    @pl.when(pl.program_id(2) == pl.num_programs(2) - 1)
    def _():
        for k in range(HG_NH):
            s_ref[0, k] = st[k].T


def _hg_prompt(proj, mix, lb_raw, gn, layer, batch, seq):
    nt = seq // HG_ROWS
    c0 = MIX_HALF * 4 // HG_DK
    nh = HG_NH
    w = nh * HG_DK
    c0 = c0 // nh
    col = lambda off: pl.BlockSpec((HG_ROWS, w), lambda b, h, i: (b * nt + i, c0 + off // nh + h))
    tri = jnp.asarray(np.kron(np.eye(HG_ROWS // HG_CHUNK), np.tril(np.ones((HG_CHUNK, HG_CHUNK)))), BF16)
    return pl.pallas_call(
        functools.partial(_hg_prompt_kernel, layer),
        grid=(batch, HG_HEADS // nh, nt),
        in_specs=[col(0), col(HG_HEADS), col(2 * HG_HEADS), col(3 * HG_HEADS),
                  pl.BlockSpec((lb_raw.shape[0], w), lambda b, h, i: (0, h)),
                  pl.BlockSpec((1, w), lambda b, h, i: (0, h)),
                  pl.BlockSpec((HG_ROWS, HG_ROWS), lambda b, h, i: (0, 0)),
                  pl.BlockSpec(memory_space=pl.ANY)],
        out_specs=[pl.BlockSpec((HG_ROWS, w), lambda b, h, i: (b * nt + i, HG_HEADS // nh + h)),
                   pl.BlockSpec((1, nh, HG_DK, HG_DK), lambda b, h, i: (b, h, 0, 0))],
        out_shape=[jax.ShapeDtypeStruct(mix.shape, mix.dtype),
                   jax.ShapeDtypeStruct((batch, HG_HEADS, HG_DK, HG_DK), F32)],
        scratch_shapes=[pltpu.VMEM((nh, HG_DK, HG_DK), F32)],
        input_output_aliases={7: 0},
        compiler_params=_cparams(3),
        name="hgrn_prompt",
    )(proj, proj, proj, proj, lb_raw, gn.reshape(1, MIX_HALF), tri, mix)


HG_SAMPLE_NH = 4


def _hg_sample_kernel(layer, gq_ref, gf_ref, gi_ref, gg_ref, lb_ref, gn_ref, s_ref, mix_ref,
                      o_ref, so_ref):
    del mix_ref
    qq_all, kk_all, f_all = _hg_gates(gq_ref[...], gf_ref[...], _hg_lower_bound(lb_ref, layer))
    tok = lax.broadcasted_iota(jnp.int32, (SAMPLE_TOK, HG_DK), 0)
    for k in range(HG_SAMPLE_NH):
        cols = slice(k * HG_DK, (k + 1) * HG_DK)
        v = gi_ref[:, cols]
        kt = _columns(kk_all[:, cols], HG_DK)
        ft = _columns(f_all[:, cols], HG_DK)
        qb = qq_all[:, cols].astype(BF16)
        o = jnp.zeros((SAMPLE_TOK, HG_DK), F32)
        for t in range(SAMPLE_TOK):
            s_new = s_ref[t, k] * ft[:, t:t + 1] + kt[:, t:t + 1] * v[t:t + 1, :]
            so_ref[t, k] = s_new
            o = jnp.where(tok == t, jnp.dot(qb, s_new.astype(BF16), preferred_element_type=F32), o)
        o_ref[:, cols] = _hg_out(o, gn_ref[:, cols], gg_ref[:, cols]).astype(o_ref.dtype)


def _hg_sample(proj, mix, state, lb_raw, gn, layer, row0):
    nb = state.shape[0]
    t = SAMPLE_TOK
    nh = HG_SAMPLE_NH
    w = nh * HG_DK
    r0 = row0 // t
    c0 = MIX_HALF * 4 // w
    col = lambda off: pl.BlockSpec((t, w), lambda b, h: (r0 + b, c0 + off // nh + h))
    st = pl.BlockSpec((t, nh, HG_DK, HG_DK), lambda b, h: (b, h, 0, 0))
    return pl.pallas_call(
        functools.partial(_hg_sample_kernel, layer),
        grid=(nb // t, HG_HEADS // nh),
        in_specs=[col(0), col(HG_HEADS), col(2 * HG_HEADS), col(3 * HG_HEADS),
                  pl.BlockSpec((lb_raw.shape[0], w), lambda b, h: (0, h)),
                  pl.BlockSpec((1, w), lambda b, h: (0, h)),
                  st,
                  pl.BlockSpec(memory_space=pl.ANY)],
        out_specs=[pl.BlockSpec((t, w), lambda b, h: (r0 + b, HG_HEADS // nh + h)), st],
        out_shape=[jax.ShapeDtypeStruct(mix.shape, mix.dtype),
                   jax.ShapeDtypeStruct(state.shape, F32)],
        input_output_aliases={7: 0},
        compiler_params=_cparams(2),
        name="hgrn_sample",
    )(proj, proj, proj, proj, lb_raw, gn.reshape(1, MIX_HALF), state, mix)


def _cmul(ar, ai, br, bi):
    return ar * br - ai * bi, ar * bi + ai * br


def _s5_prep_kernel(lr_ref, li_ref, ldt_ref, brt_ref, bit_ref, ar_ref, ai_ref, bbr_ref, bbi_ref):
    lr = lr_ref[...]
    li = li_ref[...]
    dt = jnp.exp(ldt_ref[...])
    mag = jnp.exp(lr * dt)
    ar = mag * jnp.cos(li * dt)
    ai = mag * jnp.sin(li * dt)
    den = lr * lr + li * li
    cr = ((ar - 1.0) * lr + ai * li) / den
    ci = (ai * lr - (ar - 1.0) * li) / den
    brt = brt_ref[...]
    bit = bit_ref[...]
    bbr_ref[...] = cr * brt - ci * bit
    bbi_ref[...] = cr * bit + ci * brt
    ar_ref[...] = ar
    ai_ref[...] = ai


def _s5_prep(lam_re, lam_im, log_dt, b_re, b_im):
    n = S5_GROUPS * S5_STATE
    flat = lambda t: t.reshape(1, n)
    b_t = lambda t: jnp.transpose(t, (2, 0, 1)).reshape(S5_GROUP, n)
    ldt = jnp.repeat(log_dt, S5_STATE).reshape(1, n)
    return pl.pallas_call(
        _s5_prep_kernel,
        out_shape=[jax.ShapeDtypeStruct((1, n), F32)] * 2 + [jax.ShapeDtypeStruct((S5_GROUP, n), F32)] * 2,
        compiler_params=pltpu.CompilerParams(vmem_limit_bytes=VMEM_LIMIT_BYTES),
        name="s5_prep",
    )(flat(lam_re), flat(lam_im), ldt, b_t(b_re), b_t(b_im))


S5_T = 256
S5_HALF_ST = S5_BLK_ST // 2


def _s5_mats_kernel(batch, bbr_ref, bbi_ref, cr_ref, ci_ref, ar_ref, ai_ref, b_out, c_out, a_r_out, a_i_out):
    n = S5_HALF_ST
    hg = S5_BLK_GROUPS // 2
    rows = hg * S5_GROUP
    same_group = (lax.shift_right_logical(lax.broadcasted_iota(jnp.int32, (rows, n), 0), 4)
                  == lax.shift_right_logical(lax.broadcasted_iota(jnp.int32, (rows, n), 1), 6))
    place = lambda t: jnp.where(same_group, jnp.concatenate([t] * hg, axis=0), 0.0)
    for h in range(2):
        cols = slice(h * n, (h + 1) * n)
        r0 = slice(h * rows, (h + 1) * rows)
        b_out[0, r0, 0:n] = place(bbr_ref[:, cols]).astype(BF16)
        b_out[0, r0, n:2 * n] = place(bbi_ref[:, cols]).astype(BF16)
        c_out[0, 0:n, r0] = place(cr_ref[:, cols]).T.astype(BF16)
        c_out[0, n:2 * n, r0] = (-place(ci_ref[:, cols])).T.astype(BF16)
        a_r_out[0, h * batch:(h + 1) * batch, :] = jnp.broadcast_to(ar_ref[:, cols], (batch, n))
        a_i_out[0, h * batch:(h + 1) * batch, :] = jnp.broadcast_to(ai_ref[:, cols], (batch, n))


def _s5_mats(bbr, bbi, c_re, c_im, a_r, a_i, batch):
    assert S5_GROUP == 16 and S5_STATE == 64
    n_all = S5_GROUPS * S5_STATE
    c_t = lambda t: jnp.transpose(t, (1, 0, 2)).reshape(S5_GROUP, n_all)
    blk = lambda r: pl.BlockSpec((r, S5_BLK_ST), lambda j: (0, j))
    out3 = lambda r, c: pl.BlockSpec((1, r, c), lambda j: (j, 0, 0))
    return pl.pallas_call(
        functools.partial(_s5_mats_kernel, batch),
        grid=(S5_NBLK,),
        in_specs=[blk(S5_GROUP)] * 4 + [blk(1)] * 2,
        out_specs=[out3(S5_BLK_CH, S5_BLK_ST), out3(S5_BLK_ST, S5_BLK_CH),
                   out3(2 * batch, S5_HALF_ST), out3(2 * batch, S5_HALF_ST)],
        out_shape=[jax.ShapeDtypeStruct((S5_NBLK, S5_BLK_CH, S5_BLK_ST), BF16),
                   jax.ShapeDtypeStruct((S5_NBLK, S5_BLK_ST, S5_BLK_CH), BF16),
                   jax.ShapeDtypeStruct((S5_NBLK, 2 * batch, S5_HALF_ST), F32),
                   jax.ShapeDtypeStruct((S5_NBLK, 2 * batch, S5_HALF_ST), F32)],
        compiler_params=_cparams(1),
        name="s5_mats",
    )(bbr, bbi, c_t(c_re), c_t(c_im), a_r, a_i)


S5_NB = 2


def _s5_tb_kernel(u_ref, b_ref, c_ref, ar_ref, ai_ref, d_ref, z_in_ref, z_ref, hr_out, hi_out,
                  x_ref, hb_ref, hr_ref, hi_ref):
    del z_in_ref
    i = pl.program_id(1)
    n = S5_HALF_ST
    ch = S5_BLK_CH
    t_steps = u_ref.shape[0] // 4
    half_rows = 4 * t_steps

    @pl.when(i == 0)
    def _():
        hr_ref[...] = jnp.zeros_like(hr_ref)
        hi_ref[...] = jnp.zeros_like(hi_ref)

    first4 = lax.broadcasted_iota(jnp.int32, (1, 8, ch), 1) < 4
    half0 = lax.broadcasted_iota(jnp.int32, (1, 8, ch), 2) < ch // 2
    for k in range(S5_NB):
        u3 = u_ref[:, k * ch:(k + 1) * ch].reshape(t_steps // 2, 8, ch)
        r3 = pltpu.roll(u3, 4, 1)
        even = jnp.where(first4, jnp.where(half0, u3, 0.0), jnp.where(half0, 0.0, r3))
        odd = jnp.where(first4, jnp.where(half0, r3, 0.0), jnp.where(half0, 0.0, u3))
        lhs = jnp.stack([even, odd], axis=1).reshape(8 * t_steps, ch).astype(BF16)
        for r0 in (0, half_rows):
            x_ref[k, r0:r0 + half_rows, :] = jnp.dot(lhs[r0:r0 + half_rows], b_ref[k],
                                                     preferred_element_type=F32)
    a = [(ar_ref[k], ai_ref[k]) for k in range(S5_NB)]

    def advance(t, hs):
        rows = pl.ds(pl.multiple_of(t * 8, 8), 8)
        out = []
        for k in range(S5_NB):
            dr, di = _cmul(*a[k], *hs[k])
            out.append((dr + x_ref[k, rows, 0:n], di + x_ref[k, rows, n:2 * n]))
        return tuple(out)

    def emit(t2, hs):
        h1 = advance(2 * t2, hs)
        h2 = advance(2 * t2 + 1, h1)
        rows = pl.ds(pl.multiple_of(t2 * 16, 16), 16)
        for k in range(S5_NB):
            hb_ref[k, rows, 0:n] = jnp.concatenate([h1[k][0], h2[k][0]], axis=0).astype(BF16)
            hb_ref[k, rows, n:2 * n] = jnp.concatenate([h1[k][1], h2[k][1]], axis=0).astype(BF16)
        return h2

    hs = lax.fori_loop(0, t_steps // 2, emit, tuple((hr_ref[k], hi_ref[k]) for k in range(S5_NB)), unroll=2)
    for k in range(S5_NB):
        hr_ref[k] = hs[k][0]
        hi_ref[k] = hs[k][1]

    pick = lambda yv: jnp.where(half0, yv, pltpu.roll(yv, 4, 1))
    for k in range(S5_NB):
        y2 = jnp.concatenate([jnp.dot(hb_ref[k, r0:r0 + half_rows, :], c_ref[k], preferred_element_type=F32)
                              for r0 in (0, half_rows)], axis=0)
        y4 = y2.reshape(t_steps // 2, 2, 8, ch)
        y = jnp.where(first4, pick(y4[:, 0]), pltpu.roll(pick(y4[:, 1]), 4, 1)).reshape(4 * t_steps, ch)
        cols = slice(k * ch, (k + 1) * ch)
        z_ref[:, cols] = jax.nn.gelu(y + d_ref[:, cols] * u_ref[:, cols]).astype(z_ref.dtype)

    @pl.when(i == pl.num_programs(1) - 1)
    def _():
        for k in range(S5_NB):
            for h in range(2):
                cols = slice((2 * k + h) * n, (2 * k + h + 1) * n)
                hr_out[:, cols] = hs[k][0][4 * h:4 * h + 4, :]
                hi_out[:, cols] = hs[k][1][4 * h:4 * h + 4, :]


def _s5_prompt_tb(u, z, mats, d, batch, seq):
    assert batch == 4
    b_m, c_m, a_r, a_i = mats
    rows = batch * S5_T
    nt = seq // S5_T
    nb = S5_NB
    blk3 = lambda r, c: pl.BlockSpec((nb, r, c), lambda j, i: (j, 0, 0))
    st_spec = pl.BlockSpec((batch, nb * S5_BLK_ST), lambda j, i: (0, j))
    st_shape = jax.ShapeDtypeStruct((batch, S5_GROUPS * S5_STATE), F32)
    return pl.pallas_call(
        _s5_tb_kernel,
        grid=(S5_NBLK // nb, nt),
        in_specs=[pl.BlockSpec((rows, nb * S5_BLK_CH), lambda j, i: (i, j)),
                  blk3(S5_BLK_CH, S5_BLK_ST), blk3(S5_BLK_ST, S5_BLK_CH),
                  blk3(2 * batch, S5_HALF_ST), blk3(2 * batch, S5_HALF_ST),
                  pl.BlockSpec((1, nb * S5_BLK_CH), lambda j, i: (0, j)),
                  pl.BlockSpec(memory_space=pl.ANY)],
        out_specs=[pl.BlockSpec((rows, nb * S5_BLK_CH), lambda j, i: (i, j)), st_spec, st_spec],
        out_shape=[jax.ShapeDtypeStruct(z.shape, z.dtype), st_shape, st_shape],
        scratch_shapes=[pltpu.VMEM((nb, 2 * rows, S5_BLK_ST), F32), pltpu.VMEM((nb, 2 * rows, S5_BLK_ST), BF16),
                        pltpu.VMEM((nb, 2 * batch, S5_HALF_ST), F32), pltpu.VMEM((nb, 2 * batch, S5_HALF_ST), F32)],
        input_output_aliases={6: 0},
        compiler_params=_cparams(2),
        name="s5_prompt",
    )(u, b_m, c_m, a_r, a_i, d.reshape(1, D_MODEL), z)


def _s5_sample_kernel(u_ref, b_ref, c_ref, ar_ref, ai_ref, d_ref, h0r_ref, h0i_ref,
                      z_in_ref, z_ref, hr_out, hi_out):
    del z_in_ref
    n = S5_HALF_ST
    u = u_ref[...]
    nb = u.shape[0]
    half0 = lax.broadcasted_iota(jnp.int32, u.shape, 1) < S5_BLK_CH // 2
    lhs = jnp.concatenate([jnp.where(half0, u, 0.0), jnp.where(half0, 0.0, u)], axis=0).astype(BF16)
    x = jnp.dot(lhs, b_ref[0], preferred_element_type=F32)
    halves = []
    for h in range(2):
        rows = slice(h * nb, (h + 1) * nb)
        cols = slice(h * n, (h + 1) * n)
        row_a = slice(h * (ar_ref.shape[1] // 2), h * (ar_ref.shape[1] // 2) + 1)
        dr, di = _cmul(ar_ref[0, row_a, :], ai_ref[0, row_a, :], h0r_ref[:, cols], h0i_ref[:, cols])
        hr = x[rows, 0:n] + dr
        hi = x[rows, n:2 * n] + di
        hr_out[:, cols] = hr
        hi_out[:, cols] = hi
        halves.append(jnp.concatenate([hr, hi], axis=1))
    y2 = jnp.dot(jnp.concatenate(halves, axis=0).astype(BF16), c_ref[0], preferred_element_type=F32)
    y = jnp.where(half0, y2[0:nb], y2[nb:2 * nb])
    z_ref[...] = jax.nn.gelu(y + d_ref[...] * u).astype(z_ref.dtype)


def _s5_sample(u, z, mats, d, h0r, h0i, row0):
    b_m, c_m, a_r, a_i = mats
    nb = h0r.shape[0]
    r0 = row0 // nb
    blk3 = lambda r, c: pl.BlockSpec((1, r, c), lambda j: (j, 0, 0))
    st_spec = pl.BlockSpec((nb, S5_BLK_ST), lambda j: (0, j))
    st_shape = jax.ShapeDtypeStruct(h0r.shape, F32)
    return pl.pallas_call(
        _s5_sample_kernel,
        grid=(S5_NBLK,),
        in_specs=[pl.BlockSpec((nb, S5_BLK_CH), lambda j: (r0, j)),
                  blk3(S5_BLK_CH, S5_BLK_ST), blk3(S5_BLK_ST, S5_BLK_CH),
                  blk3(a_r.shape[1], S5_HALF_ST), blk3(a_r.shape[1], S5_HALF_ST),
                  pl.BlockSpec((1, S5_BLK_CH), lambda j: (0, j)),
                  st_spec, st_spec,
                  pl.BlockSpec(memory_space=pl.ANY)],
        out_specs=[pl.BlockSpec((nb, S5_BLK_CH), lambda j: (r0, j)), st_spec, st_spec],
        out_shape=[jax.ShapeDtypeStruct(z.shape, z.dtype), st_shape, st_shape],
        input_output_aliases={8: 0},
        compiler_params=_cparams(1),
        name="s5_sample",
    )(u, b_m, c_m, a_r, a_i, d.reshape(1, D_MODEL), h0r, h0i, z)


TB_T = 128
LANES = 128


def _interleave_norm_kernel(n_steps, n_sample, batch, *refs):
    x_refs = refs[:batch]
    xs_ref, g_ref, xt_ref, ut_ref, slab_ref = refs[batch:]
    i = pl.program_id(0)
    d = xt_ref.shape[1]

    @pl.when(i < n_steps)
    def _():
        for dst_ref, norm in ((xt_ref, False), (ut_ref, True)):
            for b, x_ref in enumerate(x_refs):
                v = x_ref[...]
                if norm:
                    v = _rms(v, g_ref[...])
                for s in range(d // LANES):
                    slab_ref[s, pl.ds(b, TB_T, stride=batch), :] = v[:, s * LANES:(s + 1) * LANES]
            for s in range(d // LANES):
                dst_ref[:, s * LANES:(s + 1) * LANES] = slab_ref[s]

    @pl.when(i >= n_steps)
    def _():
        v = xs_ref[...]
        xt_ref[0:n_sample, :] = v
        ut_ref[0:n_sample, :] = _rms(v, g_ref[...])


def _interleave_norm(x, g, batch, seq):
    m, d = x.shape
    n_steps = seq // TB_T
    ms = m - batch * seq
    rows = batch * TB_T
    per_batch = lambda b: pl.BlockSpec((TB_T, d), lambda i: (b * n_steps + jnp.minimum(i, n_steps - 1), 0))
    out = pl.BlockSpec((rows, d), lambda i: (i, 0))
    return pl.pallas_call(
        functools.partial(_interleave_norm_kernel, n_steps, ms, batch),
        grid=(n_steps + 1,),
        in_specs=[per_batch(b) for b in range(batch)]
        + [pl.BlockSpec((ms, d), lambda i: (batch * seq // ms, 0)), pl.BlockSpec((1, d), lambda i: (0, 0))],
        out_specs=[out, out],
        out_shape=[jax.ShapeDtypeStruct((m, d), F32)] * 2,
        scratch_shapes=[pltpu.VMEM((d // LANES, rows, LANES), F32)],
        compiler_params=_cparams(1),
        name="interleave_norm",
    )(*([x] * (batch + 1)), g.reshape(1, d))


def _deinterleave_norm_kernel(n_steps, n_sample, batch, x_ref, g_ref, y_ref, ys_ref, slab_ref):
    i = pl.program_id(0)
    d = x_ref.shape[1]

    @pl.when(i < n_steps)
    def _():
        y = _rms(x_ref[...], g_ref[...])
        for s in range(d // LANES):
            slab_ref[s] = y[:, s * LANES:(s + 1) * LANES]
        for b in range(batch):
            for s in range(d // LANES):
                y_ref[b, :, s * LANES:(s + 1) * LANES] = slab_ref[s, pl.ds(b, TB_T, stride=batch), :]

    @pl.when(i >= n_steps)
    def _():
        ys_ref[...] = _rms(x_ref[0:n_sample, :], g_ref[...])


def _deinterleave_norm(x, g, batch, seq):
    m, d = x.shape
    n_steps = seq // TB_T
    ms = m - batch * seq
    rows = batch * TB_T
    return pl.pallas_call(
        functools.partial(_deinterleave_norm_kernel, n_steps, ms, batch),
        grid=(n_steps + 1,),
        in_specs=[pl.BlockSpec((rows, d), lambda i: (i, 0)), pl.BlockSpec((1, d), lambda i: (0, 0))],
        out_specs=[pl.BlockSpec((batch, TB_T, d), lambda i: (0, jnp.minimum(i, n_steps - 1), 0)),
                   pl.BlockSpec((ms, d), lambda i: (0, 0))],
        out_shape=[jax.ShapeDtypeStruct((batch, seq, d), F32), jax.ShapeDtypeStruct((ms, d), F32)],
        scratch_shapes=[pltpu.VMEM((d // LANES, rows, LANES), F32)],
        compiler_params=_cparams(1),
        name="deinterleave_norm",
    )(x, g.reshape(1, d))


def _rope_tables(pos):
    inv = ROPE_BASE ** (-jnp.arange(RET_HALF, dtype=F32) / RET_HALF)
    ang = pos[:, None] * inv[None, :]
    return jnp.cos(ang), jnp.sin(ang)


def _ffn(x, norm_g, wg, wu, wd, layer):
    h = _rmsnorm(x, norm_g[layer], [BF16])[0]
    t = _matmul(h, [wg, wu], _epi_swiglu, BF16, "ffn_gate_up", layer=layer)
    return _matmul(t, [wd], _epi_id, F32, "ffn_down", res=x, layer=layer)


def kernel(x_prompt, x_sample, state_ret, state_hgrn, state_s5_re, state_s5_im, attn_norm_g, w_in, ret_gn_g, hg_lb, hg_gn_g, w_out, ssm_norm_g, s5_lam_re, s5_lam_im, s5_log_dt, s5_b_re, s5_b_im, s5_c_re, s5_c_im, s5_d, w_glu_a, w_glu_b, ffn_norm_g, w_ffn_gate, w_ffn_up, w_ffn_down, final_norm_g):
    one = lambda t: t.reshape(t.shape[1:])
    (state_ret, state_hgrn, state_s5_re, state_s5_im, attn_norm_g, w_in, ret_gn_g, hg_gn_g, w_out, ssm_norm_g,
     s5_lam_re, s5_lam_im, s5_log_dt, s5_b_re, s5_b_im, s5_c_re, s5_c_im, s5_d, w_glu_a, w_glu_b) = map(one, (
         state_ret, state_hgrn, state_s5_re, state_s5_im, attn_norm_g, w_in, ret_gn_g, hg_gn_g, w_out, ssm_norm_g,
         s5_lam_re, s5_lam_im, s5_log_dt, s5_b_re, s5_b_im, s5_c_re, s5_c_im, s5_d, w_glu_a, w_glu_b))
    cos_p, sin_p = _rope_tables(jnp.arange(SEQ, dtype=F32))
    cos_s, sin_s = _rope_tables(jnp.full((1,), float(PAST_LEN), F32))

    h, x = _rmsnorm_stack(x_prompt.reshape(N_PROMPT, D_MODEL), x_sample.reshape(DEC_BATCH, D_MODEL), attn_norm_g)
    proj = _matmul(h, [w_in], _epi_id, F32, "w_in")
    mix = jnp.zeros((N_ROWS, D_MODEL), BF16)
    mix, ret_p = _ret_prompt(proj, mix, cos_p, sin_p, ret_gn_g, BATCH, SEQ)
    mix, hg_p = _hg_prompt(proj, mix, hg_lb, hg_gn_g, 0, BATCH, SEQ)
    mix, ret_s = _ret_sample(proj, mix, state_ret, cos_s, sin_s, ret_gn_g, N_PROMPT)
    mix, hg_s = _hg_sample(proj, mix, state_hgrn, hg_lb, hg_gn_g, 0, N_PROMPT)
    x = _matmul(mix, [w_out], _epi_id, F32, "w_out", res=x)
    x = _ffn(x, ffn_norm_g, w_ffn_gate, w_ffn_up, w_ffn_down, 0)

    x, u = _interleave_norm(x, ssm_norm_g, BATCH, SEQ)
    p_r, p_i, bbr, bbi = _s5_prep(s5_lam_re, s5_lam_im, s5_log_dt, s5_b_re, s5_b_im)
    mats = _s5_mats(bbr, bbi, s5_c_re, s5_c_im, p_r, p_i, BATCH)
    z = jnp.zeros((N_ROWS, D_MODEL), BF16)
    z, s5r_p, s5i_p = _s5_prompt_tb(u, z, mats, s5_d, BATCH, SEQ)
    n_st = S5_GROUPS * S5_STATE
    z, s5r_s, s5i_s = _s5_sample(u, z, mats, s5_d, state_s5_re.reshape(DEC_BATCH, n_st),
                                 state_s5_im.reshape(DEC_BATCH, n_st), N_PROMPT)
    x = _matmul(z, [w_glu_a, w_glu_b], _epi_glu, F32, "glu", res=x)
    x = _ffn(x, ffn_norm_g, w_ffn_gate, w_ffn_up, w_ffn_down, 1)

    y_p, y_s = _deinterleave_norm(x, final_norm_g, BATCH, SEQ)
    st = lambda t, b: t.reshape(1, b, S5_GROUPS, S5_STATE)
    return (y_p, y_s.reshape(DEC_BATCH, 1, D_MODEL),
            ret_p[None], ret_s[None], hg_p[None], hg_s[None],
            st(s5r_p, BATCH), st(s5i_p, BATCH), st(s5r_s, DEC_BATCH), st(s5i_s, DEC_BATCH))
```

```python
import functools

import numpy as np
import jax
import jax.numpy as jnp
from jax import lax
from jax.experimental import pallas as pl
from jax.experimental.pallas import tpu as pltpu

F32 = jnp.float32
BF16 = jnp.bfloat16

D_MODEL = 2048
BATCH = 4
SEQ = 2048
DEC_BATCH = 128
PAST_LEN = 16384
N_PROMPT = BATCH * SEQ
N_ROWS = N_PROMPT + DEC_BATCH
MIX_HALF = D_MODEL // 2
RET_HEADS = 4
RET_DK = MIX_HALF // RET_HEADS
RET_HALF = RET_DK // 2
RET_CHUNK = 128
RET_ROWS = 512
HG_HEADS = 8
HG_DK = MIX_HALF // HG_HEADS
HG_CHUNK = 64
HG_SUB = 16
HG_ROWS = 256
HG_NH = 8
HG_SAFE_SPAN = 40.0
S5_GROUP = 16
S5_GROUPS = D_MODEL // S5_GROUP
S5_STATE = 64
S5_BLK_GROUPS = 16
S5_BLK_CH = S5_BLK_GROUPS * S5_GROUP
S5_BLK_ST = S5_BLK_GROUPS * S5_STATE
S5_NBLK = S5_GROUPS // S5_BLK_GROUPS
ROPE_BASE = 10000.0
EPS = 1e-6
SAMPLE_TOK = 16

VMEM_LIMIT_BYTES = 56 * 1024 * 1024

MM_TILES = {"w_in": (1664, 1024), "w_out": (1664, 512), "ffn_gate_up": (1664, 512),
            "ffn_down": (832, 512), "glu": (1664, 512)}
MM_SUB_TILES = 2


def _cparams(n_axes):
    return pltpu.CompilerParams(dimension_semantics=("arbitrary",) * n_axes,
                                vmem_limit_bytes=VMEM_LIMIT_BYTES)


def _sigmoid(x):
    return 0.5 * jnp.tanh(0.5 * x) + 0.5


def _silu(x):
    return x * _sigmoid(x)


def _rmsnorm_kernel(x_ref, g_ref, *o_refs):
    x = x_ref[...]
    y = x * lax.rsqrt(jnp.mean(x * x, axis=-1, keepdims=True) + EPS) * g_ref[...]
    for o_ref in o_refs:
        o_ref[...] = y.astype(o_ref.dtype)


def _rmsnorm(x, g, dtypes, tm=640):
    m, d = x.shape
    return pl.pallas_call(
        _rmsnorm_kernel,
        grid=(m // tm,),
        in_specs=[pl.BlockSpec((tm, d), lambda i: (i, 0)),
                  pl.BlockSpec((1, d), lambda i: (0, 0))],
        out_specs=[pl.BlockSpec((tm, d), lambda i: (i, 0)) for _ in dtypes],
        out_shape=[jax.ShapeDtypeStruct((m, d), dt) for dt in dtypes],
        compiler_params=_cparams(1),
        name="rmsnorm",
    )(x, g.reshape(1, d))


ROW_TILE = 1024


def _rms(x, g):
    return x * lax.rsqrt(jnp.mean(x * x, axis=-1, keepdims=True) + EPS) * g


def _rmsnorm_stack_kernel(n_prompt_tiles, n_sample, xp_ref, xs_ref, g_ref, h_ref, x_ref):
    i = pl.program_id(0)

    @pl.when(i < n_prompt_tiles)
    def _():
        x = xp_ref[...]
        x_ref[...] = x
        h_ref[...] = _rms(x, g_ref[...]).astype(h_ref.dtype)

    @pl.when(i >= n_prompt_tiles)
    def _():
        x = xs_ref[...]
        x_ref[0:n_sample, :] = x
        h_ref[0:n_sample, :] = _rms(x, g_ref[...]).astype(h_ref.dtype)


def _rmsnorm_stack(x_prompt, x_sample, g):
    (mp, d), ms = x_prompt.shape, x_sample.shape[0]
    tm = ROW_TILE
    npt = mp // tm
    m = mp + ms
    return pl.pallas_call(
        functools.partial(_rmsnorm_stack_kernel, npt, ms),
        grid=(npt + 1,),
        in_specs=[pl.BlockSpec((tm, d), lambda i: (jnp.minimum(i, npt - 1), 0)),
                  pl.BlockSpec((ms, d), lambda i: (0, 0)),
                  pl.BlockSpec((1, d), lambda i: (0, 0))],
        out_specs=[pl.BlockSpec((tm, d), lambda i: (i, 0))] * 2,
        out_shape=[jax.ShapeDtypeStruct((m, d), BF16), jax.ShapeDtypeStruct((m, d), F32)],
        compiler_params=_cparams(1),
        name="rmsnorm_stack",
    )(x_prompt, x_sample, g.reshape(1, d))


def _mm_kernel(n_w, epilogue, has_res, a_ref, *refs):
    w_refs = refs[:n_w]
    res_ref = refs[n_w] if has_res else None
    o_ref = refs[n_w + has_res]
    wb_refs = refs[n_w + has_res + 1:]

    @pl.when(pl.program_id(1) == 0)
    def _():
        for w_ref, wb_ref in zip(w_refs, wb_refs):
            wb_ref[...] = w_ref[...].astype(BF16)

    sub = a_ref.shape[0] // MM_SUB_TILES
    for r in range(MM_SUB_TILES):
        rows = pl.ds(r * sub, sub)
        a = a_ref[rows, :]
        y = epilogue(*[jnp.dot(a, wb_ref[...], preferred_element_type=F32) for wb_ref in wb_refs])
        if has_res:
            y = res_ref[rows, :] + y
        o_ref[rows, :] = y.astype(o_ref.dtype)


def _matmul(a, ws, epilogue, out_dtype, name, res=None, layer=None):
    tm, tn = MM_TILES[name]
    m, k = a.shape
    n = ws[0].shape[-1]
    in_specs = [pl.BlockSpec((tm, k), lambda j, i: (i, 0))]
    if layer is None:
        in_specs += [pl.BlockSpec((k, tn), lambda j, i: (0, j)) for _ in ws]
    else:
        in_specs += [pl.BlockSpec((None, k, tn), lambda j, i: (layer, 0, j)) for _ in ws]
    args = [a, *ws]
    if res is not None:
        in_specs.append(pl.BlockSpec((tm, tn), lambda j, i: (i, j)))
        args.append(res)
    return pl.pallas_call(
        functools.partial(_mm_kernel, len(ws), epilogue, res is not None),
        grid=(n // tn, m // tm),
        in_specs=in_specs,
        out_specs=pl.BlockSpec((tm, tn), lambda j, i: (i, j)),
        out_shape=jax.ShapeDtypeStruct((m, n), out_dtype),
        scratch_shapes=[pltpu.VMEM((k, tn), BF16) for _ in ws],
        compiler_params=_cparams(2),
        name=name,
    )(*args)


def _epi_id(y):
    return y


def _epi_swiglu(g, u):
    return _silu(g) * u


def _epi_glu(a, b):
    return a * _sigmoid(b)


def _rotary(x, cos, sin):
    x1 = x[:, :RET_HALF]
    x2 = x[:, RET_HALF:]
    return jnp.concatenate([x1 * cos - x2 * sin, x1 * sin + x2 * cos], axis=1)


def _group_norm_gate(o, gn, g):
    mu = jnp.mean(o, axis=-1, keepdims=True)
    d = o - mu
    var = jnp.mean(d * d, axis=-1, keepdims=True)
    return d * lax.rsqrt(var + EPS) * gn * _silu(g)


def _ret_prompt_kernel(q_ref, k_ref, v_ref, g_ref, cos_ref, sin_ref, intra_ref, qdec_ref, kdec_ref,
                       cdec_ref, gn_ref, mix_ref, o_ref, s_ref):
    del mix_ref

    @pl.when(pl.program_id(1) == 0)
    def _():
        s_ref[...] = jnp.zeros_like(s_ref)

    c = RET_CHUNK
    nch = RET_ROWS // c
    nh = RET_HEADS

    def split(ref, rotate):
        out = []
        for ci in range(nch):
            rows = pl.ds(ci * c, c)
            for k in range(nh):
                x = ref[rows, k * RET_DK:(k + 1) * RET_DK]
                out.append(_rotary(x, cos_ref[rows, :], sin_ref[rows, :]) if rotate else x)
        return jnp.stack(out)

    per_chunk = lambda t: jnp.concatenate([t] * nch, axis=0)
    q = split(q_ref, True)
    k = split(k_ref, True) * (RET_DK ** -0.5)
    v = split(v_ref, False).astype(BF16)
    att = jnp.einsum('bid,bjd->bij', q.astype(BF16), k.astype(BF16),
                     preferred_element_type=F32) * per_chunk(intra_ref[...])
    o = jnp.einsum('bij,bjv->biv', att.astype(BF16), v, preferred_element_type=F32)
    kv = jnp.einsum('bjd,bjv->bdv', (k * per_chunk(kdec_ref[...])).astype(BF16), v, preferred_element_type=F32)
    s = s_ref[0]
    cdec = cdec_ref[:, 0:1, :]
    starts = []
    for ci in range(nch):
        starts.append(s.astype(BF16))
        s = s * cdec + kv[ci * nh:(ci + 1) * nh]
    s_ref[0] = s
    o = o + jnp.einsum('bid,bdv->biv', (q * per_chunk(qdec_ref[...])).astype(BF16),
                       jnp.concatenate(starts, axis=0), preferred_element_type=F32)
    for ci in range(nch):
        rows = pl.ds(ci * c, c)
        for k in range(nh):
            cols = slice(k * RET_DK, (k + 1) * RET_DK)
            o_ref[rows, cols] = _group_norm_gate(o[ci * nh + k], gn_ref[:, cols], g_ref[rows, cols]).astype(o_ref.dtype)


def _ret_decay_tables(c):
    lg = np.log(1.0 - 2.0 ** (-5.0 - np.arange(RET_HEADS, dtype=np.float64)))
    idx = np.arange(c, dtype=np.float64)
    diff = idx[:, None] - idx[None, :]
    intra = np.where(diff >= 0, np.exp(np.maximum(diff, 0.0)[None] * lg[:, None, None]), 0.0)
    ones = np.ones((1, 1, RET_DK))
    qdec = np.exp((idx[None, :, None] + 1.0) * lg[:, None, None]) * ones
    kdec = np.exp((c - 1.0 - idx[None, :, None]) * lg[:, None, None]) * ones
    cdec = np.exp(c * lg)[:, None, None] * np.ones((1, 8, RET_DK))
    return [jnp.asarray(t, F32) for t in (intra, qdec, kdec, cdec)]


def _ret_prompt(proj, mix, cos, sin, gn, batch, seq):
    c = RET_CHUNK
    r = RET_ROWS
    nc = seq // r
    w = RET_HEADS * RET_DK
    tables = _ret_decay_tables(c)
    col = lambda j: pl.BlockSpec((r, w), lambda b, i: (b * nc + i, j))
    whole = lambda t: pl.BlockSpec(t.shape, lambda b, i: (0,) * t.ndim)
    return pl.pallas_call(
        _ret_prompt_kernel,
        grid=(batch, nc),
        in_specs=[col(0), col(1), col(2), col(3),
                  pl.BlockSpec((r, RET_HALF), lambda b, i: (i, 0)),
                  pl.BlockSpec((r, RET_HALF), lambda b, i: (i, 0)),
                  *[whole(t) for t in tables],
                  pl.BlockSpec((1, w), lambda b, i: (0, 0)),
                  pl.BlockSpec(memory_space=pl.ANY)],
        out_specs=[pl.BlockSpec((r, w), lambda b, i: (b * nc + i, 0)),
                   pl.BlockSpec((1, RET_HEADS, RET_DK, RET_DK), lambda b, i: (b, 0, 0, 0))],
        out_shape=[jax.ShapeDtypeStruct(mix.shape, mix.dtype),
                   jax.ShapeDtypeStruct((batch, RET_HEADS, RET_DK, RET_DK), F32)],
        input_output_aliases={11: 0},
        compiler_params=_cparams(2),
        name="ret_prompt",
    )(proj, proj, proj, proj, cos, sin, *tables, gn.reshape(1, MIX_HALF), mix)


def _columns(x, n):
    t = x.shape[0]
    parts = [jnp.concatenate([x[:, i:i + 128]] * (128 // t), axis=0).T for i in range(0, n, 128)]
    return parts[0] if len(parts) == 1 else jnp.concatenate(parts, axis=0)


def _ret_sample_kernel(q_ref, k_ref, v_ref, g_ref, cos_ref, sin_ref, cdec_ref, gn_ref, s_ref, mix_ref,
                       o_ref, so_ref):
    del mix_ref
    cos = cos_ref[...]
    sin = sin_ref[...]
    q = _rotary(q_ref[...], cos, sin)
    k = _rotary(k_ref[...], cos, sin) * (RET_DK ** -0.5)
    v = v_ref[...]
    kt = _columns(k, RET_DK)
    gamma = cdec_ref[0, 0:1, :]
    qb = q.astype(BF16)
    tok = lax.broadcasted_iota(jnp.int32, q.shape, 0)
    o = jnp.zeros(q.shape, F32)
    for t in range(SAMPLE_TOK):
        s_new = s_ref[t, 0] * gamma + kt[:, t:t + 1] * v[t:t + 1, :]
        so_ref[t, 0] = s_new
        o = jnp.where(tok == t, jnp.dot(qb, s_new.astype(BF16), preferred_element_type=F32), o)
    o_ref[...] = _group_norm_gate(o, gn_ref[...], g_ref[...]).astype(o_ref.dtype)


def _ret_sample(proj, mix, state, cos, sin, gn, row0):
    nb = state.shape[0]
    t = SAMPLE_TOK
    r0 = row0 // t
    _, _, _, cdec = _ret_decay_tables(1)
    col = lambda off: pl.BlockSpec((t, RET_DK), lambda b, h: (r0 + b, off + h))
    st = pl.BlockSpec((t, 1, RET_DK, RET_DK), lambda b, h: (b, h, 0, 0))
    return pl.pallas_call(
        _ret_sample_kernel,
        grid=(nb // t, RET_HEADS),
        in_specs=[col(0), col(RET_HEADS), col(2 * RET_HEADS), col(3 * RET_HEADS),
                  pl.BlockSpec((1, RET_HALF), lambda b, h: (0, 0)),
                  pl.BlockSpec((1, RET_HALF), lambda b, h: (0, 0)),
                  pl.BlockSpec((1, 8, RET_DK), lambda b, h: (h, 0, 0)),
                  pl.BlockSpec((1, RET_DK), lambda b, h: (0, h)),
                  st,
                  pl.BlockSpec(memory_space=pl.ANY)],
        out_specs=[pl.BlockSpec((t, RET_DK), lambda b, h: (r0 + b, h)), st],
        out_shape=[jax.ShapeDtypeStruct(mix.shape, mix.dtype),
                   jax.ShapeDtypeStruct(state.shape, F32)],
        input_output_aliases={9: 0},
        compiler_params=_cparams(2),
        name="ret_sample",
    )(proj, proj, proj, proj, cos, sin, cdec, gn.reshape(1, MIX_HALF), state, mix)


def _hg_lower_bound(lb_ref, layer):
    x = lb_ref[...]
    e = jnp.exp(x - jnp.max(x, axis=0, keepdims=True))
    return jnp.sum(e[:layer + 1], axis=0, keepdims=True) / jnp.sum(e, axis=0, keepdims=True)


def _hg_gates(gq, gf, lb):
    f = lb + (1.0 - lb) * jax.nn.sigmoid(gf)
    return _silu(gq), 1.0 - f, f


def _hg_out(o, gn, gg):
    return o * lax.rsqrt(jnp.mean(o * o, axis=-1, keepdims=True) + EPS) * gn * _silu(gg)


def _split3(x):
    hi = x.astype(BF16)
    r = x - hi.astype(F32)
    mid = r.astype(BF16)
    lo = (r - mid.astype(F32)).astype(BF16)
    return hi, mid, lo


def _hg_block(qq, kk, lf, v, st, tri):
    c = HG_CHUNK
    nch = qq.shape[0] // c
    nh = qq.shape[1] // HG_DK
    w = nh * HG_DK
    hi, mid, lo = _split3(lf)
    b3 = jnp.dot(tri, jnp.concatenate([hi, mid, lo], axis=1), preferred_element_type=F32)
    b = b3[:, :w] + b3[:, w:2 * w] + b3[:, 2 * w:]
    split = lambda t: jnp.stack([t[ci * c:(ci + 1) * c, k * HG_DK:(k + 1) * HG_DK]
                                 for ci in range(nch) for k in range(nh)])
    q3, k3, bc = split(qq), split(kk), split(b)
    vb = split(v.astype(BF16))
    nb = nch * nh
    ri = lax.broadcasted_iota(jnp.int32, (1, c, c), 1)
    cj = lax.broadcasted_iota(jnp.int32, (1, c, c), 2)
    mid = bc[:, c // 2 - 1:c // 2]
    off_mid = bc - mid

    def att_mid_referenced():
        q_m = (q3 * jnp.exp(off_mid)).astype(BF16)
        k_m = (k3 * jnp.exp(-off_mid)).astype(BF16)
        return jnp.where(ri >= cj, jnp.einsum('cid,cjd->cij', q_m, k_m, preferred_element_type=F32), 0.0)

    def att_any_decay():
        refs = [jnp.zeros_like(bc[:, 0:1])] + [bc[:, i0 - 1:i0] for i0 in range(HG_SUB, c, HG_SUB)]
        span = jnp.concatenate([jnp.broadcast_to(r, (nb, HG_SUB, HG_DK)) for r in refs], axis=1) - bc
        q_t = (q3 * jnp.exp(-span)).astype(BF16)
        row = lax.broadcasted_iota(jnp.int32, (1, c, HG_DK), 1)
        sub_row = lax.broadcasted_iota(jnp.int32, (1, HG_SUB, c), 1)
        sub_col = lax.broadcasted_iota(jnp.int32, (1, HG_SUB, c), 2)
        rows = []
        for i0 in range(0, c, HG_SUB):
            att_i = jnp.zeros((nb, HG_SUB, c), F32)
            if i0 > 0:
                r = refs[i0 // HG_SUB]
                k_t = jnp.where(row < i0, k3 * jnp.exp(jnp.minimum(r - bc, 0.0)), 0.0).astype(BF16)
                att_i = jnp.einsum('cid,cjd->cij', q_t[:, i0:i0 + HG_SUB], k_t, preferred_element_type=F32)
            q_i = q3[:, i0:i0 + HG_SUB]
            b_i = bc[:, i0:i0 + HG_SUB]
            for j in range(HG_SUB):
                jj = i0 + j
                p = q_i * (k3[:, jj:jj + 1] * jnp.exp(jnp.minimum(b_i - bc[:, jj:jj + 1], 0.0)))
                s_j = jnp.sum(p, axis=2, keepdims=True)
                att_i = jnp.where((sub_col == jj) & (sub_row >= j), s_j, att_i)
            rows.append(att_i)
        return jnp.concatenate(rows, axis=1)

    att = lax.cond(jnp.max(jnp.abs(off_mid)) < HG_SAFE_SPAN, att_mid_referenced, att_any_decay).astype(BF16)
    o_intra = jnp.einsum('cij,cjv->civ', att, vb, preferred_element_type=F32)
    b_last = bc[:, c - 1:c]
    qe = (q3 * jnp.exp(bc)).astype(BF16)
    khat = (k3 * jnp.exp(b_last - bc)).astype(BF16)
    dec = jnp.exp(b_last)
    upd = jnp.einsum('cjv,cjd->cvd', vb, khat, preferred_element_type=F32)
    starts = []
    for ci in range(nch):
        heads = slice(ci * nh, (ci + 1) * nh)
        starts.append(st.astype(BF16))
        st = st * dec[heads] + upd[heads]
    o = o_intra + jnp.einsum('cid,cvd->civ', qe, jnp.concatenate(starts, axis=0), preferred_element_type=F32)
    o = jnp.concatenate([jnp.concatenate([o[ci * nh + k] for k in range(nh)], axis=1) for ci in range(nch)], axis=0)
    return o, st


def _hg_prompt_kernel(layer, gq_ref, gf_ref, gi_ref, gg_ref, lb_ref, gn_ref, tri_ref, mix_ref,
                      o_ref, s_ref, st_ref):
    del mix_ref

    @pl.when(pl.program_id(2) == 0)
    def _():
        st_ref[...] = jnp.zeros_like(st_ref)

    qq, kk, f = _hg_gates(gq_ref[...], gf_ref[...], _hg_lower_bound(lb_ref, layer))
    o, st = _hg_block(qq, kk, jnp.log(f), gi_ref[...], st_ref[...], tri_ref[...])
    for k in range(HG_NH):
        cols = slice(k * HG_DK, (k + 1) * HG_DK)
        o_ref[:, cols] = _hg_out(o[:, cols], gn_ref[:, cols], gg_ref[:, cols]).astype(o_ref.dtype)
    st_ref[...] = st

    @pl.when(pl.program_id(2) == pl.num_programs(2) - 1)
    def _():
        for k in range(HG_NH):
            s_ref[0, k] = st[k].T


def _hg_prompt(proj, mix, lb_raw, gn, layer, batch, seq):
    nt = seq // HG_ROWS
    c0 = MIX_HALF * 4 // HG_DK
    nh = HG_NH
    w = nh * HG_DK
    c0 = c0 // nh
    col = lambda off: pl.BlockSpec((HG_ROWS, w), lambda b, h, i: (b * nt + i, c0 + off // nh + h))
    tri = jnp.asarray(np.kron(np.eye(HG_ROWS // HG_CHUNK), np.tril(np.ones((HG_CHUNK, HG_CHUNK)))), BF16)
    return pl.pallas_call(
        functools.partial(_hg_prompt_kernel, layer),
        grid=(batch, HG_HEADS // nh, nt),
        in_specs=[col(0), col(HG_HEADS), col(2 * HG_HEADS), col(3 * HG_HEADS),
                  pl.BlockSpec((lb_raw.shape[0], w), lambda b, h, i: (0, h)),
                  pl.BlockSpec((1, w), lambda b, h, i: (0, h)),
                  pl.BlockSpec((HG_ROWS, HG_ROWS), lambda b, h, i: (0, 0)),
                  pl.BlockSpec(memory_space=pl.ANY)],
        out_specs=[pl.BlockSpec((HG_ROWS, w), lambda b, h, i: (b * nt + i, HG_HEADS // nh + h)),
                   pl.BlockSpec((1, nh, HG_DK, HG_DK), lambda b, h, i: (b, h, 0, 0))],
        out_shape=[jax.ShapeDtypeStruct(mix.shape, mix.dtype),
                   jax.ShapeDtypeStruct((batch, HG_HEADS, HG_DK, HG_DK), F32)],
        scratch_shapes=[pltpu.VMEM((nh, HG_DK, HG_DK), F32)],
        input_output_aliases={7: 0},
        compiler_params=_cparams(3),
        name="hgrn_prompt",
    )(proj, proj, proj, proj, lb_raw, gn.reshape(1, MIX_HALF), tri, mix)


HG_SAMPLE_NH = 4


def _hg_sample_kernel(layer, gq_ref, gf_ref, gi_ref, gg_ref, lb_ref, gn_ref, s_ref, mix_ref,
                      o_ref, so_ref):
    del mix_ref
    qq_all, kk_all, f_all = _hg_gates(gq_ref[...], gf_ref[...], _hg_lower_bound(lb_ref, layer))
    tok = lax.broadcasted_iota(jnp.int32, (SAMPLE_TOK, HG_DK), 0)
    for k in range(HG_SAMPLE_NH):
        cols = slice(k * HG_DK, (k + 1) * HG_DK)
        v = gi_ref[:, cols]
        kt = _columns(kk_all[:, cols], HG_DK)
        ft = _columns(f_all[:, cols], HG_DK)
        qb = qq_all[:, cols].astype(BF16)
        o = jnp.zeros((SAMPLE_TOK, HG_DK), F32)
        for t in range(SAMPLE_TOK):
            s_new = s_ref[t, k] * ft[:, t:t + 1] + kt[:, t:t + 1] * v[t:t + 1, :]
            so_ref[t, k] = s_new
            o = jnp.where(tok == t, jnp.dot(qb, s_new.astype(BF16), preferred_element_type=F32), o)
        o_ref[:, cols] = _hg_out(o, gn_ref[:, cols], gg_ref[:, cols]).astype(o_ref.dtype)


def _hg_sample(proj, mix, state, lb_raw, gn, layer, row0):
    nb = state.shape[0]
    t = SAMPLE_TOK
    nh = HG_SAMPLE_NH
    w = nh * HG_DK
    r0 = row0 // t
    c0 = MIX_HALF * 4 // w
    col = lambda off: pl.BlockSpec((t, w), lambda b, h: (r0 + b, c0 + off // nh + h))
    st = pl.BlockSpec((t, nh, HG_DK, HG_DK), lambda b, h: (b, h, 0, 0))
    return pl.pallas_call(
        functools.partial(_hg_sample_kernel, layer),
        grid=(nb // t, HG_HEADS // nh),
        in_specs=[col(0), col(HG_HEADS), col(2 * HG_HEADS), col(3 * HG_HEADS),
                  pl.BlockSpec((lb_raw.shape[0], w), lambda b, h: (0, h)),
                  pl.BlockSpec((1, w), lambda b, h: (0, h)),
                  st,
                  pl.BlockSpec(memory_space=pl.ANY)],
        out_specs=[pl.BlockSpec((t, w), lambda b, h: (r0 + b, HG_HEADS // nh + h)), st],
        out_shape=[jax.ShapeDtypeStruct(mix.shape, mix.dtype),
                   jax.ShapeDtypeStruct(state.shape, F32)],
        input_output_aliases={7: 0},
        compiler_params=_cparams(2),
        name="hgrn_sample",
    )(proj, proj, proj, proj, lb_raw, gn.reshape(1, MIX_HALF), state, mix)


def _cmul(ar, ai, br, bi):
    return ar * br - ai * bi, ar * bi + ai * br


def _s5_prep_kernel(lr_ref, li_ref, ldt_ref, brt_ref, bit_ref, ar_ref, ai_ref, bbr_ref, bbi_ref):
    lr = lr_ref[...]
    li = li_ref[...]
    dt = jnp.exp(ldt_ref[...])
    mag = jnp.exp(lr * dt)
    ar = mag * jnp.cos(li * dt)
    ai = mag * jnp.sin(li * dt)
    den = lr * lr + li * li
    cr = ((ar - 1.0) * lr + ai * li) / den
    ci = (ai * lr - (ar - 1.0) * li) / den
    brt = brt_ref[...]
    bit = bit_ref[...]
    bbr_ref[...] = cr * brt - ci * bit
    bbi_ref[...] = cr * bit + ci * brt
    ar_ref[...] = ar
    ai_ref[...] = ai


def _s5_prep(lam_re, lam_im, log_dt, b_re, b_im):
    n = S5_GROUPS * S5_STATE
    flat = lambda t: t.reshape(1, n)
    b_t = lambda t: jnp.transpose(t, (2, 0, 1)).reshape(S5_GROUP, n)
    ldt = jnp.repeat(log_dt, S5_STATE).reshape(1, n)
    return pl.pallas_call(
        _s5_prep_kernel,
        out_shape=[jax.ShapeDtypeStruct((1, n), F32)] * 2 + [jax.ShapeDtypeStruct((S5_GROUP, n), F32)] * 2,
        compiler_params=pltpu.CompilerParams(vmem_limit_bytes=VMEM_LIMIT_BYTES),
        name="s5_prep",
    )(flat(lam_re), flat(lam_im), ldt, b_t(b_re), b_t(b_im))


S5_T = 256
S5_HALF_ST = S5_BLK_ST // 2


def _s5_mats_kernel(batch, bbr_ref, bbi_ref, cr_ref, ci_ref, ar_ref, ai_ref, b_out, c_out, a_r_out, a_i_out):
    n = S5_HALF_ST
    hg = S5_BLK_GROUPS // 2
    rows = hg * S5_GROUP
    same_group = (lax.shift_right_logical(lax.broadcasted_iota(jnp.int32, (rows, n), 0), 4)
                  == lax.shift_right_logical(lax.broadcasted_iota(jnp.int32, (rows, n), 1), 6))
    place = lambda t: jnp.where(same_group, jnp.concatenate([t] * hg, axis=0), 0.0)
    for h in range(2):
        cols = slice(h * n, (h + 1) * n)
        r0 = slice(h * rows, (h + 1) * rows)
        b_out[0, r0, 0:n] = place(bbr_ref[:, cols]).astype(BF16)
        b_out[0, r0, n:2 * n] = place(bbi_ref[:, cols]).astype(BF16)
        c_out[0, 0:n, r0] = place(cr_ref[:, cols]).T.astype(BF16)
        c_out[0, n:2 * n, r0] = (-place(ci_ref[:, cols])).T.astype(BF16)
        a_r_out[0, h * batch:(h + 1) * batch, :] = jnp.broadcast_to(ar_ref[:, cols], (batch, n))
        a_i_out[0, h * batch:(h + 1) * batch, :] = jnp.broadcast_to(ai_ref[:, cols], (batch, n))


def _s5_mats(bbr, bbi, c_re, c_im, a_r, a_i, batch):
    assert S5_GROUP == 16 and S5_STATE == 64
    n_all = S5_GROUPS * S5_STATE
    c_t = lambda t: jnp.transpose(t, (1, 0, 2)).reshape(S5_GROUP, n_all)
    blk = lambda r: pl.BlockSpec((r, S5_BLK_ST), lambda j: (0, j))
    out3 = lambda r, c: pl.BlockSpec((1, r, c), lambda j: (j, 0, 0))
    return pl.pallas_call(
        functools.partial(_s5_mats_kernel, batch),
        grid=(S5_NBLK,),
        in_specs=[blk(S5_GROUP)] * 4 + [blk(1)] * 2,
        out_specs=[out3(S5_BLK_CH, S5_BLK_ST), out3(S5_BLK_ST, S5_BLK_CH),
                   out3(2 * batch, S5_HALF_ST), out3(2 * batch, S5_HALF_ST)],
        out_shape=[jax.ShapeDtypeStruct((S5_NBLK, S5_BLK_CH, S5_BLK_ST), BF16),
                   jax.ShapeDtypeStruct((S5_NBLK, S5_BLK_ST, S5_BLK_CH), BF16),
                   jax.ShapeDtypeStruct((S5_NBLK, 2 * batch, S5_HALF_ST), F32),
                   jax.ShapeDtypeStruct((S5_NBLK, 2 * batch, S5_HALF_ST), F32)],
        compiler_params=_cparams(1),
        name="s5_mats",
    )(bbr, bbi, c_t(c_re), c_t(c_im), a_r, a_i)


S5_NB = 2


def _s5_tb_kernel(u_ref, b_ref, c_ref, ar_ref, ai_ref, d_ref, z_in_ref, z_ref, hr_out, hi_out,
                  x_ref, hb_ref, hr_ref, hi_ref):
    del z_in_ref
    i = pl.program_id(1)
    n = S5_HALF_ST
    ch = S5_BLK_CH
    t_steps = u_ref.shape[0] // 4
    half_rows = 4 * t_steps

    @pl.when(i == 0)
    def _():
        hr_ref[...] = jnp.zeros_like(hr_ref)
        hi_ref[...] = jnp.zeros_like(hi_ref)

    first4 = lax.broadcasted_iota(jnp.int32, (1, 8, ch), 1) < 4
    half0 = lax.broadcasted_iota(jnp.int32, (1, 8, ch), 2) < ch // 2
    for k in range(S5_NB):
        u3 = u_ref[:, k * ch:(k + 1) * ch].reshape(t_steps // 2, 8, ch)
        r3 = pltpu.roll(u3, 4, 1)
        even = jnp.where(first4, jnp.where(half0, u3, 0.0), jnp.where(half0, 0.0, r3))
        odd = jnp.where(first4, jnp.where(half0, r3, 0.0), jnp.where(half0, 0.0, u3))
        lhs = jnp.stack([even, odd], axis=1).reshape(8 * t_steps, ch).astype(BF16)
        for r0 in (0, half_rows):
            x_ref[k, r0:r0 + half_rows, :] = jnp.dot(lhs[r0:r0 + half_rows], b_ref[k],
                                                     preferred_element_type=F32)
    a = [(ar_ref[k], ai_ref[k]) for k in range(S5_NB)]

    def advance(t, hs):
        rows = pl.ds(pl.multiple_of(t * 8, 8), 8)
        out = []
        for k in range(S5_NB):
            dr, di = _cmul(*a[k], *hs[k])
            out.append((dr + x_ref[k, rows, 0:n], di + x_ref[k, rows, n:2 * n]))
        return tuple(out)

    def emit(t2, hs):
        h1 = advance(2 * t2, hs)
        h2 = advance(2 * t2 + 1, h1)
        rows = pl.ds(pl.multiple_of(t2 * 16, 16), 16)
        for k in range(S5_NB):
            hb_ref[k, rows, 0:n] = jnp.concatenate([h1[k][0], h2[k][0]], axis=0).astype(BF16)
            hb_ref[k, rows, n:2 * n] = jnp.concatenate([h1[k][1], h2[k][1]], axis=0).astype(BF16)
        return h2

    hs = lax.fori_loop(0, t_steps // 2, emit, tuple((hr_ref[k], hi_ref[k]) for k in range(S5_NB)), unroll=2)
    for k in range(S5_NB):
        hr_ref[k] = hs[k][0]
        hi_ref[k] = hs[k][1]

    pick = lambda yv: jnp.where(half0, yv, pltpu.roll(yv, 4, 1))
    for k in range(S5_NB):
        y2 = jnp.concatenate([jnp.dot(hb_ref[k, r0:r0 + half_rows, :], c_ref[k], preferred_element_type=F32)
                              for r0 in (0, half_rows)], axis=0)
        y4 = y2.reshape(t_steps // 2, 2, 8, ch)
        y = jnp.where(first4, pick(y4[:, 0]), pltpu.roll(pick(y4[:, 1]), 4, 1)).reshape(4 * t_steps, ch)
        cols = slice(k * ch, (k + 1) * ch)
        z_ref[:, cols] = jax.nn.gelu(y + d_ref[:, cols] * u_ref[:, cols]).astype(z_ref.dtype)

    @pl.when(i == pl.num_programs(1) - 1)
    def _():
        for k in range(S5_NB):
            for h in range(2):
                cols = slice((2 * k + h) * n, (2 * k + h + 1) * n)
                hr_out[:, cols] = hs[k][0][4 * h:4 * h + 4, :]
                hi_out[:, cols] = hs[k][1][4 * h:4 * h + 4, :]


def _s5_prompt_tb(u, z, mats, d, batch, seq):
    assert batch == 4
    b_m, c_m, a_r, a_i = mats
    rows = batch * S5_T
    nt = seq // S5_T
    nb = S5_NB
    blk3 = lambda r, c: pl.BlockSpec((nb, r, c), lambda j, i: (j, 0, 0))
    st_spec = pl.BlockSpec((batch, nb * S5_BLK_ST), lambda j, i: (0, j))
    st_shape = jax.ShapeDtypeStruct((batch, S5_GROUPS * S5_STATE), F32)
    return pl.pallas_call(
        _s5_tb_kernel,
        grid=(S5_NBLK // nb, nt),
        in_specs=[pl.BlockSpec((rows, nb * S5_BLK_CH), lambda j, i: (i, j)),
                  blk3(S5_BLK_CH, S5_BLK_ST), blk3(S5_BLK_ST, S5_BLK_CH),
                  blk3(2 * batch, S5_HALF_ST), blk3(2 * batch, S5_HALF_ST),
                  pl.BlockSpec((1, nb * S5_BLK_CH), lambda j, i: (0, j)),
                  pl.BlockSpec(memory_space=pl.ANY)],
        out_specs=[pl.BlockSpec((rows, nb * S5_BLK_CH), lambda j, i: (i, j)), st_spec, st_spec],
        out_shape=[jax.ShapeDtypeStruct(z.shape, z.dtype), st_shape, st_shape],
        scratch_shapes=[pltpu.VMEM((nb, 2 * rows, S5_BLK_ST), F32), pltpu.VMEM((nb, 2 * rows, S5_BLK_ST), BF16),
                        pltpu.VMEM((nb, 2 * batch, S5_HALF_ST), F32), pltpu.VMEM((nb, 2 * batch, S5_HALF_ST), F32)],
        input_output_aliases={6: 0},
        compiler_params=_cparams(2),
        name="s5_prompt",
    )(u, b_m, c_m, a_r, a_i, d.reshape(1, D_MODEL), z)


def _s5_sample_kernel(u_ref, b_ref, c_ref, ar_ref, ai_ref, d_ref, h0r_ref, h0i_ref,
                      z_in_ref, z_ref, hr_out, hi_out):
    del z_in_ref
    n = S5_HALF_ST
    u = u_ref[...]
    nb = u.shape[0]
    half0 = lax.broadcasted_iota(jnp.int32, u.shape, 1) < S5_BLK_CH // 2
    lhs = jnp.concatenate([jnp.where(half0, u, 0.0), jnp.where(half0, 0.0, u)], axis=0).astype(BF16)
    x = jnp.dot(lhs, b_ref[0], preferred_element_type=F32)
    halves = []
    for h in range(2):
        rows = slice(h * nb, (h + 1) * nb)
        cols = slice(h * n, (h + 1) * n)
        row_a = slice(h * (ar_ref.shape[1] // 2), h * (ar_ref.shape[1] // 2) + 1)
        dr, di = _cmul(ar_ref[0, row_a, :], ai_ref[0, row_a, :], h0r_ref[:, cols], h0i_ref[:, cols])
        hr = x[rows, 0:n] + dr
        hi = x[rows, n:2 * n] + di
        hr_out[:, cols] = hr
        hi_out[:, cols] = hi
        halves.append(jnp.concatenate([hr, hi], axis=1))
    y2 = jnp.dot(jnp.concatenate(halves, axis=0).astype(BF16), c_ref[0], preferred_element_type=F32)
    y = jnp.where(half0, y2[0:nb], y2[nb:2 * nb])
    z_ref[...] = jax.nn.gelu(y + d_ref[...] * u).astype(z_ref.dtype)


def _s5_sample(u, z, mats, d, h0r, h0i, row0):
    b_m, c_m, a_r, a_i = mats
    nb = h0r.shape[0]
    r0 = row0 // nb
    blk3 = lambda r, c: pl.BlockSpec((1, r, c), lambda j: (j, 0, 0))
    st_spec = pl.BlockSpec((nb, S5_BLK_ST), lambda j: (0, j))
    st_shape = jax.ShapeDtypeStruct(h0r.shape, F32)
    return pl.pallas_call(
        _s5_sample_kernel,
        grid=(S5_NBLK,),
        in_specs=[pl.BlockSpec((nb, S5_BLK_CH), lambda j: (r0, j)),
                  blk3(S5_BLK_CH, S5_BLK_ST), blk3(S5_BLK_ST, S5_BLK_CH),
                  blk3(a_r.shape[1], S5_HALF_ST), blk3(a_r.shape[1], S5_HALF_ST),
                  pl.BlockSpec((1, S5_BLK_CH), lambda j: (0, j)),
                  st_spec, st_spec,
                  pl.BlockSpec(memory_space=pl.ANY)],
        out_specs=[pl.BlockSpec((nb, S5_BLK_CH), lambda j: (r0, j)), st_spec, st_spec],
        out_shape=[jax.ShapeDtypeStruct(z.shape, z.dtype), st_shape, st_shape],
        input_output_aliases={8: 0},
        compiler_params=_cparams(1),
        name="s5_sample",
    )(u, b_m, c_m, a_r, a_i, d.reshape(1, D_MODEL), h0r, h0i, z)


TB_T = 128
LANES = 128


def _interleave_norm_kernel(n_steps, n_sample, batch, *refs):
    x_refs = refs[:batch]
    xs_ref, g_ref, xt_ref, ut_ref, slab_ref = refs[batch:]
    i = pl.program_id(0)
    d = xt_ref.shape[1]

    @pl.when(i < n_steps)
    def _():
        for dst_ref, norm in ((xt_ref, False), (ut_ref, True)):
            for b, x_ref in enumerate(x_refs):
                v = x_ref[...]
                if norm:
                    v = _rms(v, g_ref[...])
                for s in range(d // LANES):
                    slab_ref[s, pl.ds(b, TB_T, stride=batch), :] = v[:, s * LANES:(s + 1) * LANES]
            for s in range(d // LANES):
                dst_ref[:, s * LANES:(s + 1) * LANES] = slab_ref[s]

    @pl.when(i >= n_steps)
    def _():
        v = xs_ref[...]
        xt_ref[0:n_sample, :] = v
        ut_ref[0:n_sample, :] = _rms(v, g_ref[...])


def _interleave_norm(x, g, batch, seq):
    m, d = x.shape
    n_steps = seq // TB_T
    ms = m - batch * seq
    rows = batch * TB_T
    per_batch = lambda b: pl.BlockSpec((TB_T, d), lambda i: (b * n_steps + jnp.minimum(i, n_steps - 1), 0))
    out = pl.BlockSpec((rows, d), lambda i: (i, 0))
    return pl.pallas_call(
        functools.partial(_interleave_norm_kernel, n_steps, ms, batch),
        grid=(n_steps + 1,),
        in_specs=[per_batch(b) for b in range(batch)]
        + [pl.BlockSpec((ms, d), lambda i: (batch * seq // ms, 0)), pl.BlockSpec((1, d), lambda i: (0, 0))],
        out_specs=[out, out],
        out_shape=[jax.ShapeDtypeStruct((m, d), F32)] * 2,
        scratch_shapes=[pltpu.VMEM((d // LANES, rows, LANES), F32)],
        compiler_params=_cparams(1),
        name="interleave_norm",
    )(*([x] * (batch + 1)), g.reshape(1, d))


def _deinterleave_norm_kernel(n_steps, n_sample, batch, x_ref, g_ref, y_ref, ys_ref, slab_ref):
    i = pl.program_id(0)
    d = x_ref.shape[1]

    @pl.when(i < n_steps)
    def _():
        y = _rms(x_ref[...], g_ref[...])
        for s in range(d // LANES):
            slab_ref[s] = y[:, s * LANES:(s + 1) * LANES]
        for b in range(batch):
            for s in range(d // LANES):
                y_ref[b, :, s * LANES:(s + 1) * LANES] = slab_ref[s, pl.ds(b, TB_T, stride=batch), :]

    @pl.when(i >= n_steps)
    def _():
        ys_ref[...] = _rms(x_ref[0:n_sample, :], g_ref[...])


def _deinterleave_norm(x, g, batch, seq):
    m, d = x.shape
    n_steps = seq // TB_T
    ms = m - batch * seq
    rows = batch * TB_T
    return pl.pallas_call(
        functools.partial(_deinterleave_norm_kernel, n_steps, ms, batch),
        grid=(n_steps + 1,),
        in_specs=[pl.BlockSpec((rows, d), lambda i: (i, 0)), pl.BlockSpec((1, d), lambda i: (0, 0))],
        out_specs=[pl.BlockSpec((batch, TB_T, d), lambda i: (0, jnp.minimum(i, n_steps - 1), 0)),
                   pl.BlockSpec((ms, d), lambda i: (0, 0))],
        out_shape=[jax.ShapeDtypeStruct((batch, seq, d), F32), jax.ShapeDtypeStruct((ms, d), F32)],
        scratch_shapes=[pltpu.VMEM((d // LANES, rows, LANES), F32)],
        compiler_params=_cparams(1),
        name="deinterleave_norm",
    )(x, g.reshape(1, d))


def _rope_tables(pos):
    inv = ROPE_BASE ** (-jnp.arange(RET_HALF, dtype=F32) / RET_HALF)
    ang = pos[:, None] * inv[None, :]
    return jnp.cos(ang), jnp.sin(ang)


def _ffn(x, norm_g, wg, wu, wd, layer):
    h = _rmsnorm(x, norm_g[layer], [BF16])[0]
    t = _matmul(h, [wg, wu], _epi_swiglu, BF16, "ffn_gate_up", layer=layer)
    return _matmul(t, [wd], _epi_id, F32, "ffn_down", res=x, layer=layer)


def kernel(x_prompt, x_sample, state_ret, state_hgrn, state_s5_re, state_s5_im, attn_norm_g, w_in, ret_gn_g, hg_lb, hg_gn_g, w_out, ssm_norm_g, s5_lam_re, s5_lam_im, s5_log_dt, s5_b_re, s5_b_im, s5_c_re, s5_c_im, s5_d, w_glu_a, w_glu_b, ffn_norm_g, w_ffn_gate, w_ffn_up, w_ffn_down, final_norm_g):
    one = lambda t: t.reshape(t.shape[1:])
    (state_ret, state_hgrn, state_s5_re, state_s5_im, attn_norm_g, w_in, ret_gn_g, hg_gn_g, w_out, ssm_norm_g,
     s5_lam_re, s5_lam_im, s5_log_dt, s5_b_re, s5_b_im, s5_c_re, s5_c_im, s5_d, w_glu_a, w_glu_b) = map(one, (
         state_ret, state_hgrn, state_s5_re, state_s5_im, attn_norm_g, w_in, ret_gn_g, hg_gn_g, w_out, ssm_norm_g,
         s5_lam_re, s5_lam_im, s5_log_dt, s5_b_re, s5_b_im, s5_c_re, s5_c_im, s5_d, w_glu_a, w_glu_b))
    cos_p, sin_p = _rope_tables(jnp.arange(SEQ, dtype=F32))
    cos_s, sin_s = _rope_tables(jnp.full((1,), float(PAST_LEN), F32))

    h, x = _rmsnorm_stack(x_prompt.reshape(N_PROMPT, D_MODEL), x_sample.reshape(DEC_BATCH, D_MODEL), attn_norm_g)
    proj = _matmul(h, [w_in], _epi_id, F32, "w_in")
    mix, ret_p = _ret_prompt(proj, h, cos_p, sin_p, ret_gn_g, BATCH, SEQ)
    mix, hg_p = _hg_prompt(proj, mix, hg_lb, hg_gn_g, 0, BATCH, SEQ)
    mix, ret_s = _ret_sample(proj, mix, state_ret, cos_s, sin_s, ret_gn_g, N_PROMPT)
    mix, hg_s = _hg_sample(proj, mix, state_hgrn, hg_lb, hg_gn_g, 0, N_PROMPT)
    x = _matmul(mix, [w_out], _epi_id, F32, "w_out", res=x)
    z = mix
    x = _ffn(x, ffn_norm_g, w_ffn_gate, w_ffn_up, w_ffn_down, 0)

    x, u = _interleave_norm(x, ssm_norm_g, BATCH, SEQ)
    p_r, p_i, bbr, bbi = _s5_prep(s5_lam_re, s5_lam_im, s5_log_dt, s5_b_re, s5_b_im)
    mats = _s5_mats(bbr, bbi, s5_c_re, s5_c_im, p_r, p_i, BATCH)
    z, s5r_p, s5i_p = _s5_prompt_tb(u, z, mats, s5_d, BATCH, SEQ)
    n_st = S5_GROUPS * S5_STATE
    z, s5r_s, s5i_s = _s5_sample(u, z, mats, s5_d, state_s5_re.reshape(DEC_BATCH, n_st),
                                 state_s5_im.reshape(DEC_BATCH, n_st), N_PROMPT)
    x = _matmul(z, [w_glu_a, w_glu_b], _epi_glu, F32, "glu", res=x)
    x = _ffn(x, ffn_norm_g, w_ffn_gate, w_ffn_up, w_ffn_down, 1)

    y_p, y_s = _deinterleave_norm(x, final_norm_g, BATCH, SEQ)
    st = lambda t, b: t.reshape(1, b, S5_GROUPS, S5_STATE)
    return (y_p, y_s.reshape(DEC_BATCH, 1, D_MODEL),
            ret_p[None], ret_s[None], hg_p[None], hg_s[None],
            st(s5r_p, BATCH), st(s5i_p, BATCH), st(s5r_s, DEC_BATCH), st(s5i_s, DEC_BATCH))
```

```python
import functools

import numpy as np
import jax
import jax.numpy as jnp
from jax import lax
from jax.experimental import pallas as pl
from jax.experimental.pallas import tpu as pltpu

F32 = jnp.float32
BF16 = jnp.bfloat16

D_MODEL = 2048
BATCH = 4
SEQ = 2048
DEC_BATCH = 128
PAST_LEN = 16384
N_PROMPT = BATCH * SEQ
N_ROWS = N_PROMPT + DEC_BATCH
MIX_HALF = D_MODEL // 2
RET_HEADS = 4
RET_DK = MIX_HALF // RET_HEADS
RET_HALF = RET_DK // 2
RET_CHUNK = 128
RET_ROWS = 512
HG_HEADS = 8
HG_DK = MIX_HALF // HG_HEADS
HG_CHUNK = 64
HG_SUB = 16
HG_ROWS = 256
HG_NH = 8
HG_SAFE_SPAN = 40.0
S5_GROUP = 16
S5_GROUPS = D_MODEL // S5_GROUP
S5_STATE = 64
S5_BLK_GROUPS = 16
S5_BLK_CH = S5_BLK_GROUPS * S5_GROUP
S5_BLK_ST = S5_BLK_GROUPS * S5_STATE
S5_NBLK = S5_GROUPS // S5_BLK_GROUPS
ROPE_BASE = 10000.0
EPS = 1e-6
SAMPLE_TOK = 32

VMEM_LIMIT_BYTES = 56 * 1024 * 1024

MM_TILES = {"w_in": (1664, 1024), "w_out": (832, 1024), "ffn_gate_up": (1664, 512),
            "ffn_down": (832, 512), "glu": (1664, 512)}
MM_SUB_TILES = 2


def _cparams(n_axes):
    return pltpu.CompilerParams(dimension_semantics=("arbitrary",) * n_axes,
                                vmem_limit_bytes=VMEM_LIMIT_BYTES)


def _sigmoid(x):
    return 0.5 * jnp.tanh(0.5 * x) + 0.5


def _silu(x):
    return x * _sigmoid(x)


def _rmsnorm_kernel(x_ref, g_ref, *o_refs):
    x = x_ref[...]
    y = x * lax.rsqrt(jnp.mean(x * x, axis=-1, keepdims=True) + EPS) * g_ref[...]
    for o_ref in o_refs:
        o_ref[...] = y.astype(o_ref.dtype)


def _rmsnorm(x, g, dtypes, tm=1664):
    m, d = x.shape
    return pl.pallas_call(
        _rmsnorm_kernel,
        grid=(m // tm,),
        in_specs=[pl.BlockSpec((tm, d), lambda i: (i, 0)),
                  pl.BlockSpec((1, d), lambda i: (0, 0))],
        out_specs=[pl.BlockSpec((tm, d), lambda i: (i, 0)) for _ in dtypes],
        out_shape=[jax.ShapeDtypeStruct((m, d), dt) for dt in dtypes],
        compiler_params=_cparams(1),
        name="rmsnorm",
    )(x, g.reshape(1, d))


ROW_TILE = 1024


def _rms(x, g):
    return x * lax.rsqrt(jnp.mean(x * x, axis=-1, keepdims=True) + EPS) * g


def _rmsnorm_stack_kernel(n_prompt_tiles, n_sample, xp_ref, xs_ref, g_ref, h_ref, x_ref):
    i = pl.program_id(0)

    @pl.when(i < n_prompt_tiles)
    def _():
        x = xp_ref[...]
        x_ref[...] = x
        h_ref[...] = _rms(x, g_ref[...]).astype(h_ref.dtype)

    @pl.when(i >= n_prompt_tiles)
    def _():
        x = xs_ref[...]
        x_ref[0:n_sample, :] = x
        h_ref[0:n_sample, :] = _rms(x, g_ref[...]).astype(h_ref.dtype)


def _rmsnorm_stack(x_prompt, x_sample, g):
    (mp, d), ms = x_prompt.shape, x_sample.shape[0]
    tm = ROW_TILE
    npt = mp // tm
    m = mp + ms
    return pl.pallas_call(
        functools.partial(_rmsnorm_stack_kernel, npt, ms),
        grid=(npt + 1,),
        in_specs=[pl.BlockSpec((tm, d), lambda i: (jnp.minimum(i, npt - 1), 0)),
                  pl.BlockSpec((ms, d), lambda i: (0, 0)),
                  pl.BlockSpec((1, d), lambda i: (0, 0))],
        out_specs=[pl.BlockSpec((tm, d), lambda i: (i, 0))] * 2,
        out_shape=[jax.ShapeDtypeStruct((m, d), BF16), jax.ShapeDtypeStruct((m, d), F32)],
        compiler_params=_cparams(1),
        name="rmsnorm_stack",
    )(x_prompt, x_sample, g.reshape(1, d))


def _mm_kernel(n_w, epilogue, has_res, a_ref, *refs):
    w_refs = refs[:n_w]
    res_ref = refs[n_w] if has_res else None
    o_ref = refs[n_w + has_res]
    wb_refs = refs[n_w + has_res + 1:]

    @pl.when(pl.program_id(1) == 0)
    def _():
        for w_ref, wb_ref in zip(w_refs, wb_refs):
            wb_ref[...] = w_ref[...].astype(BF16)

    sub = a_ref.shape[0] // MM_SUB_TILES
    for r in range(MM_SUB_TILES):
        rows = pl.ds(r * sub, sub)
        a = a_ref[rows, :]
        y = epilogue(*[jnp.dot(a, wb_ref[...], preferred_element_type=F32) for wb_ref in wb_refs])
        if has_res:
            y = res_ref[rows, :] + y
        o_ref[rows, :] = y.astype(o_ref.dtype)


def _matmul(a, ws, epilogue, out_dtype, name, res=None, layer=None):
    tm, tn = MM_TILES[name]
    m, k = a.shape
    n = ws[0].shape[-1]
    in_specs = [pl.BlockSpec((tm, k), lambda j, i: (i, 0))]
    if layer is None:
        in_specs += [pl.BlockSpec((k, tn), lambda j, i: (0, j)) for _ in ws]
    else:
        in_specs += [pl.BlockSpec((None, k, tn), lambda j, i: (layer, 0, j)) for _ in ws]
    args = [a, *ws]
    if res is not None:
        in_specs.append(pl.BlockSpec((tm, tn), lambda j, i: (i, j)))
        args.append(res)
    return pl.pallas_call(
        functools.partial(_mm_kernel, len(ws), epilogue, res is not None),
        grid=(n // tn, m // tm),
        in_specs=in_specs,
        out_specs=pl.BlockSpec((tm, tn), lambda j, i: (i, j)),
        out_shape=jax.ShapeDtypeStruct((m, n), out_dtype),
        scratch_shapes=[pltpu.VMEM((k, tn), BF16) for _ in ws],
        compiler_params=_cparams(2),
        name=name,
    )(*args)


def _epi_id(y):
    return y


def _epi_swiglu(g, u):
    return _silu(g) * u


def _epi_glu(a, b):
    return a * _sigmoid(b)


def _rotary(x, cos, sin):
    x1 = x[:, :RET_HALF]
    x2 = x[:, RET_HALF:]
    return jnp.concatenate([x1 * cos - x2 * sin, x1 * sin + x2 * cos], axis=1)


def _group_norm_gate(o, gn, g):
    mu = jnp.mean(o, axis=-1, keepdims=True)
    d = o - mu
    var = jnp.mean(d * d, axis=-1, keepdims=True)
    return d * lax.rsqrt(var + EPS) * gn * _silu(g)


def _ret_prompt_kernel(q_ref, k_ref, v_ref, g_ref, cos_ref, sin_ref, intra_ref, qdec_ref, kdec_ref,
                       cdec_ref, gn_ref, mix_ref, o_ref, s_ref):
    del mix_ref

    @pl.when(pl.program_id(1) == 0)
    def _():
        s_ref[...] = jnp.zeros_like(s_ref)

    c = RET_CHUNK
    nch = RET_ROWS // c
    nh = RET_HEADS

    def split(ref, rotate):
        out = []
        for ci in range(nch):
            rows = pl.ds(ci * c, c)
            for k in range(nh):
                x = ref[rows, k * RET_DK:(k + 1) * RET_DK]
                out.append(_rotary(x, cos_ref[rows, :], sin_ref[rows, :]) if rotate else x)
        return jnp.stack(out)

    per_chunk = lambda t: jnp.concatenate([t] * nch, axis=0)
    q = split(q_ref, True)
    k = split(k_ref, True) * (RET_DK ** -0.5)
    v = split(v_ref, False).astype(BF16)
    att = jnp.einsum('bid,bjd->bij', q.astype(BF16), k.astype(BF16),
                     preferred_element_type=F32) * per_chunk(intra_ref[...])
    o = jnp.einsum('bij,bjv->biv', att.astype(BF16), v, preferred_element_type=F32)
    kv = jnp.einsum('bjd,bjv->bdv', (k * per_chunk(kdec_ref[...])).astype(BF16), v, preferred_element_type=F32)
    s = s_ref[0]
    cdec = cdec_ref[:, 0:1, :]
    starts = []
    for ci in range(nch):
        starts.append(s.astype(BF16))
        s = s * cdec + kv[ci * nh:(ci + 1) * nh]
    s_ref[0] = s
    o = o + jnp.einsum('bid,bdv->biv', (q * per_chunk(qdec_ref[...])).astype(BF16),
                       jnp.concatenate(starts, axis=0), preferred_element_type=F32)
    for ci in range(nch):
        rows = pl.ds(ci * c, c)
        for k in range(nh):
            cols = slice(k * RET_DK, (k + 1) * RET_DK)
            o_ref[rows, cols] = _group_norm_gate(o[ci * nh + k], gn_ref[:, cols], g_ref[rows, cols]).astype(o_ref.dtype)


def _ret_decay_tables(c):
    lg = np.log(1.0 - 2.0 ** (-5.0 - np.arange(RET_HEADS, dtype=np.float64)))
    idx = np.arange(c, dtype=np.float64)
    diff = idx[:, None] - idx[None, :]
    intra = np.where(diff >= 0, np.exp(np.maximum(diff, 0.0)[None] * lg[:, None, None]), 0.0)
    ones = np.ones((1, 1, RET_DK))
    qdec = np.exp((idx[None, :, None] + 1.0) * lg[:, None, None]) * ones
    kdec = np.exp((c - 1.0 - idx[None, :, None]) * lg[:, None, None]) * ones
    cdec = np.exp(c * lg)[:, None, None] * np.ones((1, 8, RET_DK))
    return [jnp.asarray(t, F32) for t in (intra, qdec, kdec, cdec)]


def _ret_prompt(proj, mix, cos, sin, gn, batch, seq):
    c = RET_CHUNK
    r = RET_ROWS
    nc = seq // r
    w = RET_HEADS * RET_DK
    tables = _ret_decay_tables(c)
    col = lambda j: pl.BlockSpec((r, w), lambda b, i: (b * nc + i, j))
    whole = lambda t: pl.BlockSpec(t.shape, lambda b, i: (0,) * t.ndim)
    return pl.pallas_call(
        _ret_prompt_kernel,
        grid=(batch, nc),
        in_specs=[col(0), col(1), col(2), col(3),
                  pl.BlockSpec((r, RET_HALF), lambda b, i: (i, 0)),
                  pl.BlockSpec((r, RET_HALF), lambda b, i: (i, 0)),
                  *[whole(t) for t in tables],
                  pl.BlockSpec((1, w), lambda b, i: (0, 0)),
                  pl.BlockSpec(memory_space=pl.ANY)],
        out_specs=[pl.BlockSpec((r, w), lambda b, i: (b * nc + i, 0)),
                   pl.BlockSpec((1, RET_HEADS, RET_DK, RET_DK), lambda b, i: (b, 0, 0, 0))],
        out_shape=[jax.ShapeDtypeStruct(mix.shape, mix.dtype),
                   jax.ShapeDtypeStruct((batch, RET_HEADS, RET_DK, RET_DK), F32)],
        input_output_aliases={11: 0},
        compiler_params=_cparams(2),
        name="ret_prompt",
    )(proj, proj, proj, proj, cos, sin, *tables, gn.reshape(1, MIX_HALF), mix)


def _columns(x, n):
    t = x.shape[0]
    parts = [jnp.concatenate([x[:, i:i + 128]] * (128 // t), axis=0).T for i in range(0, n, 128)]
    return parts[0] if len(parts) == 1 else jnp.concatenate(parts, axis=0)


def _ret_sample_kernel(q_ref, k_ref, v_ref, g_ref, cos_ref, sin_ref, cdec_ref, gn_ref, s_ref, mix_ref,
                       o_ref, so_ref):
    del mix_ref
    cos = cos_ref[...]
    sin = sin_ref[...]
    q = _rotary(q_ref[...], cos, sin)
    k = _rotary(k_ref[...], cos, sin) * (RET_DK ** -0.5)
    v = v_ref[...]
    kt = _columns(k, RET_DK)
    gamma = cdec_ref[0, 0:1, :]
    qb = q.astype(BF16)
    tok = lax.broadcasted_iota(jnp.int32, q.shape, 0)
    o = jnp.zeros(q.shape, F32)
    for t in range(SAMPLE_TOK):
        s_new = s_ref[t, 0] * gamma + kt[:, t:t + 1] * v[t:t + 1, :]
        so_ref[t, 0] = s_new
        o = jnp.where(tok == t, jnp.dot(qb, s_new.astype(BF16), preferred_element_type=F32), o)
    o_ref[...] = _group_norm_gate(o, gn_ref[...], g_ref[...]).astype(o_ref.dtype)


def _ret_sample(proj, mix, state, cos, sin, gn, row0):
    nb = state.shape[0]
    t = SAMPLE_TOK
    r0 = row0 // t
    _, _, _, cdec = _ret_decay_tables(1)
    col = lambda off: pl.BlockSpec((t, RET_DK), lambda b, h: (r0 + b, off + h))
    st = pl.BlockSpec((t, 1, RET_DK, RET_DK), lambda b, h: (b, h, 0, 0))
    return pl.pallas_call(
        _ret_sample_kernel,
        grid=(nb // t, RET_HEADS),
        in_specs=[col(0), col(RET_HEADS), col(2 * RET_HEADS), col(3 * RET_HEADS),
                  pl.BlockSpec((1, RET_HALF), lambda b, h: (0, 0)),
                  pl.BlockSpec((1, RET_HALF), lambda b, h: (0, 0)),
                  pl.BlockSpec((1, 8, RET_DK), lambda b, h: (h, 0, 0)),
                  pl.BlockSpec((1, RET_DK), lambda b, h: (0, h)),
                  st,
                  pl.BlockSpec(memory_space=pl.ANY)],
        out_specs=[pl.BlockSpec((t, RET_DK), lambda b, h: (r0 + b, h)), st],
        out_shape=[jax.ShapeDtypeStruct(mix.shape, mix.dtype),
                   jax.ShapeDtypeStruct(state.shape, F32)],
        input_output_aliases={9: 0},
        compiler_params=_cparams(2),
        name="ret_sample",
    )(proj, proj, proj, proj, cos, sin, cdec, gn.reshape(1, MIX_HALF), state, mix)


def _hg_lower_bound(lb_ref, layer):
    x = lb_ref[...]
    e = jnp.exp(x - jnp.max(x, axis=0, keepdims=True))
    return jnp.sum(e[:layer + 1], axis=0, keepdims=True) / jnp.sum(e, axis=0, keepdims=True)


def _hg_gates(gq, gf, lb):
    f = lb + (1.0 - lb) * jax.nn.sigmoid(gf)
    return _silu(gq), 1.0 - f, f


def _hg_out(o, gn, gg):
    return o * lax.rsqrt(jnp.mean(o * o, axis=-1, keepdims=True) + EPS) * gn * _silu(gg)


def _split3(x):
    hi = x.astype(BF16)
    r = x - hi.astype(F32)
    mid = r.astype(BF16)
    lo = (r - mid.astype(F32)).astype(BF16)
    return hi, mid, lo


def _hg_block(qq, kk, lf, v, st, tri):
    c = HG_CHUNK
    nch = qq.shape[0] // c
    nh = qq.shape[1] // HG_DK
    w = nh * HG_DK
    hi, mid, lo = _split3(lf)
    b3 = jnp.dot(tri, jnp.concatenate([hi, mid, lo], axis=1), preferred_element_type=F32)
    b = b3[:, :w] + b3[:, w:2 * w] + b3[:, 2 * w:]
    split = lambda t: jnp.stack([t[ci * c:(ci + 1) * c, k * HG_DK:(k + 1) * HG_DK]
                                 for ci in range(nch) for k in range(nh)])
    q3, k3, bc = split(qq), split(kk), split(b)
    vb = split(v.astype(BF16))
    nb = nch * nh
    ri = lax.broadcasted_iota(jnp.int32, (1, c, c), 1)
    cj = lax.broadcasted_iota(jnp.int32, (1, c, c), 2)
    mid = bc[:, c // 2 - 1:c // 2]
    off_mid = bc - mid

    def att_mid_referenced():
        q_m = (q3 * jnp.exp(off_mid)).astype(BF16)
        k_m = (k3 * jnp.exp(-off_mid)).astype(BF16)
        return jnp.where(ri >= cj, jnp.einsum('cid,cjd->cij', q_m, k_m, preferred_element_type=F32), 0.0)

    def att_any_decay():
        refs = [jnp.zeros_like(bc[:, 0:1])] + [bc[:, i0 - 1:i0] for i0 in range(HG_SUB, c, HG_SUB)]
        span = jnp.concatenate([jnp.broadcast_to(r, (nb, HG_SUB, HG_DK)) for r in refs], axis=1) - bc
        q_t = (q3 * jnp.exp(-span)).astype(BF16)
        row = lax.broadcasted_iota(jnp.int32, (1, c, HG_DK), 1)
        sub_row = lax.broadcasted_iota(jnp.int32, (1, HG_SUB, c), 1)
        sub_col = lax.broadcasted_iota(jnp.int32, (1, HG_SUB, c), 2)
        rows = []
        for i0 in range(0, c, HG_SUB):
            att_i = jnp.zeros((nb, HG_SUB, c), F32)
            if i0 > 0:
                r = refs[i0 // HG_SUB]
                k_t = jnp.where(row < i0, k3 * jnp.exp(jnp.minimum(r - bc, 0.0)), 0.0).astype(BF16)
                att_i = jnp.einsum('cid,cjd->cij', q_t[:, i0:i0 + HG_SUB], k_t, preferred_element_type=F32)
            q_i = q3[:, i0:i0 + HG_SUB]
            b_i = bc[:, i0:i0 + HG_SUB]
            for j in range(HG_SUB):
                jj = i0 + j
                p = q_i * (k3[:, jj:jj + 1] * jnp.exp(jnp.minimum(b_i - bc[:, jj:jj + 1], 0.0)))
                s_j = jnp.sum(p, axis=2, keepdims=True)
                att_i = jnp.where((sub_col == jj) & (sub_row >= j), s_j, att_i)
            rows.append(att_i)
        return jnp.concatenate(rows, axis=1)

    att = lax.cond(jnp.max(jnp.abs(off_mid)) < HG_SAFE_SPAN, att_mid_referenced, att_any_decay).astype(BF16)
    o_intra = jnp.einsum('cij,cjv->civ', att, vb, preferred_element_type=F32)
    b_last = bc[:, c - 1:c]
    qe = (q3 * jnp.exp(bc)).astype(BF16)
    khat = (k3 * jnp.exp(b_last - bc)).astype(BF16)
    dec = jnp.exp(b_last)
    upd = jnp.einsum('cjv,cjd->cvd', vb, khat, preferred_element_type=F32)
    starts = []
    for ci in range(nch):
        heads = slice(ci * nh, (ci + 1) * nh)
        starts.append(st.astype(BF16))
        st = st * dec[heads] + upd[heads]
    o = o_intra + jnp.einsum('cid,cvd->civ', qe, jnp.concatenate(starts, axis=0), preferred_element_type=F32)
    o = jnp.concatenate([jnp.concatenate([o[ci * nh + k] for k in range(nh)], axis=1) for ci in range(nch)], axis=0)
    return o, st


def _hg_prompt_kernel(layer, gq_ref, gf_ref, gi_ref, gg_ref, lb_ref, gn_ref, tri_ref, mix_ref,
                      o_ref, s_ref, st_ref):
    del mix_ref

    @pl.when(pl.program_id(2) == 0)
    def _():
        st_ref[...] = jnp.zeros_like(st_ref)

    qq, kk, f = _hg_gates(gq_ref[...], gf_ref[...], _hg_lower_bound(lb_ref, layer))
    o, st = _hg_block(qq, kk, jnp.log(f), gi_ref[...], st_ref[...], tri_ref[...])
    for k in range(HG_NH):
        cols = slice(k * HG_DK, (k + 1) * HG_DK)
        o_ref[:, cols] = _hg_out(o[:, cols], gn_ref[:, cols], gg_ref[:, cols]).astype(o_ref.dtype)
    st_ref[...] = st

    @pl.when(pl.program_id(2) == pl.num_programs(2) - 1)
    def _():
        for k in range(HG_NH):
            s_ref[0, k] = st[k].T


def _hg_prompt(proj, mix, lb_raw, gn, layer, batch, seq):
    nt = seq // HG_ROWS
    c0 = MIX_HALF * 4 // HG_DK
    nh = HG_NH
    w = nh * HG_DK
    c0 = c0 // nh
    col = lambda off: pl.BlockSpec((HG_ROWS, w), lambda b, h, i: (b * nt + i, c0 + off // nh + h))
    tri = jnp.asarray(np.kron(np.eye(HG_ROWS // HG_CHUNK), np.tril(np.ones((HG_CHUNK, HG_CHUNK)))), BF16)
    return pl.pallas_call(
        functools.partial(_hg_prompt_kernel, layer),
        grid=(batch, HG_HEADS // nh, nt),
        in_specs=[col(0), col(HG_HEADS), col(2 * HG_HEADS), col(3 * HG_HEADS),
                  pl.BlockSpec((lb_raw.shape[0], w), lambda b, h, i: (0, h)),
                  pl.BlockSpec((1, w), lambda b, h, i: (0, h)),
                  pl.BlockSpec((HG_ROWS, HG_ROWS), lambda b, h, i: (0, 0)),
                  pl.BlockSpec(memory_space=pl.ANY)],
        out_specs=[pl.BlockSpec((HG_ROWS, w), lambda b, h, i: (b * nt + i, HG_HEADS // nh + h)),
                   pl.BlockSpec((1, nh, HG_DK, HG_DK), lambda b, h, i: (b, h, 0, 0))],
        out_shape=[jax.ShapeDtypeStruct(mix.shape, mix.dtype),
                   jax.ShapeDtypeStruct((batch, HG_HEADS, HG_DK, HG_DK), F32)],
        scratch_shapes=[pltpu.VMEM((nh, HG_DK, HG_DK), F32)],
        input_output_aliases={7: 0},
        compiler_params=_cparams(3),
        name="hgrn_prompt",
    )(proj, proj, proj, proj, lb_raw, gn.reshape(1, MIX_HALF), tri, mix)


HG_SAMPLE_NH = 4


def _hg_sample_kernel(layer, gq_ref, gf_ref, gi_ref, gg_ref, lb_ref, gn_ref, s_ref, mix_ref,
                      o_ref, so_ref):
    del mix_ref
    qq_all, kk_all, f_all = _hg_gates(gq_ref[...], gf_ref[...], _hg_lower_bound(lb_ref, layer))
    tok = lax.broadcasted_iota(jnp.int32, (SAMPLE_TOK, HG_DK), 0)
    for k in range(HG_SAMPLE_NH):
        cols = slice(k * HG_DK, (k + 1) * HG_DK)
        v = gi_ref[:, cols]
        kt = _columns(kk_all[:, cols], HG_DK)
        ft = _columns(f_all[:, cols], HG_DK)
        qb = qq_all[:, cols].astype(BF16)
        o = jnp.zeros((SAMPLE_TOK, HG_DK), F32)
        for t in range(SAMPLE_TOK):
            s_new = s_ref[t, k] * ft[:, t:t + 1] + kt[:, t:t + 1] * v[t:t + 1, :]
            so_ref[t, k] = s_new
            o = jnp.where(tok == t, jnp.dot(qb, s_new.astype(BF16), preferred_element_type=F32), o)
        o_ref[:, cols] = _hg_out(o, gn_ref[:, cols], gg_ref[:, cols]).astype(o_ref.dtype)


def _hg_sample(proj, mix, state, lb_raw, gn, layer, row0):
    nb = state.shape[0]
    t = SAMPLE_TOK
    nh = HG_SAMPLE_NH
    w = nh * HG_DK
    r0 = row0 // t
    c0 = MIX_HALF * 4 // w
    col = lambda off: pl.BlockSpec((t, w), lambda b, h: (r0 + b, c0 + off // nh + h))
    st = pl.BlockSpec((t, nh, HG_DK, HG_DK), lambda b, h: (b, h, 0, 0))
    return pl.pallas_call(
        functools.partial(_hg_sample_kernel, layer),
        grid=(nb // t, HG_HEADS // nh),
        in_specs=[col(0), col(HG_HEADS), col(2 * HG_HEADS), col(3 * HG_HEADS),
                  pl.BlockSpec((lb_raw.shape[0], w), lambda b, h: (0, h)),
                  pl.BlockSpec((1, w), lambda b, h: (0, h)),
                  st,
                  pl.BlockSpec(memory_space=pl.ANY)],
        out_specs=[pl.BlockSpec((t, w), lambda b, h: (r0 + b, HG_HEADS // nh + h)), st],
        out_shape=[jax.ShapeDtypeStruct(mix.shape, mix.dtype),
                   jax.ShapeDtypeStruct(state.shape, F32)],
        input_output_aliases={7: 0},
        compiler_params=_cparams(2),
        name="hgrn_sample",
    )(proj, proj, proj, proj, lb_raw, gn.reshape(1, MIX_HALF), state, mix)


def _cmul(ar, ai, br, bi):
    return ar * br - ai * bi, ar * bi + ai * br


def _s5_prep_kernel(lr_ref, li_ref, ldt_ref, brt_ref, bit_ref, ar_ref, ai_ref, bbr_ref, bbi_ref):
    lr = lr_ref[...]
    li = li_ref[...]
    dt = jnp.exp(ldt_ref[...])
    mag = jnp.exp(lr * dt)
    ar = mag * jnp.cos(li * dt)
    ai = mag * jnp.sin(li * dt)
    den = lr * lr + li * li
    cr = ((ar - 1.0) * lr + ai * li) / den
    ci = (ai * lr - (ar - 1.0) * li) / den
    brt = brt_ref[...]
    bit = bit_ref[...]
    bbr_ref[...] = cr * brt - ci * bit
    bbi_ref[...] = cr * bit + ci * brt
    ar_ref[...] = ar
    ai_ref[...] = ai


def _s5_prep(lam_re, lam_im, log_dt, b_re, b_im):
    n = S5_GROUPS * S5_STATE
    flat = lambda t: t.reshape(1, n)
    b_t = lambda t: jnp.transpose(t, (2, 0, 1)).reshape(S5_GROUP, n)
    ldt = jnp.repeat(log_dt, S5_STATE).reshape(1, n)
    return pl.pallas_call(
        _s5_prep_kernel,
        out_shape=[jax.ShapeDtypeStruct((1, n), F32)] * 2 + [jax.ShapeDtypeStruct((S5_GROUP, n), F32)] * 2,
        compiler_params=pltpu.CompilerParams(vmem_limit_bytes=VMEM_LIMIT_BYTES),
        name="s5_prep",
    )(flat(lam_re), flat(lam_im), ldt, b_t(b_re), b_t(b_im))


S5_T = 256
S5_HALF_ST = S5_BLK_ST // 2


def _s5_mats_kernel(batch, bbr_ref, bbi_ref, cr_ref, ci_ref, ar_ref, ai_ref, b_out, c_out, a_r_out, a_i_out):
    n = S5_HALF_ST
    hg = S5_BLK_GROUPS // 2
    rows = hg * S5_GROUP
    same_group = (lax.shift_right_logical(lax.broadcasted_iota(jnp.int32, (rows, n), 0), 4)
                  == lax.shift_right_logical(lax.broadcasted_iota(jnp.int32, (rows, n), 1), 6))
    place = lambda t: jnp.where(same_group, jnp.concatenate([t] * hg, axis=0), 0.0)
    for h in range(2):
        cols = slice(h * n, (h + 1) * n)
        r0 = slice(h * rows, (h + 1) * rows)
        b_out[0, r0, 0:n] = place(bbr_ref[:, cols]).astype(BF16)
        b_out[0, r0, n:2 * n] = place(bbi_ref[:, cols]).astype(BF16)
        c_out[0, 0:n, r0] = place(cr_ref[:, cols]).T.astype(BF16)
        c_out[0, n:2 * n, r0] = (-place(ci_ref[:, cols])).T.astype(BF16)
        a_r_out[0, h * batch:(h + 1) * batch, :] = jnp.broadcast_to(ar_ref[:, cols], (batch, n))
        a_i_out[0, h * batch:(h + 1) * batch, :] = jnp.broadcast_to(ai_ref[:, cols], (batch, n))


def _s5_mats(bbr, bbi, c_re, c_im, a_r, a_i, batch):
    assert S5_GROUP == 16 and S5_STATE == 64
    n_all = S5_GROUPS * S5_STATE
    c_t = lambda t: jnp.transpose(t, (1, 0, 2)).reshape(S5_GROUP, n_all)
    blk = lambda r: pl.BlockSpec((r, S5_BLK_ST), lambda j: (0, j))
    out3 = lambda r, c: pl.BlockSpec((1, r, c), lambda j: (j, 0, 0))
    return pl.pallas_call(
        functools.partial(_s5_mats_kernel, batch),
        grid=(S5_NBLK,),
        in_specs=[blk(S5_GROUP)] * 4 + [blk(1)] * 2,
        out_specs=[out3(S5_BLK_CH, S5_BLK_ST), out3(S5_BLK_ST, S5_BLK_CH),
                   out3(2 * batch, S5_HALF_ST), out3(2 * batch, S5_HALF_ST)],
        out_shape=[jax.ShapeDtypeStruct((S5_NBLK, S5_BLK_CH, S5_BLK_ST), BF16),
                   jax.ShapeDtypeStruct((S5_NBLK, S5_BLK_ST, S5_BLK_CH), BF16),
                   jax.ShapeDtypeStruct((S5_NBLK, 2 * batch, S5_HALF_ST), F32),
                   jax.ShapeDtypeStruct((S5_NBLK, 2 * batch, S5_HALF_ST), F32)],
        compiler_params=_cparams(1),
        name="s5_mats",
    )(bbr, bbi, c_t(c_re), c_t(c_im), a_r, a_i)


S5_NB = 2


def _s5_tb_kernel(u_ref, b_ref, c_ref, ar_ref, ai_ref, d_ref, z_in_ref, z_ref, hr_out, hi_out,
                  x_ref, hb_ref, hr_ref, hi_ref):
    del z_in_ref
    i = pl.program_id(1)
    n = S5_HALF_ST
    ch = S5_BLK_CH
    t_steps = u_ref.shape[0] // 4
    half_rows = 4 * t_steps

    @pl.when(i == 0)
    def _():
        hr_ref[...] = jnp.zeros_like(hr_ref)
        hi_ref[...] = jnp.zeros_like(hi_ref)

    first4 = lax.broadcasted_iota(jnp.int32, (1, 8, ch), 1) < 4
    half0 = lax.broadcasted_iota(jnp.int32, (1, 8, ch), 2) < ch // 2
    for k in range(S5_NB):
        u3 = u_ref[:, k * ch:(k + 1) * ch].reshape(t_steps // 2, 8, ch)
        r3 = pltpu.roll(u3, 4, 1)
        even = jnp.where(first4, jnp.where(half0, u3, 0.0), jnp.where(half0, 0.0, r3))
        odd = jnp.where(first4, jnp.where(half0, r3, 0.0), jnp.where(half0, 0.0, u3))
        lhs = jnp.stack([even, odd], axis=1).reshape(8 * t_steps, ch).astype(BF16)
        for r0 in (0, half_rows):
            x_ref[k, r0:r0 + half_rows, :] = jnp.dot(lhs[r0:r0 + half_rows], b_ref[k],
                                                     preferred_element_type=F32)
    a = [(ar_ref[k], ai_ref[k]) for k in range(S5_NB)]

    def advance(t, hs):
        rows = pl.ds(pl.multiple_of(t * 8, 8), 8)
        out = []
        for k in range(S5_NB):
            dr, di = _cmul(*a[k], *hs[k])
            out.append((dr + x_ref[k, rows, 0:n], di + x_ref[k, rows, n:2 * n]))
        return tuple(out)

    def emit(t2, hs):
        h1 = advance(2 * t2, hs)
        h2 = advance(2 * t2 + 1, h1)
        rows = pl.ds(pl.multiple_of(t2 * 16, 16), 16)
        for k in range(S5_NB):
            hb_ref[k, rows, 0:n] = jnp.concatenate([h1[k][0], h2[k][0]], axis=0).astype(BF16)
            hb_ref[k, rows, n:2 * n] = jnp.concatenate([h1[k][1], h2[k][1]], axis=0).astype(BF16)
        return h2

    hs = lax.fori_loop(0, t_steps // 2, emit, tuple((hr_ref[k], hi_ref[k]) for k in range(S5_NB)), unroll=2)
    for k in range(S5_NB):
        hr_ref[k] = hs[k][0]
        hi_ref[k] = hs[k][1]

    pick = lambda yv: jnp.where(half0, yv, pltpu.roll(yv, 4, 1))
    for k in range(S5_NB):
        y2 = jnp.concatenate([jnp.dot(hb_ref[k, r0:r0 + half_rows, :], c_ref[k], preferred_element_type=F32)
                              for r0 in (0, half_rows)], axis=0)
        y4 = y2.reshape(t_steps // 2, 2, 8, ch)
        y = jnp.where(first4, pick(y4[:, 0]), pltpu.roll(pick(y4[:, 1]), 4, 1)).reshape(4 * t_steps, ch)
        cols = slice(k * ch, (k + 1) * ch)
        z_ref[:, cols] = jax.nn.gelu(y + d_ref[:, cols] * u_ref[:, cols]).astype(z_ref.dtype)

    @pl.when(i == pl.num_programs(1) - 1)
    def _():
        for k in range(S5_NB):
            for h in range(2):
                cols = slice((2 * k + h) * n, (2 * k + h + 1) * n)
                hr_out[:, cols] = hs[k][0][4 * h:4 * h + 4, :]
                hi_out[:, cols] = hs[k][1][4 * h:4 * h + 4, :]


def _s5_prompt_tb(u, z, mats, d, batch, seq):
    assert batch == 4
    b_m, c_m, a_r, a_i = mats
    rows = batch * S5_T
    nt = seq // S5_T
    nb = S5_NB
    blk3 = lambda r, c: pl.BlockSpec((nb, r, c), lambda j, i: (j, 0, 0))
    st_spec = pl.BlockSpec((batch, nb * S5_BLK_ST), lambda j, i: (0, j))
    st_shape = jax.ShapeDtypeStruct((batch, S5_GROUPS * S5_STATE), F32)
    return pl.pallas_call(
        _s5_tb_kernel,
        grid=(S5_NBLK // nb, nt),
        in_specs=[pl.BlockSpec((rows, nb * S5_BLK_CH), lambda j, i: (i, j)),
                  blk3(S5_BLK_CH, S5_BLK_ST), blk3(S5_BLK_ST, S5_BLK_CH),
                  blk3(2 * batch, S5_HALF_ST), blk3(2 * batch, S5_HALF_ST),
                  pl.BlockSpec((1, nb * S5_BLK_CH), lambda j, i: (0, j)),
                  pl.BlockSpec(memory_space=pl.ANY)],
        out_specs=[pl.BlockSpec((rows, nb * S5_BLK_CH), lambda j, i: (i, j)), st_spec, st_spec],
        out_shape=[jax.ShapeDtypeStruct(z.shape, z.dtype), st_shape, st_shape],
        scratch_shapes=[pltpu.VMEM((nb, 2 * rows, S5_BLK_ST), F32), pltpu.VMEM((nb, 2 * rows, S5_BLK_ST), BF16),
                        pltpu.VMEM((nb, 2 * batch, S5_HALF_ST), F32), pltpu.VMEM((nb, 2 * batch, S5_HALF_ST), F32)],
        input_output_aliases={6: 0},
        compiler_params=_cparams(2),
        name="s5_prompt",
    )(u, b_m, c_m, a_r, a_i, d.reshape(1, D_MODEL), z)


def _s5_sample_kernel(u_ref, b_ref, c_ref, ar_ref, ai_ref, d_ref, h0r_ref, h0i_ref,
                      z_in_ref, z_ref, hr_out, hi_out):
    del z_in_ref
    n = S5_HALF_ST
    u = u_ref[...]
    nb = u.shape[0]
    half0 = lax.broadcasted_iota(jnp.int32, u.shape, 1) < S5_BLK_CH // 2
    lhs = jnp.concatenate([jnp.where(half0, u, 0.0), jnp.where(half0, 0.0, u)], axis=0).astype(BF16)
    x = jnp.dot(lhs, b_ref[0], preferred_element_type=F32)
    halves = []
    for h in range(2):
        rows = slice(h * nb, (h + 1) * nb)
        cols = slice(h * n, (h + 1) * n)
        row_a = slice(h * (ar_ref.shape[1] // 2), h * (ar_ref.shape[1] // 2) + 1)
        dr, di = _cmul(ar_ref[0, row_a, :], ai_ref[0, row_a, :], h0r_ref[:, cols], h0i_ref[:, cols])
        hr = x[rows, 0:n] + dr
        hi = x[rows, n:2 * n] + di
        hr_out[:, cols] = hr
        hi_out[:, cols] = hi
        halves.append(jnp.concatenate([hr, hi], axis=1))
    y2 = jnp.dot(jnp.concatenate(halves, axis=0).astype(BF16), c_ref[0], preferred_element_type=F32)
    y = jnp.where(half0, y2[0:nb], y2[nb:2 * nb])
    z_ref[...] = jax.nn.gelu(y + d_ref[...] * u).astype(z_ref.dtype)


def _s5_sample(u, z, mats, d, h0r, h0i, row0):
    b_m, c_m, a_r, a_i = mats
    nb = h0r.shape[0]
    r0 = row0 // nb
    blk3 = lambda r, c: pl.BlockSpec((1, r, c), lambda j: (j, 0, 0))
    st_spec = pl.BlockSpec((nb, S5_BLK_ST), lambda j: (0, j))
    st_shape = jax.ShapeDtypeStruct(h0r.shape, F32)
    return pl.pallas_call(
        _s5_sample_kernel,
        grid=(S5_NBLK,),
        in_specs=[pl.BlockSpec((nb, S5_BLK_CH), lambda j: (r0, j)),
                  blk3(S5_BLK_CH, S5_BLK_ST), blk3(S5_BLK_ST, S5_BLK_CH),
                  blk3(a_r.shape[1], S5_HALF_ST), blk3(a_r.shape[1], S5_HALF_ST),
                  pl.BlockSpec((1, S5_BLK_CH), lambda j: (0, j)),
                  st_spec, st_spec,
                  pl.BlockSpec(memory_space=pl.ANY)],
        out_specs=[pl.BlockSpec((nb, S5_BLK_CH), lambda j: (r0, j)), st_spec, st_spec],
        out_shape=[jax.ShapeDtypeStruct(z.shape, z.dtype), st_shape, st_shape],
        input_output_aliases={8: 0},
        compiler_params=_cparams(1),
        name="s5_sample",
    )(u, b_m, c_m, a_r, a_i, d.reshape(1, D_MODEL), h0r, h0i, z)


TB_T = 128
LANES = 128


def _interleave_norm_kernel(n_steps, n_sample, batch, *refs):
    x_refs = refs[:batch]
    xs_ref, g_ref, xt_ref, ut_ref, slab_ref = refs[batch:]
    i = pl.program_id(0)
    d = xt_ref.shape[1]

    @pl.when(i < n_steps)
    def _():
        for dst_ref, norm in ((xt_ref, False), (ut_ref, True)):
            for b, x_ref in enumerate(x_refs):
                v = x_ref[...]
                if norm:
                    v = _rms(v, g_ref[...])
                for s in range(d // LANES):
                    slab_ref[s, pl.ds(b, TB_T, stride=batch), :] = v[:, s * LANES:(s + 1) * LANES]
            for s in range(d // LANES):
                dst_ref[:, s * LANES:(s + 1) * LANES] = slab_ref[s]

    @pl.when(i >= n_steps)
    def _():
        v = xs_ref[...]
        xt_ref[0:n_sample, :] = v
        ut_ref[0:n_sample, :] = _rms(v, g_ref[...])


def _interleave_norm(x, g, batch, seq):
    m, d = x.shape
    n_steps = seq // TB_T
    ms = m - batch * seq
    rows = batch * TB_T
    per_batch = lambda b: pl.BlockSpec((TB_T, d), lambda i: (b * n_steps + jnp.minimum(i, n_steps - 1), 0))
    out = pl.BlockSpec((rows, d), lambda i: (i, 0))
    return pl.pallas_call(
        functools.partial(_interleave_norm_kernel, n_steps, ms, batch),
        grid=(n_steps + 1,),
        in_specs=[per_batch(b) for b in range(batch)]
        + [pl.BlockSpec((ms, d), lambda i: (batch * seq // ms, 0)), pl.BlockSpec((1, d), lambda i: (0, 0))],
        out_specs=[out, out],
        out_shape=[jax.ShapeDtypeStruct((m, d), F32)] * 2,
        scratch_shapes=[pltpu.VMEM((d // LANES, rows, LANES), F32)],
        compiler_params=_cparams(1),
        name="interleave_norm",
    )(*([x] * (batch + 1)), g.reshape(1, d))


def _deinterleave_norm_kernel(n_steps, n_sample, batch, x_ref, g_ref, y_ref, ys_ref, slab_ref):
    i = pl.program_id(0)
    d = x_ref.shape[1]

    @pl.when(i < n_steps)
    def _():
        y = _rms(x_ref[...], g_ref[...])
        for s in range(d // LANES):
            slab_ref[s] = y[:, s * LANES:(s + 1) * LANES]
        for b in range(batch):
            for s in range(d // LANES):
                y_ref[b, :, s * LANES:(s + 1) * LANES] = slab_ref[s, pl.ds(b, TB_T, stride=batch), :]

    @pl.when(i >= n_steps)
    def _():
        ys_ref[...] = _rms(x_ref[0:n_sample, :], g_ref[...])


def _deinterleave_norm(x, g, batch, seq):
    m, d = x.shape
    n_steps = seq // TB_T
    ms = m - batch * seq
    rows = batch * TB_T
    return pl.pallas_call(
        functools.partial(_deinterleave_norm_kernel, n_steps, ms, batch),
        grid=(n_steps + 1,),
        in_specs=[pl.BlockSpec((rows, d), lambda i: (i, 0)), pl.BlockSpec((1, d), lambda i: (0, 0))],
        out_specs=[pl.BlockSpec((batch, TB_T, d), lambda i: (0, jnp.minimum(i, n_steps - 1), 0)),
                   pl.BlockSpec((ms, d), lambda i: (0, 0))],
        out_shape=[jax.ShapeDtypeStruct((batch, seq, d), F32), jax.ShapeDtypeStruct((ms, d), F32)],
        scratch_shapes=[pltpu.VMEM((d // LANES, rows, LANES), F32)],
        compiler_params=_cparams(1),
        name="deinterleave_norm",
    )(x, g.reshape(1, d))


def _rope_tables(pos):
    inv = ROPE_BASE ** (-jnp.arange(RET_HALF, dtype=F32) / RET_HALF)
    ang = pos[:, None] * inv[None, :]
    return jnp.cos(ang), jnp.sin(ang)


def _ffn(x, norm_g, wg, wu, wd, layer):
    h = _rmsnorm(x, norm_g[layer], [BF16])[0]
    t = _matmul(h, [wg, wu], _epi_swiglu, BF16, "ffn_gate_up", layer=layer)
    return _matmul(t, [wd], _epi_id, F32, "ffn_down", res=x, layer=layer)


def kernel(x_prompt, x_sample, state_ret, state_hgrn, state_s5_re, state_s5_im, attn_norm_g, w_in, ret_gn_g, hg_lb, hg_gn_g, w_out, ssm_norm_g, s5_lam_re, s5_lam_im, s5_log_dt, s5_b_re, s5_b_im, s5_c_re, s5_c_im, s5_d, w_glu_a, w_glu_b, ffn_norm_g, w_ffn_gate, w_ffn_up, w_ffn_down, final_norm_g):
    one = lambda t: t.reshape(t.shape[1:])
    (state_ret, state_hgrn, state_s5_re, state_s5_im, attn_norm_g, w_in, ret_gn_g, hg_gn_g, w_out, ssm_norm_g,
     s5_lam_re, s5_lam_im, s5_log_dt, s5_b_re, s5_b_im, s5_c_re, s5_c_im, s5_d, w_glu_a, w_glu_b) = map(one, (
         state_ret, state_hgrn, state_s5_re, state_s5_im, attn_norm_g, w_in, ret_gn_g, hg_gn_g, w_out, ssm_norm_g,
         s5_lam_re, s5_lam_im, s5_log_dt, s5_b_re, s5_b_im, s5_c_re, s5_c_im, s5_d, w_glu_a, w_glu_b))
    cos_p, sin_p = _rope_tables(jnp.arange(SEQ, dtype=F32))
    cos_s, sin_s = _rope_tables(jnp.full((1,), float(PAST_LEN), F32))

    h, x = _rmsnorm_stack(x_prompt.reshape(N_PROMPT, D_MODEL), x_sample.reshape(DEC_BATCH, D_MODEL), attn_norm_g)
    proj = _matmul(h, [w_in], _epi_id, F32, "w_in")
    mix, ret_p = _ret_prompt(proj, h, cos_p, sin_p, ret_gn_g, BATCH, SEQ)
    mix, hg_p = _hg_prompt(proj, mix, hg_lb, hg_gn_g, 0, BATCH, SEQ)
    mix, ret_s = _ret_sample(proj, mix, state_ret, cos_s, sin_s, ret_gn_g, N_PROMPT)
    mix, hg_s = _hg_sample(proj, mix, state_hgrn, hg_lb, hg_gn_g, 0, N_PROMPT)
    x = _matmul(mix, [w_out], _epi_id, F32, "w_out", res=x)
    z = mix
    x = _ffn(x, ffn_norm_g, w_ffn_gate, w_ffn_up, w_ffn_down, 0)

    x, u = _interleave_norm(x, ssm_norm_g, BATCH, SEQ)
    p_r, p_i, bbr, bbi = _s5_prep(s5_lam_re, s5_lam_im, s5_log_dt, s5_b_re, s5_b_im)
    mats = _s5_mats(bbr, bbi, s5_c_re, s5_c_im, p_r, p_i, BATCH)
    z, s5r_p, s5i_p = _s5_prompt_tb(u, z, mats, s5_d, BATCH, SEQ)
    n_st = S5_GROUPS * S5_STATE
    z, s5r_s, s5i_s = _s5_sample(u, z, mats, s5_d, state_s5_re.reshape(DEC_BATCH, n_st),
                                 state_s5_im.reshape(DEC_BATCH, n_st), N_PROMPT)
    x = _matmul(z, [w_glu_a, w_glu_b], _epi_glu, F32, "glu", res=x)
    x = _ffn(x, ffn_norm_g, w_ffn_gate, w_ffn_up, w_ffn_down, 1)

    y_p, y_s = _deinterleave_norm(x, final_norm_g, BATCH, SEQ)
    st = lambda t, b: t.reshape(1, b, S5_GROUPS, S5_STATE)
    return (y_p, y_s.reshape(DEC_BATCH, 1, D_MODEL),
            ret_p[None], ret_s[None], hg_p[None], hg_s[None],
            st(s5r_p, BATCH), st(s5i_p, BATCH), st(s5r_s, DEC_BATCH), st(s5i_s, DEC_BATCH))
```

```python
import functools

import numpy as np
import jax
import jax.numpy as jnp
from jax import lax
from jax.experimental import pallas as pl
from jax.experimental.pallas import tpu as pltpu

F32 = jnp.float32
BF16 = jnp.bfloat16

D_MODEL = 2048
BATCH = 4
SEQ = 2048
DEC_BATCH = 128
PAST_LEN = 16384
N_PROMPT = BATCH * SEQ
MIX_HALF = D_MODEL // 2
RET_HEADS = 4
RET_DK = MIX_HALF // RET_HEADS
RET_HALF = RET_DK // 2
RET_CHUNK = 128
RET_ROWS = 512
HG_HEADS = 8
HG_DK = MIX_HALF // HG_HEADS
HG_CHUNK = 64
HG_SUB = 16
HG_ROWS = 256
HG_NH = 8
HG_SAFE_SPAN = 40.0
S5_GROUP = 16
S5_GROUPS = D_MODEL // S5_GROUP
S5_STATE = 64
S5_BLK_GROUPS = 16
S5_BLK_CH = S5_BLK_GROUPS * S5_GROUP
S5_BLK_ST = S5_BLK_GROUPS * S5_STATE
S5_NBLK = S5_GROUPS // S5_BLK_GROUPS
ROPE_BASE = 10000.0
EPS = 1e-6
SAMPLE_TOK = 32

VMEM_LIMIT_BYTES = 56 * 1024 * 1024

MM_TILES = {"w_in": (1664, 1024), "w_out": (832, 1024), "ffn_gate_up": (1664, 512),
            "ffn_down": (832, 512), "glu": (1664, 512)}
MM_SUB_TILES = 2


def _cparams(n_axes):
    return pltpu.CompilerParams(dimension_semantics=("arbitrary",) * n_axes,
                                vmem_limit_bytes=VMEM_LIMIT_BYTES)


def _sigmoid(x):
    return 0.5 * jnp.tanh(0.5 * x) + 0.5


def _silu(x):
    return x * _sigmoid(x)


def _rmsnorm_kernel(x_ref, g_ref, *o_refs):
    x = x_ref[...]
    y = x * lax.rsqrt(jnp.mean(x * x, axis=-1, keepdims=True) + EPS) * g_ref[...]
    for o_ref in o_refs:
        o_ref[...] = y.astype(o_ref.dtype)


def _rmsnorm(x, g, dtypes, tm=1664):
    m, d = x.shape
    return pl.pallas_call(
        _rmsnorm_kernel,
        grid=(m // tm,),
        in_specs=[pl.BlockSpec((tm, d), lambda i: (i, 0)),
                  pl.BlockSpec((1, d), lambda i: (0, 0))],
        out_specs=[pl.BlockSpec((tm, d), lambda i: (i, 0)) for _ in dtypes],
        out_shape=[jax.ShapeDtypeStruct((m, d), dt) for dt in dtypes],
        compiler_params=_cparams(1),
        name="rmsnorm",
    )(x, g.reshape(1, d))


ROW_TILE = 1024


def _rms(x, g):
    return x * lax.rsqrt(jnp.mean(x * x, axis=-1, keepdims=True) + EPS) * g


def _rmsnorm_stack_kernel(n_prompt_tiles, n_sample, xp_ref, xs_ref, g_ref, h_ref, x_ref):
    i = pl.program_id(0)

    @pl.when(i < n_prompt_tiles)
    def _():
        x = xp_ref[...]
        x_ref[...] = x
        h_ref[...] = _rms(x, g_ref[...]).astype(h_ref.dtype)

    @pl.when(i >= n_prompt_tiles)
    def _():
        x = xs_ref[...]
        x_ref[0:n_sample, :] = x
        h_ref[0:n_sample, :] = _rms(x, g_ref[...]).astype(h_ref.dtype)


def _rmsnorm_stack(x_prompt, x_sample, g):
    (mp, d), ms = x_prompt.shape, x_sample.shape[0]
    tm = ROW_TILE
    npt = mp // tm
    m = mp + ms
    return pl.pallas_call(
        functools.partial(_rmsnorm_stack_kernel, npt, ms),
        grid=(npt + 1,),
        in_specs=[pl.BlockSpec((tm, d), lambda i: (jnp.minimum(i, npt - 1), 0)),
                  pl.BlockSpec((ms, d), lambda i: (0, 0)),
                  pl.BlockSpec((1, d), lambda i: (0, 0))],
        out_specs=[pl.BlockSpec((tm, d), lambda i: (i, 0))] * 2,
        out_shape=[jax.ShapeDtypeStruct((m, d), BF16), jax.ShapeDtypeStruct((m, d), F32)],
        compiler_params=_cparams(1),
        name="rmsnorm_stack",
    )(x_prompt, x_sample, g.reshape(1, d))


def _mm_kernel(n_w, epilogue, has_res, a_ref, *refs):
    w_refs = refs[:n_w]
    res_ref = refs[n_w] if has_res else None
    o_ref = refs[n_w + has_res]
    wb_refs = refs[n_w + has_res + 1:]

    @pl.when(pl.program_id(1) == 0)
    def _():
        for w_ref, wb_ref in zip(w_refs, wb_refs):
            wb_ref[...] = w_ref[...].astype(BF16)

    sub = a_ref.shape[0] // MM_SUB_TILES
    for r in range(MM_SUB_TILES):
        rows = pl.ds(r * sub, sub)
        a = a_ref[rows, :]
        y = epilogue(*[jnp.dot(a, wb_ref[...], preferred_element_type=F32) for wb_ref in wb_refs])
        if has_res:
            y = res_ref[rows, :] + y
        o_ref[rows, :] = y.astype(o_ref.dtype)


def _matmul(a, ws, epilogue, out_dtype, name, res=None, layer=None):
    tm, tn = MM_TILES[name]
    m, k = a.shape
    n = ws[0].shape[-1]
    in_specs = [pl.BlockSpec((tm, k), lambda j, i: (i, 0))]
    if layer is None:
        in_specs += [pl.BlockSpec((k, tn), lambda j, i: (0, j)) for _ in ws]
    else:
        in_specs += [pl.BlockSpec((None, k, tn), lambda j, i: (layer, 0, j)) for _ in ws]
    args = [a, *ws]
    if res is not None:
        in_specs.append(pl.BlockSpec((tm, tn), lambda j, i: (i, j)))
        args.append(res)
    return pl.pallas_call(
        functools.partial(_mm_kernel, len(ws), epilogue, res is not None),
        grid=(n // tn, m // tm),
        in_specs=in_specs,
        out_specs=pl.BlockSpec((tm, tn), lambda j, i: (i, j)),
        out_shape=jax.ShapeDtypeStruct((m, n), out_dtype),
        scratch_shapes=[pltpu.VMEM((k, tn), BF16) for _ in ws],
        compiler_params=_cparams(2),
        name=name,
    )(*args)


def _epi_id(y):
    return y


def _epi_swiglu(g, u):
    return _silu(g) * u


def _epi_glu(a, b):
    return a * _sigmoid(b)


def _rotary(x, cos, sin):
    x1 = x[:, :RET_HALF]
    x2 = x[:, RET_HALF:]
    return jnp.concatenate([x1 * cos - x2 * sin, x1 * sin + x2 * cos], axis=1)


def _group_norm_gate(o, gn, g):
    mu = jnp.mean(o, axis=-1, keepdims=True)
    d = o - mu
    var = jnp.mean(d * d, axis=-1, keepdims=True)
    return d * lax.rsqrt(var + EPS) * gn * _silu(g)


def _ret_prompt_kernel(q_ref, k_ref, v_ref, g_ref, cos_ref, sin_ref, intra_ref, qdec_ref, kdec_ref,
                       cdec_ref, gn_ref, mix_ref, o_ref, s_ref):
    del mix_ref

    @pl.when(pl.program_id(1) == 0)
    def _():
        s_ref[...] = jnp.zeros_like(s_ref)

    c = RET_CHUNK
    nch = RET_ROWS // c
    nh = RET_HEADS

    def split(ref, rotate):
        out = []
        for ci in range(nch):
            rows = pl.ds(ci * c, c)
            for k in range(nh):
                x = ref[rows, k * RET_DK:(k + 1) * RET_DK]
                out.append(_rotary(x, cos_ref[rows, :], sin_ref[rows, :]) if rotate else x)
        return jnp.stack(out)

    per_chunk = lambda t: jnp.concatenate([t] * nch, axis=0)
    q = split(q_ref, True)
    k = split(k_ref, True) * (RET_DK ** -0.5)
    v = split(v_ref, False).astype(BF16)
    att = jnp.einsum('bid,bjd->bij', q.astype(BF16), k.astype(BF16),
                     preferred_element_type=F32) * per_chunk(intra_ref[...])
    o = jnp.einsum('bij,bjv->biv', att.astype(BF16), v, preferred_element_type=F32)
    kv = jnp.einsum('bjd,bjv->bdv', (k * per_chunk(kdec_ref[...])).astype(BF16), v, preferred_element_type=F32)
    s = s_ref[0]
    cdec = cdec_ref[:, 0:1, :]
    starts = []
    for ci in range(nch):
        starts.append(s.astype(BF16))
        s = s * cdec + kv[ci * nh:(ci + 1) * nh]
    s_ref[0] = s
    o = o + jnp.einsum('bid,bdv->biv', (q * per_chunk(qdec_ref[...])).astype(BF16),
                       jnp.concatenate(starts, axis=0), preferred_element_type=F32)
    for ci in range(nch):
        rows = pl.ds(ci * c, c)
        for k in range(nh):
            cols = slice(k * RET_DK, (k + 1) * RET_DK)
            o_ref[rows, cols] = _group_norm_gate(o[ci * nh + k], gn_ref[:, cols], g_ref[rows, cols]).astype(o_ref.dtype)


def _ret_decay_tables(c):
    lg = np.log(1.0 - 2.0 ** (-5.0 - np.arange(RET_HEADS, dtype=np.float64)))
    idx = np.arange(c, dtype=np.float64)
    diff = idx[:, None] - idx[None, :]
    intra = np.where(diff >= 0, np.exp(np.maximum(diff, 0.0)[None] * lg[:, None, None]), 0.0)
    ones = np.ones((1, 1, RET_DK))
    qdec = np.exp((idx[None, :, None] + 1.0) * lg[:, None, None]) * ones
    kdec = np.exp((c - 1.0 - idx[None, :, None]) * lg[:, None, None]) * ones
    cdec = np.exp(c * lg)[:, None, None] * np.ones((1, 8, RET_DK))
    return [jnp.asarray(t, F32) for t in (intra, qdec, kdec, cdec)]


def _ret_prompt(proj, mix, cos, sin, gn, batch, seq):
    c = RET_CHUNK
    r = RET_ROWS
    nc = seq // r
    w = RET_HEADS * RET_DK
    tables = _ret_decay_tables(c)
    col = lambda j: pl.BlockSpec((r, w), lambda b, i: (b * nc + i, j))
    whole = lambda t: pl.BlockSpec(t.shape, lambda b, i: (0,) * t.ndim)
    return pl.pallas_call(
        _ret_prompt_kernel,
        grid=(batch, nc),
        in_specs=[col(0), col(1), col(2), col(3),
                  pl.BlockSpec((r, RET_HALF), lambda b, i: (i, 0)),
                  pl.BlockSpec((r, RET_HALF), lambda b, i: (i, 0)),
                  *[whole(t) for t in tables],
                  pl.BlockSpec((1, w), lambda b, i: (0, 0)),
                  pl.BlockSpec(memory_space=pl.ANY)],
        out_specs=[pl.BlockSpec((r, w), lambda b, i: (b * nc + i, 0)),
                   pl.BlockSpec((1, RET_HEADS, RET_DK, RET_DK), lambda b, i: (b, 0, 0, 0))],
        out_shape=[jax.ShapeDtypeStruct(mix.shape, mix.dtype),
                   jax.ShapeDtypeStruct((batch, RET_HEADS, RET_DK, RET_DK), F32)],
        input_output_aliases={11: 0},
        compiler_params=_cparams(2),
        name="ret_prompt",
    )(proj, proj, proj, proj, cos, sin, *tables, gn.reshape(1, MIX_HALF), mix)


def _columns(x, n):
    t = x.shape[0]
    parts = [jnp.concatenate([x[:, i:i + 128]] * (128 // t), axis=0).T for i in range(0, n, 128)]
    return parts[0] if len(parts) == 1 else jnp.concatenate(parts, axis=0)


def _ret_sample_kernel(q_ref, k_ref, v_ref, g_ref, cos_ref, sin_ref, cdec_ref, gn_ref, s_ref, mix_ref,
                       o_ref, so_ref):
    del mix_ref
    cos = cos_ref[...]
    sin = sin_ref[...]
    q = _rotary(q_ref[...], cos, sin)
    k = _rotary(k_ref[...], cos, sin) * (RET_DK ** -0.5)
    v = v_ref[...]
    kt = _columns(k, RET_DK)
    gamma = cdec_ref[0, 0:1, :]
    qb = q.astype(BF16)
    tok = lax.broadcasted_iota(jnp.int32, q.shape, 0)
    o = jnp.zeros(q.shape, F32)
    for t in range(SAMPLE_TOK):
        s_new = s_ref[t, 0] * gamma + kt[:, t:t + 1] * v[t:t + 1, :]
        so_ref[t, 0] = s_new
        o = jnp.where(tok == t, jnp.dot(qb, s_new.astype(BF16), preferred_element_type=F32), o)
    o_ref[...] = _group_norm_gate(o, gn_ref[...], g_ref[...]).astype(o_ref.dtype)


def _ret_sample(proj, mix, state, cos, sin, gn, row0):
    nb = state.shape[0]
    t = SAMPLE_TOK
    r0 = row0 // t
    _, _, _, cdec = _ret_decay_tables(1)
    col = lambda off: pl.BlockSpec((t, RET_DK), lambda b, h: (r0 + b, off + h))
    st = pl.BlockSpec((t, 1, RET_DK, RET_DK), lambda b, h: (b, h, 0, 0))
    return pl.pallas_call(
        _ret_sample_kernel,
        grid=(nb // t, RET_HEADS),
        in_specs=[col(0), col(RET_HEADS), col(2 * RET_HEADS), col(3 * RET_HEADS),
                  pl.BlockSpec((1, RET_HALF), lambda b, h: (0, 0)),
                  pl.BlockSpec((1, RET_HALF), lambda b, h: (0, 0)),
                  pl.BlockSpec((1, 8, RET_DK), lambda b, h: (h, 0, 0)),
                  pl.BlockSpec((1, RET_DK), lambda b, h: (0, h)),
                  st,
                  pl.BlockSpec(memory_space=pl.ANY)],
        out_specs=[pl.BlockSpec((t, RET_DK), lambda b, h: (r0 + b, h)), st],
        out_shape=[jax.ShapeDtypeStruct(mix.shape, mix.dtype),
                   jax.ShapeDtypeStruct(state.shape, F32)],
        input_output_aliases={9: 0},
        compiler_params=_cparams(2),
        name="ret_sample",
    )(proj, proj, proj, proj, cos, sin, cdec, gn.reshape(1, MIX_HALF), state, mix)


def _hg_lower_bound(lb_ref, layer):
    x = lb_ref[...]
    e = jnp.exp(x - jnp.max(x, axis=0, keepdims=True))
    return jnp.sum(e[:layer + 1], axis=0, keepdims=True) / jnp.sum(e, axis=0, keepdims=True)


def _hg_gates(gq, gf, lb):
    f = lb + (1.0 - lb) * jax.nn.sigmoid(gf)
    return _silu(gq), 1.0 - f, f


def _hg_out(o, gn, gg):
    return o * lax.rsqrt(jnp.mean(o * o, axis=-1, keepdims=True) + EPS) * gn * _silu(gg)


def _split3(x):
    hi = x.astype(BF16)
    r = x - hi.astype(F32)
    mid = r.astype(BF16)
    lo = (r - mid.astype(F32)).astype(BF16)
    return hi, mid, lo


def _hg_block(qq, kk, lf, v, st, tri):
    c = HG_CHUNK
    nch = qq.shape[0] // c
    nh = qq.shape[1] // HG_DK
    w = nh * HG_DK
    hi, mid, lo = _split3(lf)
    b3 = jnp.dot(tri, jnp.concatenate([hi, mid, lo], axis=1), preferred_element_type=F32)
    b = b3[:, :w] + b3[:, w:2 * w] + b3[:, 2 * w:]
    split = lambda t: jnp.stack([t[ci * c:(ci + 1) * c, k * HG_DK:(k + 1) * HG_DK]
                                 for ci in range(nch) for k in range(nh)])
    q3, k3, bc = split(qq), split(kk), split(b)
    vb = split(v.astype(BF16))
    nb = nch * nh
    ri = lax.broadcasted_iota(jnp.int32, (1, c, c), 1)
    cj = lax.broadcasted_iota(jnp.int32, (1, c, c), 2)
    mid = bc[:, c // 2 - 1:c // 2]
    off_mid = bc - mid

    def att_mid_referenced():
        q_m = (q3 * jnp.exp(off_mid)).astype(BF16)
        k_m = (k3 * jnp.exp(-off_mid)).astype(BF16)
        return jnp.where(ri >= cj, jnp.einsum('cid,cjd->cij', q_m, k_m, preferred_element_type=F32), 0.0)

    def att_any_decay():
        refs = [jnp.zeros_like(bc[:, 0:1])] + [bc[:, i0 - 1:i0] for i0 in range(HG_SUB, c, HG_SUB)]
        span = jnp.concatenate([jnp.broadcast_to(r, (nb, HG_SUB, HG_DK)) for r in refs], axis=1) - bc
        q_t = (q3 * jnp.exp(-span)).astype(BF16)
        row = lax.broadcasted_iota(jnp.int32, (1, c, HG_DK), 1)
        sub_row = lax.broadcasted_iota(jnp.int32, (1, HG_SUB, c), 1)
        sub_col = lax.broadcasted_iota(jnp.int32, (1, HG_SUB, c), 2)
        rows = []
        for i0 in range(0, c, HG_SUB):
            att_i = jnp.zeros((nb, HG_SUB, c), F32)
            if i0 > 0:
                r = refs[i0 // HG_SUB]
                k_t = jnp.where(row < i0, k3 * jnp.exp(jnp.minimum(r - bc, 0.0)), 0.0).astype(BF16)
                att_i = jnp.einsum('cid,cjd->cij', q_t[:, i0:i0 + HG_SUB], k_t, preferred_element_type=F32)
            q_i = q3[:, i0:i0 + HG_SUB]
            b_i = bc[:, i0:i0 + HG_SUB]
            for j in range(HG_SUB):
                jj = i0 + j
                p = q_i * (k3[:, jj:jj + 1] * jnp.exp(jnp.minimum(b_i - bc[:, jj:jj + 1], 0.0)))
                s_j = jnp.sum(p, axis=2, keepdims=True)
                att_i = jnp.where((sub_col == jj) & (sub_row >= j), s_j, att_i)
            rows.append(att_i)
        return jnp.concatenate(rows, axis=1)

    att = lax.cond(jnp.max(jnp.abs(off_mid)) < HG_SAFE_SPAN, att_mid_referenced, att_any_decay).astype(BF16)
    o_intra = jnp.einsum('cij,cjv->civ', att, vb, preferred_element_type=F32)
    b_last = bc[:, c - 1:c]
    qe = (q3 * jnp.exp(bc)).astype(BF16)
    khat = (k3 * jnp.exp(b_last - bc)).astype(BF16)
    dec = jnp.exp(b_last)
    upd = jnp.einsum('cjv,cjd->cvd', vb, khat, preferred_element_type=F32)
    starts = []
    for ci in range(nch):
        heads = slice(ci * nh, (ci + 1) * nh)
        starts.append(st.astype(BF16))
        st = st * dec[heads] + upd[heads]
    o = o_intra + jnp.einsum('cid,cvd->civ', qe, jnp.concatenate(starts, axis=0), preferred_element_type=F32)
    o = jnp.concatenate([jnp.concatenate([o[ci * nh + k] for k in range(nh)], axis=1) for ci in range(nch)], axis=0)
    return o, st


def _hg_prompt_kernel(layer, gq_ref, gf_ref, gi_ref, gg_ref, lb_ref, gn_ref, tri_ref, mix_ref,
                      o_ref, s_ref, st_ref):
    del mix_ref

    @pl.when(pl.program_id(2) == 0)
    def _():
        st_ref[...] = jnp.zeros_like(st_ref)

    qq, kk, f = _hg_gates(gq_ref[...], gf_ref[...], _hg_lower_bound(lb_ref, layer))
    o, st = _hg_block(qq, kk, jnp.log(f), gi_ref[...], st_ref[...], tri_ref[...])
    for k in range(HG_NH):
        cols = slice(k * HG_DK, (k + 1) * HG_DK)
        o_ref[:, cols] = _hg_out(o[:, cols], gn_ref[:, cols], gg_ref[:, cols]).astype(o_ref.dtype)
    st_ref[...] = st

    @pl.when(pl.program_id(2) == pl.num_programs(2) - 1)
    def _():
        for k in range(HG_NH):
            s_ref[0, k] = st[k].T


def _hg_prompt(proj, mix, lb_raw, gn, layer, batch, seq):
    nt = seq // HG_ROWS
    c0 = MIX_HALF * 4 // HG_DK
    nh = HG_NH
    w = nh * HG_DK
    c0 = c0 // nh
    col = lambda off: pl.BlockSpec((HG_ROWS, w), lambda b, h, i: (b * nt + i, c0 + off // nh + h))
    tri = jnp.asarray(np.kron(np.eye(HG_ROWS // HG_CHUNK), np.tril(np.ones((HG_CHUNK, HG_CHUNK)))), BF16)
    return pl.pallas_call(
        functools.partial(_hg_prompt_kernel, layer),
        grid=(batch, HG_HEADS // nh, nt),
        in_specs=[col(0), col(HG_HEADS), col(2 * HG_HEADS), col(3 * HG_HEADS),
                  pl.BlockSpec((lb_raw.shape[0], w), lambda b, h, i: (0, h)),
                  pl.BlockSpec((1, w), lambda b, h, i: (0, h)),
                  pl.BlockSpec((HG_ROWS, HG_ROWS), lambda b, h, i: (0, 0)),
                  pl.BlockSpec(memory_space=pl.ANY)],
        out_specs=[pl.BlockSpec((HG_ROWS, w), lambda b, h, i: (b * nt + i, HG_HEADS // nh + h)),
                   pl.BlockSpec((1, nh, HG_DK, HG_DK), lambda b, h, i: (b, h, 0, 0))],
        out_shape=[jax.ShapeDtypeStruct(mix.shape, mix.dtype),
                   jax.ShapeDtypeStruct((batch, HG_HEADS, HG_DK, HG_DK), F32)],
        scratch_shapes=[pltpu.VMEM((nh, HG_DK, HG_DK), F32)],
        input_output_aliases={7: 0},
        compiler_params=_cparams(3),
        name="hgrn_prompt",
    )(proj, proj, proj, proj, lb_raw, gn.reshape(1, MIX_HALF), tri, mix)


HG_SAMPLE_NH = 4


def _hg_sample_kernel(layer, gq_ref, gf_ref, gi_ref, gg_ref, lb_ref, gn_ref, s_ref, mix_ref,
                      o_ref, so_ref):
    del mix_ref
    qq_all, kk_all, f_all = _hg_gates(gq_ref[...], gf_ref[...], _hg_lower_bound(lb_ref, layer))
    tok = lax.broadcasted_iota(jnp.int32, (SAMPLE_TOK, HG_DK), 0)
    for k in range(HG_SAMPLE_NH):
        cols = slice(k * HG_DK, (k + 1) * HG_DK)
        v = gi_ref[:, cols]
        kt = _columns(kk_all[:, cols], HG_DK)
        ft = _columns(f_all[:, cols], HG_DK)
        qb = qq_all[:, cols].astype(BF16)
        o = jnp.zeros((SAMPLE_TOK, HG_DK), F32)
        for t in range(SAMPLE_TOK):
            s_new = s_ref[t, k] * ft[:, t:t + 1] + kt[:, t:t + 1] * v[t:t + 1, :]
            so_ref[t, k] = s_new
            o = jnp.where(tok == t, jnp.dot(qb, s_new.astype(BF16), preferred_element_type=F32), o)
        o_ref[:, cols] = _hg_out(o, gn_ref[:, cols], gg_ref[:, cols]).astype(o_ref.dtype)


def _hg_sample(proj, mix, state, lb_raw, gn, layer, row0):
    nb = state.shape[0]
    t = SAMPLE_TOK
    nh = HG_SAMPLE_NH
    w = nh * HG_DK
    r0 = row0 // t
    c0 = MIX_HALF * 4 // w
    col = lambda off: pl.BlockSpec((t, w), lambda b, h: (r0 + b, c0 + off // nh + h))
    st = pl.BlockSpec((t, nh, HG_DK, HG_DK), lambda b, h: (b, h, 0, 0))
    return pl.pallas_call(
        functools.partial(_hg_sample_kernel, layer),
        grid=(nb // t, HG_HEADS // nh),
        in_specs=[col(0), col(HG_HEADS), col(2 * HG_HEADS), col(3 * HG_HEADS),
                  pl.BlockSpec((lb_raw.shape[0], w), lambda b, h: (0, h)),
                  pl.BlockSpec((1, w), lambda b, h: (0, h)),
                  st,
                  pl.BlockSpec(memory_space=pl.ANY)],
        out_specs=[pl.BlockSpec((t, w), lambda b, h: (r0 + b, HG_HEADS // nh + h)), st],
        out_shape=[jax.ShapeDtypeStruct(mix.shape, mix.dtype),
                   jax.ShapeDtypeStruct(state.shape, F32)],
        input_output_aliases={7: 0},
        compiler_params=_cparams(2),
        name="hgrn_sample",
    )(proj, proj, proj, proj, lb_raw, gn.reshape(1, MIX_HALF), state, mix)


def _cmul(ar, ai, br, bi):
    return ar * br - ai * bi, ar * bi + ai * br


def _s5_zoh(lr, li, ldt, brt, bit):
    dt = jnp.exp(ldt)
    mag = jnp.exp(lr * dt)
    ar = mag * jnp.cos(li * dt)
    ai = mag * jnp.sin(li * dt)
    den = lr * lr + li * li
    cr = ((ar - 1.0) * lr + ai * li) / den
    ci = (ai * lr - (ar - 1.0) * li) / den
    return ar, ai, cr * brt - ci * bit, cr * bit + ci * brt


S5_T = 256
S5_HALF_ST = S5_BLK_ST // 2


def _s5_mats_kernel(batch, lr_ref, li_ref, ldt_ref, brt_ref, bit_ref, cr_ref, ci_ref,
                    b_out, c_out, a_r_out, a_i_out):
    n = S5_HALF_ST
    hg = S5_BLK_GROUPS // 2
    rows = hg * S5_GROUP
    same_group = (lax.shift_right_logical(lax.broadcasted_iota(jnp.int32, (rows, n), 0), 4)
                  == lax.shift_right_logical(lax.broadcasted_iota(jnp.int32, (rows, n), 1), 6))
    place = lambda t: jnp.where(same_group, jnp.concatenate([t] * hg, axis=0), 0.0)
    a_r, a_i, bbr, bbi = _s5_zoh(lr_ref[...], li_ref[...], ldt_ref[...], brt_ref[...], bit_ref[...])
    for h in range(2):
        cols = slice(h * n, (h + 1) * n)
        r0 = slice(h * rows, (h + 1) * rows)
        b_out[0, r0, 0:n] = place(bbr[:, cols]).astype(BF16)
        b_out[0, r0, n:2 * n] = place(bbi[:, cols]).astype(BF16)
        c_out[0, 0:n, r0] = place(cr_ref[:, cols]).T.astype(BF16)
        c_out[0, n:2 * n, r0] = (-place(ci_ref[:, cols])).T.astype(BF16)
        a_r_out[0, h * batch:(h + 1) * batch, :] = jnp.broadcast_to(a_r[:, cols], (batch, n))
        a_i_out[0, h * batch:(h + 1) * batch, :] = jnp.broadcast_to(a_i[:, cols], (batch, n))


def _s5_mats(lam_re, lam_im, log_dt, b_re, b_im, c_re, c_im, batch):
    assert S5_GROUP == 16 and S5_STATE == 64
    n_all = S5_GROUPS * S5_STATE
    flat = lambda t: t.reshape(1, n_all)
    b_t = lambda t: jnp.transpose(t, (2, 0, 1)).reshape(S5_GROUP, n_all)
    c_t = lambda t: jnp.transpose(t, (1, 0, 2)).reshape(S5_GROUP, n_all)
    ldt = jnp.repeat(log_dt, S5_STATE).reshape(1, n_all)
    blk = lambda r: pl.BlockSpec((r, S5_BLK_ST), lambda j: (0, j))
    out3 = lambda r, c: pl.BlockSpec((1, r, c), lambda j: (j, 0, 0))
    return pl.pallas_call(
        functools.partial(_s5_mats_kernel, batch),
        grid=(S5_NBLK,),
        in_specs=[blk(1)] * 3 + [blk(S5_GROUP)] * 4,
        out_specs=[out3(S5_BLK_CH, S5_BLK_ST), out3(S5_BLK_ST, S5_BLK_CH),
                   out3(2 * batch, S5_HALF_ST), out3(2 * batch, S5_HALF_ST)],
        out_shape=[jax.ShapeDtypeStruct((S5_NBLK, S5_BLK_CH, S5_BLK_ST), BF16),
                   jax.ShapeDtypeStruct((S5_NBLK, S5_BLK_ST, S5_BLK_CH), BF16),
                   jax.ShapeDtypeStruct((S5_NBLK, 2 * batch, S5_HALF_ST), F32),
                   jax.ShapeDtypeStruct((S5_NBLK, 2 * batch, S5_HALF_ST), F32)],
        compiler_params=_cparams(1),
        name="s5_mats",
    )(flat(lam_re), flat(lam_im), ldt, b_t(b_re), b_t(b_im), c_t(c_re), c_t(c_im))


S5_NB = 2


def _s5_tb_kernel(u_ref, b_ref, c_ref, ar_ref, ai_ref, d_ref, z_in_ref, z_ref, hr_out, hi_out,
                  x_ref, hb_ref, hr_ref, hi_ref):
    del z_in_ref
    i = pl.program_id(1)
    n = S5_HALF_ST
    ch = S5_BLK_CH
    t_steps = u_ref.shape[0] // 4
    half_rows = 4 * t_steps

    @pl.when(i == 0)
    def _():
        hr_ref[...] = jnp.zeros_like(hr_ref)
        hi_ref[...] = jnp.zeros_like(hi_ref)

    first4 = lax.broadcasted_iota(jnp.int32, (1, 8, ch), 1) < 4
    half0 = lax.broadcasted_iota(jnp.int32, (1, 8, ch), 2) < ch // 2
    for k in range(S5_NB):
        u3 = u_ref[:, k * ch:(k + 1) * ch].reshape(t_steps // 2, 8, ch)
        r3 = pltpu.roll(u3, 4, 1)
        even = jnp.where(first4, jnp.where(half0, u3, 0.0), jnp.where(half0, 0.0, r3))
        odd = jnp.where(first4, jnp.where(half0, r3, 0.0), jnp.where(half0, 0.0, u3))
        lhs = jnp.stack([even, odd], axis=1).reshape(8 * t_steps, ch).astype(BF16)
        for r0 in (0, half_rows):
            x_ref[k, r0:r0 + half_rows, :] = jnp.dot(lhs[r0:r0 + half_rows], b_ref[k],
                                                     preferred_element_type=F32)
    a = [(ar_ref[k], ai_ref[k]) for k in range(S5_NB)]

    def advance(t, hs):
        rows = pl.ds(pl.multiple_of(t * 8, 8), 8)
        out = []
        for k in range(S5_NB):
            dr, di = _cmul(*a[k], *hs[k])
            out.append((dr + x_ref[k, rows, 0:n], di + x_ref[k, rows, n:2 * n]))
        return tuple(out)

    def emit(t2, hs):
        h1 = advance(2 * t2, hs)
        h2 = advance(2 * t2 + 1, h1)
        rows = pl.ds(pl.multiple_of(t2 * 16, 16), 16)
        for k in range(S5_NB):
            hb_ref[k, rows, 0:n] = jnp.concatenate([h1[k][0], h2[k][0]], axis=0).astype(BF16)
            hb_ref[k, rows, n:2 * n] = jnp.concatenate([h1[k][1], h2[k][1]], axis=0).astype(BF16)
        return h2

    hs = lax.fori_loop(0, t_steps // 2, emit, tuple((hr_ref[k], hi_ref[k]) for k in range(S5_NB)), unroll=2)
    for k in range(S5_NB):
        hr_ref[k] = hs[k][0]
        hi_ref[k] = hs[k][1]

    pick = lambda yv: jnp.where(half0, yv, pltpu.roll(yv, 4, 1))
    for k in range(S5_NB):
        y2 = jnp.concatenate([jnp.dot(hb_ref[k, r0:r0 + half_rows, :], c_ref[k], preferred_element_type=F32)
                              for r0 in (0, half_rows)], axis=0)
        y4 = y2.reshape(t_steps // 2, 2, 8, ch)
        y = jnp.where(first4, pick(y4[:, 0]), pltpu.roll(pick(y4[:, 1]), 4, 1)).reshape(4 * t_steps, ch)
        cols = slice(k * ch, (k + 1) * ch)
        z_ref[:, cols] = jax.nn.gelu(y + d_ref[:, cols] * u_ref[:, cols]).astype(z_ref.dtype)

    @pl.when(i == pl.num_programs(1) - 1)
    def _():
        for k in range(S5_NB):
            for h in range(2):
                cols = slice((2 * k + h) * n, (2 * k + h + 1) * n)
                hr_out[:, cols] = hs[k][0][4 * h:4 * h + 4, :]
                hi_out[:, cols] = hs[k][1][4 * h:4 * h + 4, :]


def _s5_prompt_tb(u, z, mats, d, batch, seq):
    assert batch == 4
    b_m, c_m, a_r, a_i = mats
    rows = batch * S5_T
    nt = seq // S5_T
    nb = S5_NB
    blk3 = lambda r, c: pl.BlockSpec((nb, r, c), lambda j, i: (j, 0, 0))
    st_spec = pl.BlockSpec((batch, nb * S5_BLK_ST), lambda j, i: (0, j))
    st_shape = jax.ShapeDtypeStruct((batch, S5_GROUPS * S5_STATE), F32)
    return pl.pallas_call(
        _s5_tb_kernel,
        grid=(S5_NBLK // nb, nt),
        in_specs=[pl.BlockSpec((rows, nb * S5_BLK_CH), lambda j, i: (i, j)),
                  blk3(S5_BLK_CH, S5_BLK_ST), blk3(S5_BLK_ST, S5_BLK_CH),
                  blk3(2 * batch, S5_HALF_ST), blk3(2 * batch, S5_HALF_ST),
                  pl.BlockSpec((1, nb * S5_BLK_CH), lambda j, i: (0, j)),
                  pl.BlockSpec(memory_space=pl.ANY)],
        out_specs=[pl.BlockSpec((rows, nb * S5_BLK_CH), lambda j, i: (i, j)), st_spec, st_spec],
        out_shape=[jax.ShapeDtypeStruct(z.shape, z.dtype), st_shape, st_shape],
        scratch_shapes=[pltpu.VMEM((nb, 2 * rows, S5_BLK_ST), F32), pltpu.VMEM((nb, 2 * rows, S5_BLK_ST), BF16),
                        pltpu.VMEM((nb, 2 * batch, S5_HALF_ST), F32), pltpu.VMEM((nb, 2 * batch, S5_HALF_ST), F32)],
        input_output_aliases={6: 0},
        compiler_params=_cparams(2),
        name="s5_prompt",
    )(u, b_m, c_m, a_r, a_i, d.reshape(1, D_MODEL), z)


def _s5_sample_kernel(u_ref, b_ref, c_ref, ar_ref, ai_ref, d_ref, h0r_ref, h0i_ref,
                      z_in_ref, z_ref, hr_out, hi_out):
    del z_in_ref
    n = S5_HALF_ST
    u = u_ref[...]
    nb = u.shape[0]
    half0 = lax.broadcasted_iota(jnp.int32, u.shape, 1) < S5_BLK_CH // 2
    lhs = jnp.concatenate([jnp.where(half0, u, 0.0), jnp.where(half0, 0.0, u)], axis=0).astype(BF16)
    x = jnp.dot(lhs, b_ref[0], preferred_element_type=F32)
    halves = []
    for h in range(2):
        rows = slice(h * nb, (h + 1) * nb)
        cols = slice(h * n, (h + 1) * n)
        row_a = slice(h * (ar_ref.shape[1] // 2), h * (ar_ref.shape[1] // 2) + 1)
        dr, di = _cmul(ar_ref[0, row_a, :], ai_ref[0, row_a, :], h0r_ref[:, cols], h0i_ref[:, cols])
        hr = x[rows, 0:n] + dr
        hi = x[rows, n:2 * n] + di
        hr_out[:, cols] = hr
        hi_out[:, cols] = hi
        halves.append(jnp.concatenate([hr, hi], axis=1))
    y2 = jnp.dot(jnp.concatenate(halves, axis=0).astype(BF16), c_ref[0], preferred_element_type=F32)
    y = jnp.where(half0, y2[0:nb], y2[nb:2 * nb])
    z_ref[...] = jax.nn.gelu(y + d_ref[...] * u).astype(z_ref.dtype)


def _s5_sample(u, z, mats, d, h0r, h0i, row0):
    b_m, c_m, a_r, a_i = mats
    nb = h0r.shape[0]
    r0 = row0 // nb
    blk3 = lambda r, c: pl.BlockSpec((1, r, c), lambda j: (j, 0, 0))
    st_spec = pl.BlockSpec((nb, S5_BLK_ST), lambda j: (0, j))
    st_shape = jax.ShapeDtypeStruct(h0r.shape, F32)
    return pl.pallas_call(
        _s5_sample_kernel,
        grid=(S5_NBLK,),
        in_specs=[pl.BlockSpec((nb, S5_BLK_CH), lambda j: (r0, j)),
                  blk3(S5_BLK_CH, S5_BLK_ST), blk3(S5_BLK_ST, S5_BLK_CH),
                  blk3(a_r.shape[1], S5_HALF_ST), blk3(a_r.shape[1], S5_HALF_ST),
                  pl.BlockSpec((1, S5_BLK_CH), lambda j: (0, j)),
                  st_spec, st_spec,
                  pl.BlockSpec(memory_space=pl.ANY)],
        out_specs=[pl.BlockSpec((nb, S5_BLK_CH), lambda j: (r0, j)), st_spec, st_spec],
        out_shape=[jax.ShapeDtypeStruct(z.shape, z.dtype), st_shape, st_shape],
        input_output_aliases={8: 0},
        compiler_params=_cparams(1),
        name="s5_sample",
    )(u, b_m, c_m, a_r, a_i, d.reshape(1, D_MODEL), h0r, h0i, z)


TB_T = 128
LANES = 128


def _interleave_norm_kernel(n_steps, n_sample, batch, *refs):
    x_refs = refs[:batch]
    xs_ref, g_ref, xt_ref, ut_ref, slab_ref = refs[batch:]
    i = pl.program_id(0)
    d = xt_ref.shape[1]

    @pl.when(i < n_steps)
    def _():
        for dst_ref, norm in ((xt_ref, False), (ut_ref, True)):
            for b, x_ref in enumerate(x_refs):
                v = x_ref[...]
                if norm:
                    v = _rms(v, g_ref[...])
                for s in range(d // LANES):
                    slab_ref[s, pl.ds(b, TB_T, stride=batch), :] = v[:, s * LANES:(s + 1) * LANES]
            for s in range(d // LANES):
                dst_ref[:, s * LANES:(s + 1) * LANES] = slab_ref[s]

    @pl.when(i >= n_steps)
    def _():
        v = xs_ref[...]
        xt_ref[0:n_sample, :] = v
        ut_ref[0:n_sample, :] = _rms(v, g_ref[...])


def _interleave_norm(x, g, batch, seq):
    m, d = x.shape
    n_steps = seq // TB_T
    ms = m - batch * seq
    rows = batch * TB_T
    per_batch = lambda b: pl.BlockSpec((TB_T, d), lambda i: (b * n_steps + jnp.minimum(i, n_steps - 1), 0))
    out = pl.BlockSpec((rows, d), lambda i: (i, 0))
    return pl.pallas_call(
        functools.partial(_interleave_norm_kernel, n_steps, ms, batch),
        grid=(n_steps + 1,),
        in_specs=[per_batch(b) for b in range(batch)]
        + [pl.BlockSpec((ms, d), lambda i: (batch * seq // ms, 0)), pl.BlockSpec((1, d), lambda i: (0, 0))],
        out_specs=[out, out],
        out_shape=[jax.ShapeDtypeStruct((m, d), F32)] * 2,
        scratch_shapes=[pltpu.VMEM((d // LANES, rows, LANES), F32)],
        compiler_params=_cparams(1),
        name="interleave_norm",
    )(*([x] * (batch + 1)), g.reshape(1, d))


def _deinterleave_norm_kernel(n_steps, n_sample, batch, x_ref, g_ref, y_ref, ys_ref, slab_ref):
    i = pl.program_id(0)
    d = x_ref.shape[1]

    @pl.when(i < n_steps)
    def _():
        y = _rms(x_ref[...], g_ref[...])
        for s in range(d // LANES):
            slab_ref[s] = y[:, s * LANES:(s + 1) * LANES]
        for b in range(batch):
            for s in range(d // LANES):
                y_ref[b, :, s * LANES:(s + 1) * LANES] = slab_ref[s, pl.ds(b, TB_T, stride=batch), :]

    @pl.when(i >= n_steps)
    def _():
        ys_ref[...] = _rms(x_ref[0:n_sample, :], g_ref[...])


def _deinterleave_norm(x, g, batch, seq):
    m, d = x.shape
    n_steps = seq // TB_T
    ms = m - batch * seq
    rows = batch * TB_T
    return pl.pallas_call(
        functools.partial(_deinterleave_norm_kernel, n_steps, ms, batch),
        grid=(n_steps + 1,),
        in_specs=[pl.BlockSpec((rows, d), lambda i: (i, 0)), pl.BlockSpec((1, d), lambda i: (0, 0))],
        out_specs=[pl.BlockSpec((batch, TB_T, d), lambda i: (0, jnp.minimum(i, n_steps - 1), 0)),
                   pl.BlockSpec((ms, d), lambda i: (0, 0))],
        out_shape=[jax.ShapeDtypeStruct((batch, seq, d), F32), jax.ShapeDtypeStruct((ms, d), F32)],
        scratch_shapes=[pltpu.VMEM((d // LANES, rows, LANES), F32)],
        compiler_params=_cparams(1),
        name="deinterleave_norm",
    )(x, g.reshape(1, d))


def _rope_tables(pos):
    inv = ROPE_BASE ** (-jnp.arange(RET_HALF, dtype=F32) / RET_HALF)
    ang = pos[:, None] * inv[None, :]
    return jnp.cos(ang), jnp.sin(ang)


def _ffn(x, norm_g, wg, wu, wd, layer):
    h = _rmsnorm(x, norm_g[layer], [BF16])[0]
    t = _matmul(h, [wg, wu], _epi_swiglu, BF16, "ffn_gate_up", layer=layer)
    return _matmul(t, [wd], _epi_id, F32, "ffn_down", res=x, layer=layer)


def kernel(x_prompt, x_sample, state_ret, state_hgrn, state_s5_re, state_s5_im, attn_norm_g, w_in, ret_gn_g, hg_lb, hg_gn_g, w_out, ssm_norm_g, s5_lam_re, s5_lam_im, s5_log_dt, s5_b_re, s5_b_im, s5_c_re, s5_c_im, s5_d, w_glu_a, w_glu_b, ffn_norm_g, w_ffn_gate, w_ffn_up, w_ffn_down, final_norm_g):
    one = lambda t: t.reshape(t.shape[1:])
    (state_ret, state_hgrn, state_s5_re, state_s5_im, attn_norm_g, w_in, ret_gn_g, hg_gn_g, w_out, ssm_norm_g,
     s5_lam_re, s5_lam_im, s5_log_dt, s5_b_re, s5_b_im, s5_c_re, s5_c_im, s5_d, w_glu_a, w_glu_b) = map(one, (
         state_ret, state_hgrn, state_s5_re, state_s5_im, attn_norm_g, w_in, ret_gn_g, hg_gn_g, w_out, ssm_norm_g,
         s5_lam_re, s5_lam_im, s5_log_dt, s5_b_re, s5_b_im, s5_c_re, s5_c_im, s5_d, w_glu_a, w_glu_b))
    cos_p, sin_p = _rope_tables(jnp.arange(SEQ, dtype=F32))
    cos_s, sin_s = _rope_tables(jnp.full((1,), float(PAST_LEN), F32))

    h, x = _rmsnorm_stack(x_prompt.reshape(N_PROMPT, D_MODEL), x_sample.reshape(DEC_BATCH, D_MODEL), attn_norm_g)
    proj = _matmul(h, [w_in], _epi_id, F32, "w_in")
    mix, ret_p = _ret_prompt(proj, h, cos_p, sin_p, ret_gn_g, BATCH, SEQ)
    mix, hg_p = _hg_prompt(proj, mix, hg_lb, hg_gn_g, 0, BATCH, SEQ)
    mix, ret_s = _ret_sample(proj, mix, state_ret, cos_s, sin_s, ret_gn_g, N_PROMPT)
    mix, hg_s = _hg_sample(proj, mix, state_hgrn, hg_lb, hg_gn_g, 0, N_PROMPT)
    x = _matmul(mix, [w_out], _epi_id, F32, "w_out", res=x)
    z = mix
    x = _ffn(x, ffn_norm_g, w_ffn_gate, w_ffn_up, w_ffn_down, 0)

    x, u = _interleave_norm(x, ssm_norm_g, BATCH, SEQ)
    mats = _s5_mats(s5_lam_re, s5_lam_im, s5_log_dt, s5_b_re, s5_b_im, s5_c_re, s5_c_im, BATCH)
    z, s5r_p, s5i_p = _s5_prompt_tb(u, z, mats, s5_d, BATCH, SEQ)
    n_st = S5_GROUPS * S5_STATE
    z, s5r_s, s5i_s = _s5_sample(u, z, mats, s5_d, state_s5_re.reshape(DEC_BATCH, n_st),
                                 state_s5_im.reshape(DEC_BATCH, n_st), N_PROMPT)
    x = _matmul(z, [w_glu_a, w_glu_b], _epi_glu, F32, "glu", res=x)
    x = _ffn(x, ffn_norm_g, w_ffn_gate, w_ffn_up, w_ffn_down, 1)

    y_p, y_s = _deinterleave_norm(x, final_norm_g, BATCH, SEQ)
    st = lambda t, b: t.reshape(1, b, S5_GROUPS, S5_STATE)
    return (y_p, y_s.reshape(DEC_BATCH, 1, D_MODEL),
            ret_p[None], ret_s[None], hg_p[None], hg_s[None],
            st(s5r_p, BATCH), st(s5i_p, BATCH), st(s5r_s, DEC_BATCH), st(s5i_s, DEC_BATCH))
```
